```python
import jax, jax.numpy as jnp
from jax import lax
import numpy as np

D_MODEL = 1024
BATCH = 4
SEQ = 8192
DEPTH = 2
DEC_BATCH = 32
DEC_SEQ = 4
PAST_LEN = 16384
PAGE_SIZE = 128

A_WIDTH = D_MODEL // 2
A_GROUPS = 4
A_GROUP_DIM = A_WIDTH // A_GROUPS
CHUNK = 128
B_WIDTH = D_MODEL - A_WIDTH
N_HEADS = 8
HEAD_DIM = B_WIDTH // N_HEADS
N_KV_HEADS = 2
GQA = N_HEADS // N_KV_HEADS
KV_WIDTH = N_KV_HEADS * HEAD_DIM
CMP_LEN = 32
CMP_STRIDE = 16
CMP_HIDDEN = 128
SEL_LEN = 64
N_SEL = 16
WINDOW = 512
Q_BLOCK = 128
ROPE_THETA = 10000.0
SCALE = HEAD_DIM ** -0.5
IN_WIDTH = 2 * A_WIDTH + B_WIDTH + 6 * KV_WIDTH + 3 * N_HEADS
D_FF = 2816
N_EXPERTS = 8
TOP_K = 2
N_DENSE = (DEPTH + 1) // 2
N_MOE = DEPTH // 2
EPS = 1e-6
NEG = -1e30
BIG = 1e30
TINY = 1e-30

kernel_name = 'hymba_sgu_nsa_decoder_step'


def _rmsnorm(x, g):
    xf = x.astype(jnp.float32)
    y = xf * lax.rsqrt(jnp.mean(xf * xf, axis=-1, keepdims=True) + EPS)
    return (y * g.astype(jnp.float32)).astype(x.dtype)


def _layernorm(x, g):
    xf = x.astype(jnp.float32)
    mu = jnp.mean(xf, axis=-1, keepdims=True)
    var = jnp.mean(jnp.square(xf - mu), axis=-1, keepdims=True)
    return ((xf - mu) * lax.rsqrt(var + EPS) * g.astype(jnp.float32)).astype(x.dtype)


def _rope(x, pos):
    half = HEAD_DIM // 2
    inv = ROPE_THETA ** (-jnp.arange(half, dtype=jnp.float32) / half)
    ang = pos.astype(jnp.float32)[:, None] * inv[None, :]
    cos = jnp.cos(ang)[None, :, None, :]
    sin = jnp.sin(ang)[None, :, None, :]
    xf = x.astype(jnp.float32)
    x1, x2 = xf[..., :half], xf[..., half:]
    return jnp.concatenate([x1 * cos - x2 * sin, x2 * cos + x1 * sin], axis=-1).astype(x.dtype)


def _masked_softmax(s, mask):
    s = jnp.where(mask, s, NEG)
    m = jnp.max(s, axis=-1, keepdims=True)
    p = jnp.where(mask, jnp.exp(s - m), 0.0)
    return p / jnp.maximum(jnp.sum(p, axis=-1, keepdims=True), TINY)


def _project(x, pos, g_mix, w_in, g_sgu):
    B_, T, _ = x.shape
    h = _rmsnorm(x, g_mix)
    z = jnp.einsum('btd,de->bte', h, w_in)
    sizes = (A_WIDTH, A_WIDTH, B_WIDTH) + (KV_WIDTH,) * 6 + (3 * N_HEADS,)
    cuts = [int(c) for c in np.cumsum(sizes)[:-1]]
    zu, zv, zq, zkc, zvc, zks, zvs, zkw, zvw, zg = jnp.split(z, cuts, axis=-1)
    u = jax.nn.gelu(zu)
    v = _layernorm(jax.nn.gelu(zv), g_sgu)
    q = _rope(zq.reshape(B_, T, N_HEADS, HEAD_DIM), pos)
    kv = lambda t: t.reshape(B_, T, N_KV_HEADS, HEAD_DIM)
    kc, vc = _rope(kv(zkc), pos), kv(zvc)
    ks, vs = _rope(kv(zks), pos), kv(zvs)
    kw, vw = _rope(kv(zkw), pos), kv(zvw)
    gates = jax.nn.sigmoid(zg.astype(jnp.float32)).reshape(B_, T, 3, N_HEADS).astype(x.dtype)
    return u, v, q, kc, vc, ks, vs, kw, vw, gates


def _chunk_sgu(u, v, w_s, b_s):
    B_, T, _ = v.shape
    L = min(CHUNK, T)
    vc = v.reshape(B_, T // L, L, A_GROUPS, A_GROUP_DIM)
    causal = jnp.tril(jnp.ones((L, L), dtype=bool))
    w = jnp.where(causal[None], w_s[:, :L, :L], 0.0).astype(v.dtype)
    mixed = jnp.einsum('gts,bcsgd->bctgd', w, vc) + jnp.transpose(b_s[:, :L])[None, None, :, :, None]
    return u * mixed.reshape(B_, T, A_WIDTH)


def _compress(k, w1, w2, pe):
    B_, Tk = k.shape[:2]
    n_cmp = (Tk - CMP_LEN) // CMP_STRIDE + 1
    nch = -(-Tk // CMP_STRIDE)
    kp = jnp.pad(k, ((0, 0), (0, nch * CMP_STRIDE - Tk), (0, 0), (0, 0)))
    kc = kp.reshape(B_, nch, CMP_STRIDE, N_KV_HEADS, HEAD_DIM)
    w1r = w1.reshape(CMP_LEN // CMP_STRIDE, CMP_STRIDE, HEAD_DIM, CMP_HIDDEN)
    h = jnp.einsum('ld,lde->e', pe, w1.reshape(CMP_LEN, HEAD_DIM, CMP_HIDDEN))
    for r in range(CMP_LEN // CMP_STRIDE):
        h = h + jnp.einsum('bnshd,sde->bnhe', kc[:, r:r + n_cmp], w1r[r])
    return jnp.einsum('bnhe,ed->bnhd', jax.nn.gelu(h), w2)


def _sel_blocks(k):
    B_, Tk = k.shape[:2]
    n_sel = -(-Tk // SEL_LEN)
    kp = jnp.pad(k, ((0, 0), (0, n_sel * SEL_LEN - Tk), (0, 0), (0, 0)))
    return kp.reshape(B_, n_sel, SEL_LEN, N_KV_HEADS, HEAD_DIM).transpose(0, 3, 1, 2, 4)


def _cmp_to_sel(n_cmp, n_sel):
    i0 = jnp.arange(n_cmp, dtype=jnp.int32)[:, None] * CMP_STRIDE
    j0 = jnp.arange(n_sel, dtype=jnp.int32)[None, :] * SEL_LEN
    ov = jnp.clip(jnp.minimum(i0 + CMP_LEN, j0 + SEL_LEN) - jnp.maximum(i0, j0), 0, CMP_LEN)
    return ov.astype(jnp.float32) / CMP_LEN


def _cmp_sel_block(q, t, ck, cv, m, ksb, vsb):
    B_, QB = q.shape[:2]
    n_cmp, n_sel = ck.shape[1], ksb.shape[2]
    qg = q.reshape(B_, QB, N_KV_HEADS, GQA, HEAD_DIM)
    s = jnp.einsum('bqhgd,bnhd->bhgqn', qg, ck).astype(jnp.float32) * SCALE
    cmp_end = jnp.arange(n_cmp, dtype=jnp.int32) * CMP_STRIDE + CMP_LEN - 1
    p = _masked_softmax(s, (cmp_end[None, :] <= t[:, None])[None, None, None])
    o_cmp = jnp.einsum('bhgqn,bnhd->bqhgd', p.astype(cv.dtype), cv)
    imp = jnp.einsum('bhgqn,nj->bhqj', p, m)
    blk = jnp.arange(n_sel, dtype=jnp.int32)[None, :]
    cur = (t // SEL_LEN)[:, None]
    forced = (blk == 0) | (blk == cur) | (blk == cur - 1)
    score = jnp.where(blk <= cur, jnp.where(forced, BIG, imp), NEG)
    k_top = min(N_SEL, n_sel)
    top_val, idx = lax.top_k(score, k_top)
    valid = top_val > NEG / 2
    bi = jnp.arange(B_)[:, None, None, None]
    hi = jnp.arange(N_KV_HEADS)[None, :, None, None]
    kg = ksb[bi, hi, idx].reshape(B_, N_KV_HEADS, QB, k_top * SEL_LEN, HEAD_DIM)
    vg = vsb[bi, hi, idx].reshape(B_, N_KV_HEADS, QB, k_top * SEL_LEN, HEAD_DIM)
    kpos = (idx[..., None] * SEL_LEN + jnp.arange(SEL_LEN, dtype=jnp.int32)).reshape(B_, N_KV_HEADS, QB, k_top * SEL_LEN)
    smask = (kpos <= t[None, None, :, None]) & jnp.repeat(valid, SEL_LEN, axis=-1)
    s2 = jnp.einsum('bqhgd,bhqkd->bhgqk', qg, kg).astype(jnp.float32) * SCALE
    p2 = _masked_softmax(s2, smask[:, :, None])
    o_sel = jnp.einsum('bhgqk,bhqkd->bqhgd', p2.astype(vg.dtype), vg)
    return o_cmp.reshape(B_, QB, N_HEADS, HEAD_DIM), o_sel.reshape(B_, QB, N_HEADS, HEAD_DIM)


def _nsa_cmp_sel(q, q_pos, kc, vc, ks, vs, w_ck1, w_ck2, pe_ck, w_cv1, w_cv2, pe_cv):
    ck = _compress(kc, w_ck1, w_ck2, pe_ck)
    cv = _compress(vc, w_cv1, w_cv2, pe_cv)
    ksb, vsb = _sel_blocks(ks), _sel_blocks(vs)
    m = _cmp_to_sel(ck.shape[1], ksb.shape[2])
    B_, T = q.shape[:2]
    qb_len = Q_BLOCK if T % Q_BLOCK == 0 else T
    nb = T // qb_len
    qb = q.reshape(B_, nb, qb_len, N_HEADS, HEAD_DIM).transpose(1, 0, 2, 3, 4)
    tb = q_pos.reshape(nb, qb_len)
    o_c, o_s = lax.map(lambda a: _cmp_sel_block(a[0], a[1], ck, cv, m, ksb, vsb), (qb, tb))
    unblock = lambda o: o.transpose(1, 0, 2, 3, 4).reshape(B_, T, N_HEADS, HEAD_DIM)
    return unblock(o_c), unblock(o_s)


def _window_block(q, t, k, v, kpos):
    B_, QB = q.shape[:2]
    qg = q.reshape(B_, QB, N_KV_HEADS, GQA, HEAD_DIM)
    s = jnp.einsum('bqhgd,bkhd->bhgqk', qg, k).astype(jnp.float32) * SCALE
    mask = (kpos[None, :] <= t[:, None]) & (kpos[None, :] > t[:, None] - WINDOW) & (kpos[None, :] >= 0)
    p = _masked_softmax(s, mask[None, None, None])
    o = jnp.einsum('bhgqk,bkhd->bqhgd', p.astype(v.dtype), v)
    return o.reshape(B_, QB, N_HEADS, HEAD_DIM)


def _window_banded(q, k, v):
    B_, T = q.shape[:2]
    nb = T // Q_BLOCK
    pad = ((0, 0), (WINDOW, 0), (0, 0), (0, 0))
    kp, vp = jnp.pad(k, pad), jnp.pad(v, pad)
    span = WINDOW + Q_BLOCK

    def block(b):
        start = b * Q_BLOCK
        qb = lax.dynamic_slice_in_dim(q, start, Q_BLOCK, axis=1)
        kb = lax.dynamic_slice_in_dim(kp, start, span, axis=1)
        vb = lax.dynamic_slice_in_dim(vp, start, span, axis=1)
        t = start + jnp.arange(Q_BLOCK, dtype=jnp.int32)
        kpos = start - WINDOW + jnp.arange(span, dtype=jnp.int32)
        return _window_block(qb, t, kb, vb, kpos)

    o = lax.map(block, jnp.arange(nb, dtype=jnp.int32))
    return o.transpose(1, 0, 2, 3, 4).reshape(B_, T, N_HEADS, HEAD_DIM)


def _combine(x, u, v, w_s, b_s, o_c, o_s, o_w, gates, w_o):
    B_, T, _ = x.shape
    a = _chunk_sgu(u, v, w_s, b_s)
    o = gates[:, :, 0, :, None] * o_c + gates[:, :, 1, :, None] * o_s + gates[:, :, 2, :, None] * o_w
    mix = jnp.concatenate([a, o.reshape(B_, T, B_WIDTH)], axis=-1)
    return x + jnp.einsum('bte,ed->btd', mix, w_o)


def _swiglu(h, wg, wu, wd):
    return jnp.einsum('btf,fd->btd', jax.nn.silu(jnp.einsum('btd,df->btf', h, wg)) * jnp.einsum('btd,df->btf', h, wu), wd)


def _moe(h, w_router, wg, wu, wd):
    logits = jnp.einsum('btd,de->bte', h, w_router).astype(jnp.float32)
    top_v, top_i = lax.top_k(logits, TOP_K)
    wts = jax.nn.softmax(top_v, axis=-1)
    gate = jnp.sum(jax.nn.one_hot(top_i, N_EXPERTS, dtype=jnp.float32) * wts[..., None], axis=-2)
    y = jnp.zeros_like(h)
    for e in range(N_EXPERTS):
        y = y + gate[..., e:e + 1].astype(h.dtype) * _swiglu(h, wg[e], wu[e], wd[e])
    return y


def setup_inputs(seed: int = 0) -> dict:
    key = jax.random.key(seed)
    ks = jax.random.split(key, 32)
    n_pages = PAST_LEN // PAGE_SIZE
    n_pool = (DEC_BATCH * n_pages * 5) // 4
    wbuf = min(WINDOW, PAST_LEN)
    f32 = jnp.float32
    nrm = lambda k, shape, sc: jax.random.normal(k, shape, f32) * sc
    gain = lambda k, shape: 1.0 + 0.05 * jax.random.normal(k, shape, f32)
    paged = (DEPTH, n_pool, PAGE_SIZE, N_KV_HEADS, HEAD_DIM)
    win = (DEPTH, DEC_BATCH, wbuf, N_KV_HEADS, HEAD_DIM)
    page_table = jax.random.permutation(ks[8], n_pool)[:DEC_BATCH * n_pages].reshape(DEC_BATCH, n_pages).astype(jnp.int32)
    return {
        'x_prompt': nrm(ks[0], (BATCH, SEQ, D_MODEL), 1.0),
        'x_sample': nrm(ks[1], (DEC_BATCH, DEC_SEQ, D_MODEL), 1.0),
        'cache_cmp_k': nrm(ks[2], paged, 1.0),
        'cache_cmp_v': nrm(ks[3], paged, 1.0),
        'cache_sel_k': nrm(ks[4], paged, 1.0),
        'cache_sel_v': nrm(ks[5], paged, 1.0),
        'cache_win_k': nrm(ks[6], win, 1.0),
        'cache_win_v': nrm(ks[7], win, 1.0),
        'page_table': page_table,
        'g_mix': gain(ks[9], (DEPTH, D_MODEL)),
        'w_in': nrm(ks[10], (DEPTH, D_MODEL, IN_WIDTH), D_MODEL ** -0.5),
        'g_sgu': gain(ks[11], (DEPTH, A_WIDTH)),
        'w_sgu': nrm(ks[12], (DEPTH, A_GROUPS, CHUNK, CHUNK), CHUNK ** -0.5),
        'b_sgu': 1.0 + 0.1 * jax.random.normal(ks[13], (DEPTH, A_GROUPS, CHUNK), f32),
        'w_cmpk1': nrm(ks[14], (DEPTH, CMP_LEN * HEAD_DIM, CMP_HIDDEN), (CMP_LEN * HEAD_DIM) ** -0.5),
        'w_cmpk2': nrm(ks[15], (DEPTH, CMP_HIDDEN, HEAD_DIM), CMP_HIDDEN ** -0.5),
        'pe_cmpk': nrm(ks[16], (DEPTH, CMP_LEN, HEAD_DIM), 0.5),
        'w_cmpv1': nrm(ks[17], (DEPTH, CMP_LEN * HEAD_DIM, CMP_HIDDEN), (CMP_LEN * HEAD_DIM) ** -0.5),
        'w_cmpv2': nrm(ks[18], (DEPTH, CMP_HIDDEN, HEAD_DIM), CMP_HIDDEN ** -0.5),
        'pe_cmpv': nrm(ks[19], (DEPTH, CMP_LEN, HEAD_DIM), 0.5),
        'w_o': nrm(ks[20], (DEPTH, D_MODEL, D_MODEL), D_MODEL ** -0.5),
        'g_ffn': gain(ks[21], (DEPTH, D_MODEL)),
        'w_ff_gate': nrm(ks[22], (N_DENSE, D_MODEL, D_FF), D_MODEL ** -0.5),
        'w_ff_up': nrm(ks[23], (N_DENSE, D_MODEL, D_FF), D_MODEL ** -0.5),
        'w_ff_down': nrm(ks[24], (N_DENSE, D_FF, D_MODEL), D_FF ** -0.5),
        'w_router': nrm(ks[25], (N_MOE, D_MODEL, N_EXPERTS), D_MODEL ** -0.5),
        'w_moe_gate': nrm(ks[26], (N_MOE, N_EXPERTS, D_MODEL, D_FF), D_MODEL ** -0.5),
        'w_moe_up': nrm(ks[27], (N_MOE, N_EXPERTS, D_MODEL, D_FF), D_MODEL ** -0.5),
        'w_moe_down': nrm(ks[28], (N_MOE, N_EXPERTS, D_FF, D_MODEL), D_FF ** -0.5),
        'g_final': gain(ks[29], (D_MODEL,)),
    }


def reference(x_prompt, x_sample, cache_cmp_k, cache_cmp_v, cache_sel_k, cache_sel_v, cache_win_k, cache_win_v, page_table,
              g_mix, w_in, g_sgu, w_sgu, b_sgu, w_cmpk1, w_cmpk2, pe_cmpk, w_cmpv1, w_cmpv2, pe_cmpv, w_o, g_ffn,
              w_ff_gate, w_ff_up, w_ff_down, w_router, w_moe_gate, w_moe_up, w_moe_down, g_final):
    seq = x_prompt.shape[1]
    dec_b, dec_t = x_sample.shape[:2]
    past_len = page_table.shape[1] * cache_cmp_k.shape[2]
    wbuf = cache_win_k.shape[2]
    pos_p = jnp.arange(seq, dtype=jnp.int32)
    pos_s = past_len + jnp.arange(dec_t, dtype=jnp.int32)
    win_pos_s = past_len - wbuf + jnp.arange(wbuf + dec_t, dtype=jnp.int32)
    nwin_p = min(WINDOW, seq)
    nwin_s = min(WINDOW, wbuf + dec_t)

    def gather_past(pool):
        return pool[page_table].reshape(dec_b, past_len, N_KV_HEADS, HEAD_DIM)

    xp, xs = x_prompt, x_sample
    p_ck, p_cv, p_sk, p_sv, p_wk, p_wv = [], [], [], [], [], []
    s_ck, s_cv, s_sk, s_sv, s_wk, s_wv, s_u = [], [], [], [], [], [], []
    for l in range(DEPTH):
        cmpw = (w_cmpk1[l], w_cmpk2[l], pe_cmpk[l], w_cmpv1[l], w_cmpv2[l], pe_cmpv[l])
        u, v, q, kc, vc, ks_, vs_, kw, vw, gates = _project(xp, pos_p, g_mix[l], w_in[l], g_sgu[l])
        o_c, o_s = _nsa_cmp_sel(q, pos_p, kc, vc, ks_, vs_, *cmpw)
        o_w = _window_banded(q, kw, vw)
        xp = _combine(xp, u, v, w_sgu[l], b_sgu[l], o_c, o_s, o_w, gates, w_o[l])
        p_ck.append(kc); p_cv.append(vc); p_sk.append(ks_); p_sv.append(vs_)
        p_wk.append(kw[:, seq - nwin_p:]); p_wv.append(vw[:, seq - nwin_p:])
        u, v, q, kc, vc, ks_, vs_, kw, vw, gates = _project(xs, pos_s, g_mix[l], w_in[l], g_sgu[l])
        kc_all = jnp.concatenate([gather_past(cache_cmp_k[l]), kc], axis=1)
        vc_all = jnp.concatenate([gather_past(cache_cmp_v[l]), vc], axis=1)
        ks_all = jnp.concatenate([gather_past(cache_sel_k[l]), ks_], axis=1)
        vs_all = jnp.concatenate([gather_past(cache_sel_v[l]), vs_], axis=1)
        o_c, o_s = _nsa_cmp_sel(q, pos_s, kc_all, vc_all, ks_all, vs_all, *cmpw)
        kw_all = jnp.concatenate([cache_win_k[l], kw], axis=1)
        vw_all = jnp.concatenate([cache_win_v[l], vw], axis=1)
        o_w = _window_block(q, pos_s, kw_all, vw_all, win_pos_s)
        xs = _combine(xs, u, v, w_sgu[l], b_sgu[l], o_c, o_s, o_w, gates, w_o[l])
        s_ck.append(kc); s_cv.append(vc); s_sk.append(ks_); s_sv.append(vs_)
        s_wk.append(kw_all[:, wbuf + dec_t - nwin_s:]); s_wv.append(vw_all[:, wbuf + dec_t - nwin_s:])
        s_u.append(v)
        hp, hs = _rmsnorm(xp, g_ffn[l]), _rmsnorm(xs, g_ffn[l])
        i = l // 2
        if l % 2 == 0:
            xp = xp + _swiglu(hp, w_ff_gate[i], w_ff_up[i], w_ff_down[i])
            xs = xs + _swiglu(hs, w_ff_gate[i], w_ff_up[i], w_ff_down[i])
        else:
            xp = xp + _moe(hp, w_router[i], w_moe_gate[i], w_moe_up[i], w_moe_down[i])
            xs = xs + _moe(hs, w_router[i], w_moe_gate[i], w_moe_up[i], w_moe_down[i])
    y_prompt = _rmsnorm(xp, g_final)
    y_sample = _rmsnorm(xs, g_final)
    st = lambda a: jnp.stack(a, axis=0)
    return (y_prompt, y_sample,
            st(p_ck), st(p_cv), st(p_sk), st(p_sv), st(p_wk), st(p_wv),
            st(s_ck), st(s_cv), st(s_sk), st(s_sv), st(s_wk), st(s_wv), st(s_u))
```

```python
import functools
import math

import numpy as np
import jax
import jax.numpy as jnp
from jax import lax
from jax.experimental import pallas as pl
from jax.experimental.pallas import tpu as pltpu

F32 = jnp.float32
BF16 = jnp.bfloat16
I32 = jnp.int32

A_GROUPS = 4
N_HEADS = 8
N_KV_HEADS = 2
HEAD_DIM = 64
CMP_LEN = 32
CMP_STRIDE = 16
CMP_HIDDEN = 128
SEL_LEN = 64
N_SEL = 16
WINDOW = 512
CHUNK = 128
ROPE_THETA = 10000.0
TOP_K = 2
EPS = 1e-6
NEG = -1e30
BIG = 1e30
TINY = 1e-30
SCALE = HEAD_DIM ** -0.5

LANE = 128
KV_WIDTH = N_KV_HEADS * HEAD_DIM
GQA = N_HEADS // N_KV_HEADS
VMEM_LIMIT = 56 * 1024 * 1024


def _cparams(sem, vmem=None):
    return pltpu.CompilerParams(dimension_semantics=sem, vmem_limit_bytes=vmem)


def _rms(xf, g):
    return xf * lax.rsqrt(jnp.mean(xf * xf, axis=-1, keepdims=True) + EPS) * g


def _gelu(x):
    c = math.sqrt(2.0 / math.pi)
    return x * (0.5 * (1.0 + jnp.tanh(c * (x + 0.044715 * (x * x * x)))))


def _sigmoid(x):
    return 1.0 / (1.0 + jnp.exp(-x))


def _dot(a, b):
    return jnp.dot(a, b, preferred_element_type=F32)


def _dot_nt(a, b):
    return lax.dot_general(a, b, (((1,), (1,)), ((), ())), preferred_element_type=F32)


def _split_bf16(x):
    hi = x.astype(BF16)
    lo = (x - hi.astype(F32)).astype(BF16)
    return hi, lo


def _inproj_kernel(x_ref, gmix_ref, wstd_ref, wt_ref, gsgu_ref, wsgu_ref, bsgu_ref,
                   cos_ref, sin_ref, cost_ref, sint_ref,
                   a_ref, v_ref, kc_ref, vc_ref, ks_ref, vs_ref, kw_ref, vw_ref, q_ref, gt_ref,
                   *, tm, emit_qblk):
    xf = x_ref[...]
    h = _rms(xf, gmix_ref[...]).astype(BF16)
    z = _dot(h, wstd_ref[...])
    zt = _dot_nt(wt_ref[...], h)
    aw = A_GROUPS * CHUNK
    u = _gelu(z[:, 0:aw])
    vv = _gelu(z[:, aw:2 * aw])
    mu = jnp.mean(vv, axis=-1, keepdims=True)
    d = vv - mu
    var = jnp.mean(d * d, axis=-1, keepdims=True)
    v = d * lax.rsqrt(var + EPS) * gsgu_ref[...]
    v_ref[...] = v
    vb = v.astype(BF16)
    nc = tm // CHUNK
    for g in range(A_GROUPS):
        gs = slice(g * CHUNK, (g + 1) * CHUNK)
        parts = [vb[c * CHUNK:(c + 1) * CHUNK, gs] for c in range(nc)]
        xg = parts[0] if nc == 1 else jnp.concatenate(parts, axis=1)
        yg = _dot(wsgu_ref[g], xg)
        for c in range(nc):
            cs = slice(c * CHUNK, (c + 1) * CHUNK)
            mixed = yg[:, cs] + bsgu_ref[g]
            a_ref[cs, gs] = (u[cs, gs] * mixed).astype(BF16)

    cosr = cos_ref[...]
    sinr = sin_ref[...]
    lane = lax.broadcasted_iota(I32, (tm, KV_WIDTH), 1)
    first = (lane % HEAD_DIM) < (HEAD_DIM // 2)

    def rope(x):
        rot = jnp.where(first, pltpu.roll(x, KV_WIDTH - HEAD_DIM // 2, 1), pltpu.roll(x, HEAD_DIM // 2, 1))
        return x * cosr + rot * sinr

    o = 2 * aw
    kc_ref[...] = rope(z[:, o:o + 128])
    vc_ref[...] = z[:, o + 128:o + 256]
    ks_ref[...] = rope(z[:, o + 256:o + 384])
    vs_ref[...] = z[:, o + 384:o + 512]
    kw_ref[...] = rope(z[:, o + 512:o + 640])
    vw_ref[...] = z[:, o + 640:o + 768]

    ct = cost_ref[...]
    st = sint_ref[...]
    half = HEAD_DIM // 2
    for hd in range(N_HEADS):
        x1 = zt[HEAD_DIM * hd:HEAD_DIM * hd + half]
        x2 = zt[HEAD_DIM * hd + half:HEAD_DIM * (hd + 1)]
        qh = jnp.concatenate([(x1 * ct - x2 * st) * SCALE, (x2 * ct + x1 * st) * SCALE], axis=0).astype(BF16)
        if emit_qblk:
            kvh = hd // GQA
            zero = jnp.zeros((HEAD_DIM, LANE), BF16)
            for j in range(tm // LANE):
                ls = slice(hd * LANE, (hd + 1) * LANE)
                q_ref[j, HEAD_DIM * kvh:HEAD_DIM * (kvh + 1), ls] = qh[:, j * LANE:(j + 1) * LANE]
                q_ref[j, HEAD_DIM * (1 - kvh):HEAD_DIM * (2 - kvh), ls] = zero
        else:
            q_ref[HEAD_DIM * hd:HEAD_DIM * (hd + 1), :] = qh
    nq = N_HEADS * HEAD_DIM
    gt_ref[...] = _sigmoid(zt[nq:nq + 32])


def _inproj(x, gmix, wstd, wt, gsgu, wsgu, bsgu, cos, sin, cost, sint, *, tm, emit_qblk):
    n, dm = x.shape
    nt = n // tm
    row = lambda w: pl.BlockSpec((tm, w), lambda i: (i, 0))
    full = lambda a: pl.BlockSpec(a.shape, lambda i: (0,) * a.ndim)
    if emit_qblk:
        q_shape = jax.ShapeDtypeStruct((n // LANE, KV_WIDTH, N_HEADS * LANE), BF16)
        q_spec = pl.BlockSpec((tm // LANE, KV_WIDTH, N_HEADS * LANE), lambda i: (i, 0, 0))
    else:
        q_shape = jax.ShapeDtypeStruct((N_HEADS * HEAD_DIM, n), BF16)
        q_spec = pl.BlockSpec((N_HEADS * HEAD_DIM, tm), lambda i: (0, i))
    kv = jax.ShapeDtypeStruct((n, KV_WIDTH), F32)
    out_shape = (jax.ShapeDtypeStruct((n, 512), BF16), jax.ShapeDtypeStruct((n, 512), F32),
                 kv, kv, kv, kv, kv, kv, q_shape, jax.ShapeDtypeStruct((32, n), F32))
    out_specs = (row(512), row(512), row(128), row(128), row(128), row(128), row(128), row(128), q_spec,
                 pl.BlockSpec((32, tm), lambda i: (0, i)))
    in_specs = [row(dm), full(gmix), full(wstd), full(wt), full(gsgu), full(wsgu), full(bsgu),
                row(128), row(128), pl.BlockSpec((32, tm), lambda i: (0, i)), pl.BlockSpec((32, tm), lambda i: (0, i))]
    return pl.pallas_call(
        functools.partial(_inproj_kernel, tm=tm, emit_qblk=emit_qblk),
        grid=(nt,), in_specs=in_specs, out_specs=out_specs, out_shape=out_shape,
        compiler_params=_cparams(("parallel",), VMEM_LIMIT), name="inproj",
    )(x, gmix, wstd, wt, gsgu, wsgu, bsgu, cos, sin, cost, sint)


def _compress_kernel(c_ref, wbig_ref, pe_ref, w1_ref, w2_ref, o_ref, *, transpose_out):
    c = c_ref[0].astype(BF16)
    ab = _dot(c, wbig_ref[...])
    nch = ab.shape[0]
    hw = N_KV_HEADS * CMP_HIDDEN
    peb = _dot(pe_ref[...], w1_ref[...])
    bias = jnp.concatenate([peb[0:1]] * N_KV_HEADS, axis=1)
    hh = ab[:, :hw] + pltpu.roll(ab[:, hw:], nch - 1, 0) + bias
    g = _gelu(hh).astype(BF16)
    if transpose_out:
        o_ref[0] = _dot_nt(w2_ref[...], g).astype(BF16)
    else:
        o_ref[0] = _dot(g, w2_ref[...]).astype(BF16)


def _compress(c, wbig, pe8, w1, w2, *, transpose_out):
    b, nch, cw = c.shape
    if transpose_out:
        out_shape = jax.ShapeDtypeStruct((b, KV_WIDTH, nch), BF16)
        out_spec = pl.BlockSpec((1, KV_WIDTH, nch), lambda i: (i, 0, 0))
    else:
        out_shape = jax.ShapeDtypeStruct((b, nch, KV_WIDTH), BF16)
        out_spec = pl.BlockSpec((1, nch, KV_WIDTH), lambda i: (i, 0, 0))
    full = lambda a: pl.BlockSpec(a.shape, lambda i: (0,) * a.ndim)
    return pl.pallas_call(
        functools.partial(_compress_kernel, transpose_out=transpose_out),
        grid=(b,), in_specs=[pl.BlockSpec((1, nch, cw), lambda i: (i, 0, 0)), full(wbig), full(pe8), full(w1), full(w2)],
        out_specs=out_spec, out_shape=out_shape,
        compiler_params=_cparams(("parallel",), VMEM_LIMIT), name="compress",
    )(c, wbig, pe8, w1, w2)


def _cmp_select_kernel(q_ref, ck_ref, cvt_ref, mt_ref, t_ref, oc_ref, qaug_ref, *, tq):
    L = N_HEADS * tq
    hl = L // N_KV_HEADS
    q1 = q_ref[0]
    t = t_ref[0]
    s = _dot(ck_ref[0], q1)
    nch = s.shape[0]
    n_iota = lax.broadcasted_iota(I32, (nch, L), 0)
    mask = (n_iota * CMP_STRIDE + (CMP_LEN - 1)) <= t
    s = jnp.where(mask, s, NEG)
    m = jnp.max(s, axis=0, keepdims=True)
    p = jnp.where(mask, jnp.exp(s - m), 0.0)
    p = p / jnp.maximum(jnp.sum(p, axis=0, keepdims=True), TINY)
    pb = p.astype(BF16)
    mt = mt_ref[...]
    nsel = mt.shape[0]
    blk = lax.broadcasted_iota(I32, (nsel, tq), 0).astype(F32)
    cur = (t[:, 0:tq] // SEL_LEN).astype(F32)
    biases = []
    for h in range(N_KV_HEADS):
        o_h = _dot(cvt_ref[0, HEAD_DIM * h:HEAD_DIM * (h + 1), :], pb[:, h * hl:(h + 1) * hl])
        psum = p[:, h * hl:h * hl + tq]
        for g in range(GQA):
            hd = h * GQA + g
            oc_ref[0, HEAD_DIM * hd:HEAD_DIM * (hd + 1), :] = o_h[:, g * tq:(g + 1) * tq]
            if g > 0:
                psum = psum + p[:, hd * tq:(hd + 1) * tq]
        hi, lo = _split_bf16(psum)
        imp = _dot(mt, hi) + _dot(mt, lo)
        forced = (blk == 0.0) | (blk == cur) | (blk == cur - 1.0)
        sc = jnp.where(blk <= cur, jnp.where(forced, BIG, imp), NEG)
        sel = jnp.zeros((nsel, tq), jnp.bool_)
        for _ in range(N_SEL):
            mx = jnp.max(sc, axis=0, keepdims=True)
            idx = jnp.min(jnp.where(sc == mx, blk, 1e9), axis=0, keepdims=True)
            hit = blk == idx
            sel = sel | (hit & (mx > NEG / 2))
            sc = jnp.where(hit, -3e38, sc)
        bias_h = jnp.where(sel, 0.0, NEG).astype(BF16)
        biases.extend([bias_h] * GQA)
    qaug_ref[0, 0:KV_WIDTH, :] = q1
    qaug_ref[0, KV_WIDTH:, :] = jnp.concatenate(biases, axis=1)


def _cmp_select(qblk, ck, cvt, mt, tl, *, tq, nq_per_b):
    nq, _, L = qblk.shape
    b, nch, _ = ck.shape
    nsel = mt.shape[0]
    shared_t = tl.shape[0] == 1
    return pl.pallas_call(
        functools.partial(_cmp_select_kernel, tq=tq),
        grid=(b, nq_per_b),
        in_specs=[pl.BlockSpec((1, KV_WIDTH, L), lambda i, j: (i * nq_per_b + j, 0, 0)),
                  pl.BlockSpec((1, nch, KV_WIDTH), lambda i, j: (i, 0, 0)),
                  pl.BlockSpec((1, KV_WIDTH, nch), lambda i, j: (i, 0, 0)),
                  pl.BlockSpec(mt.shape, lambda i, j: (0, 0)),
                  pl.BlockSpec((1, 1, L), (lambda i, j: (0, 0, 0)) if shared_t else (lambda i, j: (j, 0, 0)))],
        out_specs=(pl.BlockSpec((1, N_HEADS * HEAD_DIM, tq), lambda i, j: (i * nq_per_b + j, 0, 0)),
                   pl.BlockSpec((1, KV_WIDTH + nsel, L), lambda i, j: (i * nq_per_b + j, 0, 0))),
        out_shape=(jax.ShapeDtypeStruct((nq, N_HEADS * HEAD_DIM, tq), F32),
                   jax.ShapeDtypeStruct((nq, KV_WIDTH + nsel, L), BF16)),
        compiler_params=_cparams(("parallel", "parallel"), VMEM_LIMIT), name="cmp_select",
    )(qblk, ck, cvt, mt, tl)


def _flash_init(m_sc, l_sc, acc_sc):
    m_sc[...] = jnp.full(m_sc.shape, NEG, F32)
    l_sc[...] = jnp.zeros(l_sc.shape, F32)
    acc_sc[...] = jnp.zeros(acc_sc.shape, F32)


def _flash_update(s, vt, m_sc, l_sc, acc_sc):
    hl = s.shape[1] // N_KV_HEADS
    m_old = m_sc[...]
    m_new = jnp.maximum(m_old, jnp.max(s, axis=0, keepdims=True))
    alpha = jnp.exp(m_old - m_new)
    p = jnp.exp(s - m_new)
    l_sc[...] = alpha * l_sc[...] + jnp.sum(p, axis=0, keepdims=True)
    m_sc[...] = m_new
    pb = p.astype(BF16)
    for h in range(N_KV_HEADS):
        ls = slice(h * hl, (h + 1) * hl)
        acc_sc[h] = acc_sc[h] * alpha[:, ls] + _dot(vt[HEAD_DIM * h:HEAD_DIM * (h + 1), :], pb[:, ls])


def _flash_finish(o_ref, l_sc, acc_sc, tq):
    l = jnp.maximum(l_sc[...], TINY)
    for hd in range(N_HEADS):
        h, g = divmod(hd, GQA)
        o_ref[0, HEAD_DIM * hd:HEAD_DIM * (hd + 1), :] = (
            acc_sc[h][:, g * tq:(g + 1) * tq] / l[:, hd * tq:(hd + 1) * tq])


def _sel_scores(q1, sb, k, oh, kpos0, t):
    s = _dot(k.astype(BF16), q1) + _dot(oh, sb)
    kpos = kpos0 + lax.broadcasted_iota(I32, s.shape, 0)
    return jnp.where(kpos <= t, s, NEG)


def _sel_prompt_kernel(qaug_ref, k_ref, oh_ref, v_ref, t_ref, o_ref, m_sc, l_sc, acc_sc, *, tq, tk):
    qi = pl.program_id(1)
    q1 = qaug_ref[0, 0:KV_WIDTH, :]
    sb = qaug_ref[0, KV_WIDTH:, :]
    t = t_ref[0]
    _flash_init(m_sc, l_sc, acc_sc)
    n_tiles = (qi * tq + tq - 1) // tk + 1

    def body(ki, carry):
        off = pl.multiple_of(ki * tk, tk)
        s = _sel_scores(q1, sb, k_ref[0, pl.ds(off, tk), :], oh_ref[pl.ds(off, tk), :], off, t)
        vt = v_ref[0, pl.ds(off, tk), :].T.astype(BF16)
        _flash_update(s, vt, m_sc, l_sc, acc_sc)
        return carry

    lax.fori_loop(0, n_tiles, body, 0)
    _flash_finish(o_ref, l_sc, acc_sc, tq)


def _sel_prompt(qaug, k, oh, v, tl, *, tq, tk):
    nq, r, L = qaug.shape
    b, tlen, _ = k.shape
    nqb = nq // b
    return pl.pallas_call(
        functools.partial(_sel_prompt_kernel, tq=tq, tk=tk),
        grid=(b, nqb),
        in_specs=[pl.BlockSpec((1, r, L), lambda i, j: (i * nqb + j, 0, 0)),
                  pl.BlockSpec((1, tlen, KV_WIDTH), lambda i, j: (i, 0, 0)),
                  pl.BlockSpec(oh.shape, lambda i, j: (0, 0)),
                  pl.BlockSpec((1, tlen, KV_WIDTH), lambda i, j: (i, 0, 0)),
                  pl.BlockSpec((1, 1, L), lambda i, j: (j, 0, 0))],
        out_specs=pl.BlockSpec((1, N_HEADS * HEAD_DIM, tq), lambda i, j: (i * nqb + j, 0, 0)),
        out_shape=jax.ShapeDtypeStruct((nq, N_HEADS * HEAD_DIM, tq), F32),
        scratch_shapes=[pltpu.VMEM((1, L), F32), pltpu.VMEM((1, L), F32),
                        pltpu.VMEM((N_KV_HEADS, HEAD_DIM, L // N_KV_HEADS), F32)],
        compiler_params=_cparams(("parallel", "parallel"), VMEM_LIMIT), name="sel_prompt",
    )(qaug, k, oh, v, tl)


def _sel_sample_kernel(qaug_ref, k_ref, oh_ref, v_ref, kn_ref, ohn_ref, vn_ref, t_ref, o_ref,
                       m_sc, l_sc, acc_sc, *, tq, tk, past_len):
    ki = pl.program_id(1)
    q1 = qaug_ref[0, 0:KV_WIDTH, :]
    sb = qaug_ref[0, KV_WIDTH:, :]
    t = t_ref[0]

    @pl.when(ki == 0)
    def _():
        _flash_init(m_sc, l_sc, acc_sc)

    s = _sel_scores(q1, sb, k_ref[0], oh_ref[...], ki * tk, t)
    _flash_update(s, v_ref[0].T.astype(BF16), m_sc, l_sc, acc_sc)

    @pl.when(ki == pl.num_programs(1) - 1)
    def _():
        sn = _sel_scores(q1, sb, kn_ref[0], ohn_ref[...], past_len, t)
        _flash_update(sn, vn_ref[0].T.astype(BF16), m_sc, l_sc, acc_sc)
        _flash_finish(o_ref, l_sc, acc_sc, tq)


def _sel_sample(qaug, k, oh, v, kn, ohn, vn, tl, *, tq, tk, past_len):
    b, r, L = qaug.shape
    nk = past_len // tk
    nsel = oh.shape[1]
    tn = kn.shape[1]
    return pl.pallas_call(
        functools.partial(_sel_sample_kernel, tq=tq, tk=tk, past_len=past_len),
        grid=(b, nk),
        in_specs=[pl.BlockSpec((1, r, L), lambda i, j: (i, 0, 0)),
                  pl.BlockSpec((1, tk, KV_WIDTH), lambda i, j: (i, j, 0)),
                  pl.BlockSpec((tk, nsel), lambda i, j: (j, 0)),
                  pl.BlockSpec((1, tk, KV_WIDTH), lambda i, j: (i, j, 0)),
                  pl.BlockSpec((1, tn, KV_WIDTH), lambda i, j: (i, 0, 0)),
                  pl.BlockSpec((tn, nsel), lambda i, j: (0, 0)),
                  pl.BlockSpec((1, tn, KV_WIDTH), lambda i, j: (i, 0, 0)),
                  pl.BlockSpec((1, 1, L), lambda i, j: (0, 0, 0))],
        out_specs=pl.BlockSpec((1, N_HEADS * HEAD_DIM, tq), lambda i, j: (i, 0, 0)),
        out_shape=jax.ShapeDtypeStruct((b, N_HEADS * HEAD_DIM, tq), F32),
        scratch_shapes=[pltpu.VMEM((1, L), F32), pltpu.VMEM((1, L), F32),
                        pltpu.VMEM((N_KV_HEADS, HEAD_DIM, L // N_KV_HEADS), F32)],
        compiler_params=_cparams(("parallel", "arbitrary"), VMEM_LIMIT), name="sel_sample",
    )(qaug, k, oh, v, kn, ohn, vn, tl)


def _window_kernel(q_ref, k_ref, v_ref, t_ref, o_ref, *, tq, band, kpos_base):
    qi = pl.program_id(1)
    L = N_HEADS * tq
    hl = L // N_KV_HEADS
    q1 = q_ref[0]
    t = t_ref[0]
    start = pl.multiple_of(jnp.maximum(qi * tq + tq - band, 0), LANE)
    kb = k_ref[0, pl.ds(start, band), :].astype(BF16)
    s = _dot(kb, q1)
    kpos = kpos_base + start + lax.broadcasted_iota(I32, s.shape, 0)
    mask = (kpos <= t) & (kpos > t - WINDOW)
    s = jnp.where(mask, s, NEG)
    m = jnp.max(s, axis=0, keepdims=True)
    p = jnp.where(mask, jnp.exp(s - m), 0.0)
    l = jnp.maximum(jnp.sum(p, axis=0, keepdims=True), TINY)
    pb = p.astype(BF16)
    vt = v_ref[0, pl.ds(start, band), :].T.astype(BF16)
    for h in range(N_KV_HEADS):
        o_h = _dot(vt[HEAD_DIM * h:HEAD_DIM * (h + 1), :], pb[:, h * hl:(h + 1) * hl])
        for g in range(GQA):
            hd = h * GQA + g
            o_ref[0, HEAD_DIM * hd:HEAD_DIM * (hd + 1), :] = o_h[:, g * tq:(g + 1) * tq] / l[:, hd * tq:(hd + 1) * tq]


def _window(qblk, k, v, tl, *, tq, band, kpos_base):
    nq, _, L = qblk.shape
    b, tlen, _ = k.shape
    nqb = nq // b
    shared_t = tl.shape[0] == 1
    return pl.pallas_call(
        functools.partial(_window_kernel, tq=tq, band=band, kpos_base=kpos_base),
        grid=(b, nqb),
        in_specs=[pl.BlockSpec((1, KV_WIDTH, L), lambda i, j: (i * nqb + j, 0, 0)),
                  pl.BlockSpec((1, tlen, KV_WIDTH), lambda i, j: (i, 0, 0)),
                  pl.BlockSpec((1, tlen, KV_WIDTH), lambda i, j: (i, 0, 0)),
                  pl.BlockSpec((1, 1, L), (lambda i, j: (0, 0, 0)) if shared_t else (lambda i, j: (j, 0, 0)))],
        out_specs=pl.BlockSpec((1, N_HEADS * HEAD_DIM, tq), lambda i, j: (i * nqb + j, 0, 0)),
        out_shape=jax.ShapeDtypeStruct((nq, N_HEADS * HEAD_DIM, tq), F32),
        compiler_params=_cparams(("parallel", "parallel"), VMEM_LIMIT), name="window",
    )(qblk, k, v, tl)


def _combine_kernel(x_ref, a_ref, oc_ref, os_ref, ow_ref, gt_ref, e_ref, woa_ref, wob_ref, o_ref, *, tm):
    hi, lo = _split_bf16(gt_ref[...])
    ge = _dot(e_ref[...], hi) + _dot(e_ref[...], lo)
    bw = N_HEADS * HEAD_DIM
    parts = []
    for j in range(tm // LANE):
        ls = slice(j * LANE, (j + 1) * LANE)
        parts.append(ge[0:bw, ls] * oc_ref[j] + ge[bw:2 * bw, ls] * os_ref[j] + ge[2 * bw:3 * bw, ls] * ow_ref[j])
    mixt = parts[0] if len(parts) == 1 else jnp.concatenate(parts, axis=1)
    mix = mixt.T.astype(BF16)
    o_ref[...] = x_ref[...] + _dot(a_ref[...], woa_ref[...]) + _dot(mix, wob_ref[...])


def _combine(x, a, oc, os_, ow, gt, e, woa, wob, *, tm):
    n, dm = x.shape
    bw = N_HEADS * HEAD_DIM
    full = lambda arr: pl.BlockSpec(arr.shape, lambda i: (0,) * arr.ndim)
    ospec = pl.BlockSpec((tm // LANE, bw, LANE), lambda i: (i, 0, 0))
    return pl.pallas_call(
        functools.partial(_combine_kernel, tm=tm),
        grid=(n // tm,),
        in_specs=[pl.BlockSpec((tm, dm), lambda i: (i, 0)), pl.BlockSpec((tm, a.shape[1]), lambda i: (i, 0)),
                  ospec, ospec, ospec, pl.BlockSpec((32, tm), lambda i: (0, i)), full(e), full(woa), full(wob)],
        out_specs=pl.BlockSpec((tm, dm), lambda i: (i, 0)),
        out_shape=jax.ShapeDtypeStruct((n, dm), F32),
        compiler_params=_cparams(("parallel",), VMEM_LIMIT), name="combine",
    )(x, a, oc, os_, ow, gt, e, woa, wob)


def _ffn_dense_kernel(x_ref, g_ref, wg_ref, wu_ref, wd_ref, o_ref):
    xf = x_ref[...]
    h = _rms(xf, g_ref[...]).astype(BF16)
    gate = _dot(h, wg_ref[...])
    up = _dot(h, wu_ref[...])
    act = (gate * _sigmoid(gate) * up).astype(BF16)
    o_ref[...] = xf + _dot(act, wd_ref[...])


def _ffn_dense(x, g, wg, wu, wd, *, tm):
    n, dm = x.shape
    full = lambda arr: pl.BlockSpec(arr.shape, lambda i: (0,) * arr.ndim)
    return pl.pallas_call(
        _ffn_dense_kernel, grid=(n // tm,),
        in_specs=[pl.BlockSpec((tm, dm), lambda i: (i, 0)), full(g), full(wg), full(wu), full(wd)],
        out_specs=pl.BlockSpec((tm, dm), lambda i: (i, 0)),
        out_shape=jax.ShapeDtypeStruct((n, dm), F32),
        compiler_params=_cparams(("parallel",), VMEM_LIMIT), name="ffn_dense",
    )(x, g, wg, wu, wd)


def _router_kernel(x_ref, g_ref, wrh_ref, wrl_ref, tri_ref, h_ref, meta_ref, cnt_ref, *, tm, n_exp):
    h = _rms(x_ref[...], g_ref[...])
    h_ref[...] = h
    hi, lo = _split_bf16(h)
    logits = _dot(hi, wrh_ref[...]) + _dot(lo, wrh_ref[...]) + _dot(hi, wrl_ref[...])
    lane = lax.broadcasted_iota(I32, (tm, LANE), 1)
    lanef = lane.astype(F32)
    logits = jnp.where(lane < n_exp, logits, NEG)
    m1 = jnp.max(logits, axis=1, keepdims=True)
    i1 = jnp.min(jnp.where(logits == m1, lanef, 1e9), axis=1, keepdims=True)
    rest = jnp.where(lanef == i1, -3e38, logits)
    m2 = jnp.max(rest, axis=1, keepdims=True)
    i2 = jnp.min(jnp.where(rest == m2, lanef, 1e9), axis=1, keepdims=True)
    e2 = jnp.exp(m2 - m1)
    w1 = 1.0 / (1.0 + e2)
    w2 = e2 / (1.0 + e2)
    hit1 = lanef == i1
    hit2 = lanef == i2
    msel = jnp.where(hit1 | hit2, 1.0, 0.0)
    ranks = _dot(tri_ref[...], msel.astype(BF16))
    r1 = jnp.sum(jnp.where(hit1, ranks, 0.0), axis=1, keepdims=True)
    r2 = jnp.sum(jnp.where(hit2, ranks, 0.0), axis=1, keepdims=True)
    meta = jnp.where(lane == 0, i1, 0.0)
    for k, val in enumerate((i2, w1, w2, r1, r2)):
        meta = jnp.where(lane == k + 1, val, meta)
    meta_ref[...] = meta
    cnt_ref[0] = jnp.sum(msel, axis=0, keepdims=True)


def _router(x, g, wrh, wrl, tri, *, tm, n_exp):
    n, dm = x.shape
    full = lambda arr: pl.BlockSpec(arr.shape, lambda i: (0,) * arr.ndim)
    return pl.pallas_call(
        functools.partial(_router_kernel, tm=tm, n_exp=n_exp), grid=(n // tm,),
        in_specs=[pl.BlockSpec((tm, dm), lambda i: (i, 0)), full(g), full(wrh), full(wrl), full(tri)],
        out_specs=(pl.BlockSpec((tm, dm), lambda i: (i, 0)), pl.BlockSpec((tm, LANE), lambda i: (i, 0)),
                   pl.BlockSpec((1, 1, LANE), lambda i: (i, 0, 0))),
        out_shape=(jax.ShapeDtypeStruct((n, dm), F32), jax.ShapeDtypeStruct((n, LANE), F32),
                   jax.ShapeDtypeStruct((n // tm, 1, LANE), F32)),
        compiler_params=_cparams(("parallel",), VMEM_LIMIT), name="router",
    )(x, g, wrh, wrl, tri)


def _row_copy(src, src_row, dst, dst_row, sem):
    return pltpu.make_async_copy(src.at[pl.ds(src_row, 1)], dst.at[pl.ds(dst_row, 1)], sem)


def _dispatch_kernel(dest_ref, h_ref, xg_in_ref, xg_ref, sem, *, tm):
    del xg_in_ref

    def issue(r, carry):
        for k in range(TOP_K):
            _row_copy(h_ref, r, xg_ref, dest_ref[TOP_K * r + k], sem).start()
        return carry

    lax.fori_loop(0, tm, issue, 0)

    def drain(r, carry):
        for k in range(TOP_K):
            _row_copy(h_ref, 0, xg_ref, 0, sem).wait()
        return carry

    lax.fori_loop(0, tm, drain, 0)


def _dispatch(dest, h, xg0, *, tm):
    n, dm = h.shape
    return pl.pallas_call(
        functools.partial(_dispatch_kernel, tm=tm), grid=(n // tm,),
        in_specs=[pl.BlockSpec((TOP_K * tm,), lambda i: (i,), memory_space=pltpu.SMEM),
                  pl.BlockSpec((tm, dm), lambda i: (i, 0)),
                  pl.BlockSpec(memory_space=pl.ANY)],
        out_specs=pl.BlockSpec(memory_space=pl.ANY),
        out_shape=jax.ShapeDtypeStruct(xg0.shape, F32),
        scratch_shapes=[pltpu.SemaphoreType.DMA],
        input_output_aliases={2: 0},
        compiler_params=_cparams(("arbitrary",), VMEM_LIMIT), name="moe_dispatch",
    )(dest, h, xg0)


def _expert_ffn_kernel(te_ref, nused_ref, x_ref, wg_ref, wu_ref, wd_ref, o_ref):
    s = pl.program_id(0)

    @pl.when(s < nused_ref[0])
    def _():
        h = x_ref[...].astype(BF16)
        gate = _dot(h, wg_ref[0])
        up = _dot(h, wu_ref[0])
        act = (gate * _sigmoid(gate) * up).astype(BF16)
        o_ref[...] = _dot(act, wd_ref[0])

    @pl.when(s >= nused_ref[0])
    def _():
        o_ref[...] = jnp.zeros(o_ref.shape, F32)


def _expert_ffn(te, nused, xg, wg, wu, wd, *, ts):
    s_rows, dm = xg.shape
    dff = wg.shape[2]
    grid_spec = pltpu.PrefetchScalarGridSpec(
        num_scalar_prefetch=2, grid=(s_rows // ts,),
        in_specs=[pl.BlockSpec((ts, dm), lambda i, te, nu: (i, 0)),
                  pl.BlockSpec((1, dm, dff), lambda i, te, nu: (te[i], 0, 0)),
                  pl.BlockSpec((1, dm, dff), lambda i, te, nu: (te[i], 0, 0)),
                  pl.BlockSpec((1, dff, dm), lambda i, te, nu: (te[i], 0, 0))],
        out_specs=pl.BlockSpec((ts, dm), lambda i, te, nu: (i, 0)))
    return pl.pallas_call(
        _expert_ffn_kernel, grid_spec=grid_spec,
        out_shape=jax.ShapeDtypeStruct((s_rows, dm), F32),
        compiler_params=_cparams(("arbitrary",), VMEM_LIMIT), name="expert_ffn",
    )(te, nused, xg, wg, wu, wd)


def _moe_combine_kernel(dest_ref, x_ref, meta_ref, y_ref, o_ref, buf, sem, *, tm):
    def issue(r, carry):
        for k in range(TOP_K):
            _row_copy(y_ref, dest_ref[TOP_K * r + k], buf.at[k], r, sem).start()
        return carry

    lax.fori_loop(0, tm, issue, 0)

    def drain(r, carry):
        for k in range(TOP_K):
            _row_copy(y_ref, 0, buf.at[k], 0, sem).wait()
        return carry

    lax.fori_loop(0, tm, drain, 0)
    meta = meta_ref[...]
    o_ref[...] = x_ref[...] + meta[:, 2:3] * buf[0] + meta[:, 3:4] * buf[1]


def _moe_combine(dest, x, meta, y, *, tm):
    n, dm = x.shape
    return pl.pallas_call(
        functools.partial(_moe_combine_kernel, tm=tm), grid=(n // tm,),
        in_specs=[pl.BlockSpec((TOP_K * tm,), lambda i: (i,), memory_space=pltpu.SMEM),
                  pl.BlockSpec((tm, dm), lambda i: (i, 0)),
                  pl.BlockSpec((tm, LANE), lambda i: (i, 0)),
                  pl.BlockSpec(memory_space=pl.ANY)],
        out_specs=pl.BlockSpec((tm, dm), lambda i: (i, 0)),
        out_shape=jax.ShapeDtypeStruct((n, dm), F32),
        scratch_shapes=[pltpu.VMEM((TOP_K, tm, dm), F32), pltpu.SemaphoreType.DMA],
        compiler_params=_cparams(("arbitrary",), VMEM_LIMIT), name="moe_combine",
    )(dest, x, meta, y)


def _moe(x, g, wr, wg, wu, wd, *, tm, ts):
    n, dm = x.shape
    n_exp = wr.shape[1]
    wr_pad = jnp.zeros((dm, LANE), F32).at[:, :n_exp].set(wr)
    wrh, wrl = _split_bf16(wr_pad)
    tri = jnp.tril(jnp.ones((tm, tm), F32), -1).astype(BF16)
    h, meta, cnt = _router(x, g, wrh, wrl, tri, tm=tm, n_exp=n_exp)
    cnt = cnt[:, 0, :n_exp].astype(I32)
    tile_off = jnp.cumsum(cnt, axis=0) - cnt
    tot = jnp.sum(cnt, axis=0)
    padded = ((tot + ts - 1) // ts) * ts
    gend = jnp.cumsum(padded)
    gstart = gend - padded
    ei = meta[:, 0:2].astype(I32)
    rk = meta[:, 4:6].astype(I32)
    tile_id = (jnp.arange(n, dtype=I32) // tm)[:, None]
    dest = (gstart[ei] + tile_off[tile_id, ei] + rk).reshape(-1)
    n_slots = TOP_K * n + n_exp * ts
    n_slot_tiles = n_slots // ts
    te = jnp.minimum(jnp.searchsorted(gend, jnp.arange(n_slot_tiles, dtype=I32) * ts, side="right"), n_exp - 1).astype(I32)
    nused = (gend[-1] // ts).astype(I32).reshape(1)
    te = jnp.where(jnp.arange(n_slot_tiles) < nused[0], te, te[jnp.maximum(nused[0] - 1, 0)])
    xg = _dispatch(dest, h, jnp.zeros((n_slots, dm), F32), tm=tm)
    y = _expert_ffn(te, nused, xg, wg, wu, wd, ts=ts)
    return _moe_combine(dest, x, meta, y, tm=tm)


def _norm_kernel(x_ref, g_ref, o_ref):
    o_ref[...] = _rms(x_ref[...], g_ref[...])


def _final_norm(x, g, *, tm):
    n, dm = x.shape
    return pl.pallas_call(
        _norm_kernel, grid=(n // tm,),
        in_specs=[pl.BlockSpec((tm, dm), lambda i: (i, 0)), pl.BlockSpec(g.shape, lambda i: (0, 0))],
        out_specs=pl.BlockSpec((tm, dm), lambda i: (i, 0)),
        out_shape=jax.ShapeDtypeStruct((n, dm), F32),
        compiler_params=_cparams(("parallel",)), name="final_norm",
    )(x, g)


def _page_gather_kernel(pt_ref, *refs, layer, n_pages, page, n_arr):
    pools, outs, sem = refs[:n_arr], refs[n_arr:2 * n_arr], refs[2 * n_arr]
    b = pl.program_id(0)

    def copies(j, pg):
        return [pltpu.make_async_copy(p.at[layer, pg], o.at[b, pl.ds(j * page, page)], sem) for p, o in zip(pools, outs)]

    def issue(j, carry):
        for c in copies(j, pt_ref[b, j]):
            c.start()
        return carry

    lax.fori_loop(0, n_pages, issue, 0)

    def drain(j, carry):
        for c in copies(0, 0):
            c.wait()
        return carry

    lax.fori_loop(0, n_pages, drain, 0)


def _page_gather(page_table, pools, layer):
    b, n_pages = page_table.shape
    page = pools[0].shape[2]
    n_arr = len(pools)
    grid_spec = pltpu.PrefetchScalarGridSpec(
        num_scalar_prefetch=1, grid=(b,),
        in_specs=[pl.BlockSpec(memory_space=pl.ANY)] * n_arr,
        out_specs=[pl.BlockSpec(memory_space=pl.ANY)] * n_arr,
        scratch_shapes=[pltpu.SemaphoreType.DMA])
    return pl.pallas_call(
        functools.partial(_page_gather_kernel, layer=layer, n_pages=n_pages, page=page, n_arr=n_arr),
        grid_spec=grid_spec,
        out_shape=[jax.ShapeDtypeStruct((b, n_pages * page, KV_WIDTH), F32)] * n_arr,
        compiler_params=_cparams(("arbitrary",)), name="page_gather",
    )(page_table, *pools)


def _rope_tables(pos):
    half = HEAD_DIM // 2
    inv = ROPE_THETA ** (-jnp.arange(half, dtype=F32) / half)
    ang = pos.astype(F32)[:, None] * inv[None, :]
    cos, sin = jnp.cos(ang), jnp.sin(ang)
    cos_std = jnp.tile(cos, (1, 4))
    sin_std = jnp.tile(jnp.concatenate([-sin, sin], axis=1), (1, 2))
    return cos_std, sin_std, cos.T, sin.T


def _cmp_to_sel_t(n_cmp_pad, n_sel_pad):
    i0 = np.arange(n_cmp_pad, dtype=np.int64)[None, :] * CMP_STRIDE
    j0 = np.arange(n_sel_pad, dtype=np.int64)[:, None] * SEL_LEN
    ov = np.clip(np.minimum(i0 + CMP_LEN, j0 + SEL_LEN) - np.maximum(i0, j0), 0, CMP_LEN)
    return jnp.asarray(ov.astype(np.float32) / CMP_LEN, dtype=BF16)


def _block_onehot(n_keys, n_sel_pad):
    blk = np.arange(n_keys)[:, None] // SEL_LEN
    return jnp.asarray((blk == np.arange(n_sel_pad)[None, :]).astype(np.float32), dtype=BF16)


def _lane_positions(pos_tiles, tq):
    return jnp.tile(pos_tiles.astype(I32), (1, N_HEADS))[:, None, :]


def _compress_weights(w1, w2, pe):
    nr = CMP_LEN // CMP_STRIDE
    w1r = w1.reshape(nr, CMP_STRIDE, HEAD_DIM, CMP_HIDDEN)
    eye = jnp.eye(N_KV_HEADS, dtype=F32)
    wbig = jnp.einsum("rsde,gh->sgdrhe", w1r, eye)
    wbig = wbig.reshape(CMP_STRIDE * KV_WIDTH, nr * N_KV_HEADS * CMP_HIDDEN).astype(BF16)
    w2big = jnp.einsum("ed,gh->gehd", w2, eye).reshape(N_KV_HEADS * CMP_HIDDEN, KV_WIDTH).astype(BF16)
    pe8 = jnp.tile(pe.reshape(1, CMP_LEN * HEAD_DIM), (8, 1)).astype(BF16)
    return wbig, pe8, w1.astype(BF16), w2big


def _layer_weights(l, g_mix, w_in, g_sgu, w_sgu, b_sgu, w_cmpk1, w_cmpk2, pe_cmpk, w_cmpv1, w_cmpv2, pe_cmpv, w_o, dec_t):
    aw = A_GROUPS * CHUNK
    bw = N_HEADS * HEAD_DIM
    w = w_in[l]
    wstd = jnp.concatenate([w[:, :2 * aw], w[:, 2 * aw + bw:2 * aw + bw + 6 * KV_WIDTH]], axis=1).astype(BF16)
    n_gate = 3 * N_HEADS
    wt = jnp.concatenate([w[:, 2 * aw:2 * aw + bw], w[:, -n_gate:], jnp.zeros((w.shape[0], 32 - n_gate), F32)], axis=1).T.astype(BF16)
    causal = jnp.tril(jnp.ones((CHUNK, CHUNK), bool))
    wsgu_p = jnp.where(causal[None], w_sgu[l], 0.0).astype(BF16)
    bsgu_p = jnp.broadcast_to(b_sgu[l][:, :, None], (A_GROUPS, CHUNK, CHUNK)).astype(F32)
    reps = CHUNK // dec_t
    wsmall = jnp.where(causal[None, :dec_t, :dec_t], w_sgu[l][:, :dec_t, :dec_t], 0.0)
    wsgu_s = jnp.einsum("ab,gts->gatbs", jnp.eye(reps, dtype=F32), wsmall).reshape(A_GROUPS, CHUNK, CHUNK).astype(BF16)
    bsgu_s = jnp.broadcast_to(jnp.tile(b_sgu[l][:, :dec_t], (1, reps))[:, :, None], (A_GROUPS, CHUNK, CHUNK)).astype(F32)
    ck_w = _compress_weights(w_cmpk1[l], w_cmpk2[l], pe_cmpk[l])
    cv_w = _compress_weights(w_cmpv1[l], w_cmpv2[l], pe_cmpv[l])
    cv_w = cv_w[:3] + (cv_w[3].T,)
    e = np.zeros((3 * bw, 32), np.float32)
    for br in range(3):
        for hd in range(N_HEADS):
            e[br * bw + hd * HEAD_DIM:br * bw + (hd + 1) * HEAD_DIM, br * N_HEADS + hd] = 1.0
    return dict(gmix=g_mix[l][None, :], wstd=wstd, wt=wt, gsgu=g_sgu[l][None, :],
                wsgu_p=wsgu_p, bsgu_p=bsgu_p, wsgu_s=wsgu_s, bsgu_s=bsgu_s, ck_w=ck_w, cv_w=cv_w,
                e=jnp.asarray(e, dtype=BF16), woa=w_o[l][:aw].astype(BF16), wob=w_o[l][aw:].astype(BF16))


def _channel_mixer(l, x, g_ffn, dense_w, w_router, moe_w, *, tm_dense, tm_moe):
    g = g_ffn[l][None, :]
    i = l // 2
    if l % 2 == 0:
        return _ffn_dense(x, g, *(w[i] for w in dense_w), tm=tm_dense)
    return _moe(x, g, w_router[i], *(w[i] for w in moe_w), tm=tm_moe, ts=256)


def kernel(x_prompt, x_sample, cache_cmp_k, cache_cmp_v, cache_sel_k, cache_sel_v, cache_win_k, cache_win_v, page_table,
           g_mix, w_in, g_sgu, w_sgu, b_sgu, w_cmpk1, w_cmpk2, pe_cmpk, w_cmpv1, w_cmpv2, pe_cmpv, w_o, g_ffn,
           w_ff_gate, w_ff_up, w_ff_down, w_router, w_moe_gate, w_moe_up, w_moe_down, g_final):
    bsz, seq, dm = x_prompt.shape
    dec_b, dec_t, _ = x_sample.shape
    depth = g_mix.shape[0]
    page = cache_cmp_k.shape[2]
    past_len = page_table.shape[1] * page
    wbuf = cache_win_k.shape[2]
    n_p, n_s = bsz * seq, dec_b * dec_t
    assert n_s == CHUNK and seq % 512 == 0 and wbuf == WINDOW and past_len % 2048 == 0
    tq_p, tq_s = LANE, LANE // N_HEADS
    nq_b = seq // tq_p
    band = WINDOW + tq_p

    pos_p = jnp.tile(jnp.arange(seq, dtype=I32), bsz)
    pos_s = jnp.tile(past_len + jnp.arange(dec_t, dtype=I32), dec_b)
    rope_p = _rope_tables(pos_p)
    rope_s = _rope_tables(pos_s)
    tl_p = _lane_positions(jnp.arange(seq, dtype=I32).reshape(nq_b, tq_p), tq_p)
    tok_s = past_len + jnp.minimum(jnp.arange(tq_s, dtype=I32), dec_t - 1)
    tl_s = _lane_positions(tok_s[None, :], tq_s)
    nch_p = seq // CMP_STRIDE
    nsel_p = seq // SEL_LEN
    nch_s = past_len // CMP_STRIDE
    nsel_s = -(-(past_len // SEL_LEN + 1) // LANE) * LANE
    mt_p = _cmp_to_sel_t(nch_p, nsel_p)
    mt_s = _cmp_to_sel_t(nch_s, nsel_s)
    oh_p = _block_onehot(seq, nsel_p)
    oh_s = _block_onehot(past_len + SEL_LEN, nsel_s)
    head_of_row = jnp.asarray((np.arange(N_HEADS)[None, :] // GQA == np.arange(N_KV_HEADS)[:, None]).astype(np.float32))

    pools = [c.reshape(depth, c.shape[1], page, KV_WIDTH) for c in (cache_cmp_k, cache_cmp_v, cache_sel_k, cache_sel_v)]
    flat = lambda a: a.reshape(a.shape[0], a.shape[1], KV_WIDTH)

    dense_w = tuple(w.astype(BF16) for w in (w_ff_gate, w_ff_up, w_ff_down))
    moe_w = tuple(w.astype(BF16) for w in (w_moe_gate, w_moe_up, w_moe_down))
    xp = x_prompt.reshape(n_p, dm)
    xs = x_sample.reshape(n_s, dm)
    outs = [[] for _ in range(13)]
    for l in range(depth):
        lw = _layer_weights(l, g_mix, w_in, g_sgu, w_sgu, b_sgu, w_cmpk1, w_cmpk2, pe_cmpk, w_cmpv1, w_cmpv2, pe_cmpv, w_o, dec_t)
        a, _, kc, vc, ks, vs, kw, vw, qblk, gt = _inproj(
            xp, lw["gmix"], lw["wstd"], lw["wt"], lw["gsgu"], lw["wsgu_p"], lw["bsgu_p"], *rope_p, tm=512, emit_qblk=True)
        ck = _compress(kc.reshape(bsz, nch_p, CMP_STRIDE * KV_WIDTH), *lw["ck_w"], transpose_out=False)
        cvt = _compress(vc.reshape(bsz, nch_p, CMP_STRIDE * KV_WIDTH), *lw["cv_w"], transpose_out=True)
        oc, qaug = _cmp_select(qblk, ck, cvt, mt_p, tl_p, tq=tq_p, nq_per_b=nq_b)
        osel = _sel_prompt(qaug, ks.reshape(bsz, seq, KV_WIDTH), oh_p, vs.reshape(bsz, seq, KV_WIDTH), tl_p, tq=tq_p, tk=512)
        ow = _window(qblk, kw.reshape(bsz, seq, KV_WIDTH), vw.reshape(bsz, seq, KV_WIDTH), tl_p, tq=tq_p, band=band, kpos_base=0)
        xp = _combine(xp, a, oc, osel, ow, gt, lw["e"], lw["woa"], lw["wob"], tm=512)
        kv4 = lambda t: t.reshape(bsz, seq, N_KV_HEADS, HEAD_DIM)
        for idx, t in enumerate((kc, vc, ks, vs)):
            outs[idx].append(kv4(t))
        nwin_p = min(WINDOW, seq)
        outs[4].append(kv4(kw)[:, seq - nwin_p:])
        outs[5].append(kv4(vw)[:, seq - nwin_p:])

        a, v_s, kc, vc, ks, vs, kw, vw, qt, gt = _inproj(
            xs, lw["gmix"], lw["wstd"], lw["wt"], lw["gsgu"], lw["wsgu_s"], lw["bsgu_s"], *rope_s, tm=CHUNK, emit_qblk=False)
        q4 = qt.reshape(N_HEADS, HEAD_DIM, dec_b, dec_t)
        q4 = jnp.pad(q4, ((0, 0), (0, 0), (0, 0), (0, tq_s - dec_t)))
        qb = jnp.einsum("hdbt,gh->bgdht", q4.astype(F32), head_of_row).astype(BF16)
        qblk_s = qb.reshape(dec_b, KV_WIDTH, N_HEADS * tq_s)
        past_ck, past_cv, past_sk, past_sv = _page_gather(page_table, pools, l)
        ck = _compress(past_ck.reshape(dec_b, nch_s, CMP_STRIDE * KV_WIDTH), *lw["ck_w"], transpose_out=False)
        cvt = _compress(past_cv.reshape(dec_b, nch_s, CMP_STRIDE * KV_WIDTH), *lw["cv_w"], transpose_out=True)
        oc, qaug = _cmp_select(qblk_s, ck, cvt, mt_s, tl_s, tq=tq_s, nq_per_b=1)
        pad_new = lambda t: jnp.pad(t.reshape(dec_b, dec_t, KV_WIDTH), ((0, 0), (0, SEL_LEN - dec_t), (0, 0)))
        osel = _sel_sample(qaug, past_sk, oh_s, past_sv, pad_new(ks), oh_s[past_len:], pad_new(vs), tl_s,
                           tq=tq_s, tk=2048, past_len=past_len)
        kw_all = jnp.concatenate([flat(cache_win_k[l]), kw.reshape(dec_b, dec_t, KV_WIDTH)], axis=1)
        vw_all = jnp.concatenate([flat(cache_win_v[l]), vw.reshape(dec_b, dec_t, KV_WIDTH)], axis=1)
        wpad = ((0, 0), (0, band - wbuf - dec_t), (0, 0))
        ow = _window(qblk_s, jnp.pad(kw_all, wpad), jnp.pad(vw_all, wpad), tl_s, tq=tq_s, band=band, kpos_base=past_len - wbuf)
        untile = lambda o: o[:, :, :dec_t].transpose(1, 0, 2).reshape(1, N_HEADS * HEAD_DIM, n_s)
        xs = _combine(xs, a, untile(oc), untile(osel), untile(ow), gt, lw["e"], lw["woa"], lw["wob"], tm=CHUNK)
        kv4s = lambda t: t.reshape(dec_b, dec_t, N_KV_HEADS, HEAD_DIM)
        for idx, t in enumerate((kc, vc, ks, vs)):
            outs[6 + idx].append(kv4s(t))
        nwin_s = min(WINDOW, wbuf + dec_t)
        outs[10].append(kw_all[:, wbuf + dec_t - nwin_s:].reshape(dec_b, nwin_s, N_KV_HEADS, HEAD_DIM))
        outs[11].append(vw_all[:, wbuf + dec_t - nwin_s:].reshape(dec_b, nwin_s, N_KV_HEADS, HEAD_DIM))
        outs[12].append(v_s.reshape(dec_b, dec_t, A_GROUPS * CHUNK))

        xp = _channel_mixer(l, xp, g_ffn, dense_w, w_router, moe_w, tm_dense=256, tm_moe=512)
        xs = _channel_mixer(l, xs, g_ffn, dense_w, w_router, moe_w, tm_dense=CHUNK, tm_moe=CHUNK)

    gf = g_final[None, :]
    y_prompt = _final_norm(xp, gf, tm=512).reshape(bsz, seq, dm)
    y_sample = _final_norm(xs, gf, tm=CHUNK).reshape(dec_b, dec_t, dm)
    return (y_prompt, y_sample) + tuple(jnp.stack(o, axis=0) for o in outs)
```

```python
import functools
import math

import numpy as np
import jax
import jax.numpy as jnp
from jax import lax
from jax.experimental import pallas as pl
from jax.experimental.pallas import tpu as pltpu

F32 = jnp.float32
BF16 = jnp.bfloat16
I32 = jnp.int32

A_GROUPS = 4
N_HEADS = 8
N_KV_HEADS = 2
HEAD_DIM = 64
CMP_LEN = 32
CMP_STRIDE = 16
CMP_HIDDEN = 128
SEL_LEN = 64
N_SEL = 16
WINDOW = 512
CHUNK = 128
ROPE_THETA = 10000.0
TOP_K = 2
EPS = 1e-6
NEG = -1e30
BIG = 1e30
TINY = 1e-30
SCALE = HEAD_DIM ** -0.5
QSCALE = SCALE * math.log2(math.e)

LANE = 128
KV_WIDTH = N_KV_HEADS * HEAD_DIM
GQA = N_HEADS // N_KV_HEADS
VMEM_LIMIT = 56 * 1024 * 1024


def _cparams(sem, vmem=None):
    return pltpu.CompilerParams(dimension_semantics=sem, vmem_limit_bytes=vmem)


def _rms(xf, g):
    return xf * lax.rsqrt(jnp.mean(xf * xf, axis=-1, keepdims=True) + EPS) * g


def _gelu(x):
    c = math.sqrt(2.0 / math.pi)
    return x * (0.5 * (1.0 + jnp.tanh(c * (x + 0.044715 * (x * x * x)))))


def _sigmoid(x):
    return 1.0 / (1.0 + jnp.exp(-x))


def _dot(a, b):
    return jnp.dot(a, b, preferred_element_type=F32)


def _dot_nt(a, b):
    return lax.dot_general(a, b, (((1,), (1,)), ((), ())), preferred_element_type=F32)


def _split_bf16(x):
    hi = x.astype(BF16)
    lo = (x - hi.astype(F32)).astype(BF16)
    return hi, lo


def _inproj_kernel(x_ref, gmix_ref, wstd_ref, wt_ref, gsgu_ref, wsgu_ref, bsgu_ref,
                   cos_ref, sin_ref, cost_ref, sint_ref,
                   a_ref, v_ref, kc_ref, vc_ref, ks_ref, vs_ref, kw_ref, vw_ref, q_ref, gt_ref,
                   *, tm, emit_qblk):
    xf = x_ref[...]
    h = _rms(xf, gmix_ref[...]).astype(BF16)
    z = _dot(h, wstd_ref[...])
    zt = _dot_nt(wt_ref[...], h)
    aw = A_GROUPS * CHUNK
    u = _gelu(z[:, 0:aw])
    vv = _gelu(z[:, aw:2 * aw])
    mu = jnp.mean(vv, axis=-1, keepdims=True)
    d = vv - mu
    var = jnp.mean(d * d, axis=-1, keepdims=True)
    v = d * lax.rsqrt(var + EPS) * gsgu_ref[...]
    v_ref[...] = v
    vb = v.astype(BF16)
    nc = tm // CHUNK
    for g in range(A_GROUPS):
        gs = slice(g * CHUNK, (g + 1) * CHUNK)
        parts = [vb[c * CHUNK:(c + 1) * CHUNK, gs] for c in range(nc)]
        xg = parts[0] if nc == 1 else jnp.concatenate(parts, axis=1)
        yg = _dot(wsgu_ref[g], xg)
        for c in range(nc):
            cs = slice(c * CHUNK, (c + 1) * CHUNK)
            mixed = yg[:, cs] + bsgu_ref[g]
            a_ref[cs, gs] = (u[cs, gs] * mixed).astype(BF16)

    cosr = cos_ref[...]
    sinr = sin_ref[...]
    lane = lax.broadcasted_iota(I32, (tm, KV_WIDTH), 1)
    first = (lane % HEAD_DIM) < (HEAD_DIM // 2)

    def rope(x):
        rot = jnp.where(first, pltpu.roll(x, KV_WIDTH - HEAD_DIM // 2, 1), pltpu.roll(x, HEAD_DIM // 2, 1))
        return x * cosr + rot * sinr

    o = 2 * aw
    kc_ref[...] = rope(z[:, o:o + 128])
    vc_ref[...] = z[:, o + 128:o + 256]
    ks_ref[...] = rope(z[:, o + 256:o + 384])
    vs_ref[...] = z[:, o + 384:o + 512]
    kw_ref[...] = rope(z[:, o + 512:o + 640])
    vw_ref[...] = z[:, o + 640:o + 768]

    ct = cost_ref[...]
    st = sint_ref[...]
    half = HEAD_DIM // 2
    for hd in range(N_HEADS):
        x1 = zt[HEAD_DIM * hd:HEAD_DIM * hd + half]
        x2 = zt[HEAD_DIM * hd + half:HEAD_DIM * (hd + 1)]
        qh = jnp.concatenate([(x1 * ct - x2 * st) * QSCALE, (x2 * ct + x1 * st) * QSCALE], axis=0).astype(BF16)
        if emit_qblk:
            kvh = hd // GQA
            zero = jnp.zeros((HEAD_DIM, LANE), BF16)
            for j in range(tm // LANE):
                ls = slice(hd * LANE, (hd + 1) * LANE)
                q_ref[j, HEAD_DIM * kvh:HEAD_DIM * (kvh + 1), ls] = qh[:, j * LANE:(j + 1) * LANE]
                q_ref[j, HEAD_DIM * (1 - kvh):HEAD_DIM * (2 - kvh), ls] = zero
        else:
            q_ref[HEAD_DIM * hd:HEAD_DIM * (hd + 1), :] = qh
    nq = N_HEADS * HEAD_DIM
    gt_ref[...] = _sigmoid(zt[nq:nq + 32])


def _inproj(x, gmix, wstd, wt, gsgu, wsgu, bsgu, cos, sin, cost, sint, *, tm, emit_qblk):
    n, dm = x.shape
    nt = n // tm
    row = lambda w: pl.BlockSpec((tm, w), lambda i: (i, 0))
    full = lambda a: pl.BlockSpec(a.shape, lambda i: (0,) * a.ndim)
    if emit_qblk:
        q_shape = jax.ShapeDtypeStruct((n // LANE, KV_WIDTH, N_HEADS * LANE), BF16)
        q_spec = pl.BlockSpec((tm // LANE, KV_WIDTH, N_HEADS * LANE), lambda i: (i, 0, 0))
    else:
        q_shape = jax.ShapeDtypeStruct((N_HEADS * HEAD_DIM, n), BF16)
        q_spec = pl.BlockSpec((N_HEADS * HEAD_DIM, tm), lambda i: (0, i))
    kv = jax.ShapeDtypeStruct((n, KV_WIDTH), F32)
    out_shape = (jax.ShapeDtypeStruct((n, 512), BF16), jax.ShapeDtypeStruct((n, 512), F32),
                 kv, kv, kv, kv, kv, kv, q_shape, jax.ShapeDtypeStruct((32, n), F32))
    out_specs = (row(512), row(512), row(128), row(128), row(128), row(128), row(128), row(128), q_spec,
                 pl.BlockSpec((32, tm), lambda i: (0, i)))
    in_specs = [row(dm), full(gmix), full(wstd), full(wt), full(gsgu), full(wsgu), full(bsgu),
                row(128), row(128), pl.BlockSpec((32, tm), lambda i: (0, i)), pl.BlockSpec((32, tm), lambda i: (0, i))]
    return pl.pallas_call(
        functools.partial(_inproj_kernel, tm=tm, emit_qblk=emit_qblk),
        grid=(nt,), in_specs=in_specs, out_specs=out_specs, out_shape=out_shape,
        compiler_params=_cparams(("parallel",), VMEM_LIMIT), name="inproj",
    )(x, gmix, wstd, wt, gsgu, wsgu, bsgu, cos, sin, cost, sint)


def _compress_kernel(c_ref, wbig_ref, pe_ref, w1_ref, w2_ref, o_ref, *, transpose_out):
    c = c_ref[0].astype(BF16)
    ab = _dot(c, wbig_ref[...])
    nch = ab.shape[0]
    hw = N_KV_HEADS * CMP_HIDDEN
    peb = _dot(pe_ref[...], w1_ref[...])
    bias = jnp.concatenate([peb[0:1]] * N_KV_HEADS, axis=1)
    hh = ab[:, :hw] + pltpu.roll(ab[:, hw:], nch - 1, 0) + bias
    g = _gelu(hh).astype(BF16)
    if transpose_out:
        o_ref[0] = _dot_nt(w2_ref[...], g).astype(BF16)
    else:
        o_ref[0] = _dot(g, w2_ref[...]).astype(BF16)


def _compress(c, wbig, pe8, w1, w2, *, transpose_out):
    b, nch, cw = c.shape
    if transpose_out:
        out_shape = jax.ShapeDtypeStruct((b, KV_WIDTH, nch), BF16)
        out_spec = pl.BlockSpec((1, KV_WIDTH, nch), lambda i: (i, 0, 0))
    else:
        out_shape = jax.ShapeDtypeStruct((b, nch, KV_WIDTH), BF16)
        out_spec = pl.BlockSpec((1, nch, KV_WIDTH), lambda i: (i, 0, 0))
    full = lambda a: pl.BlockSpec(a.shape, lambda i: (0,) * a.ndim)
    return pl.pallas_call(
        functools.partial(_compress_kernel, transpose_out=transpose_out),
        grid=(b,), in_specs=[pl.BlockSpec((1, nch, cw), lambda i: (i, 0, 0)), full(wbig), full(pe8), full(w1), full(w2)],
        out_specs=out_spec, out_shape=out_shape,
        compiler_params=_cparams(("parallel",), VMEM_LIMIT), name="compress",
    )(c, wbig, pe8, w1, w2)


def _cmp_select_kernel(q_ref, ck_ref, cvt_ref, mt_ref, t_ref, oc_ref, qaug_ref, *, tq):
    L = N_HEADS * tq
    hl = L // N_KV_HEADS
    q1 = q_ref[0]
    t = t_ref[0]
    s = _dot(ck_ref[0], q1)
    nch = s.shape[0]
    n_iota = lax.broadcasted_iota(I32, (nch, L), 0)
    mask = (n_iota * CMP_STRIDE + (CMP_LEN - 1)) <= t
    s = jnp.where(mask, s, NEG)
    m = jnp.max(s, axis=0, keepdims=True)
    p = jnp.where(mask, jnp.exp2(s - m), 0.0)
    p = p * (1.0 / jnp.maximum(jnp.sum(p, axis=0, keepdims=True), TINY))
    pb = p.astype(BF16)
    mt = mt_ref[...]
    nsel = mt.shape[0]
    blk = lax.broadcasted_iota(I32, (nsel, tq), 0).astype(F32)
    cur = (t[:, 0:tq] // SEL_LEN).astype(F32)
    biases = []
    for h in range(N_KV_HEADS):
        o_h = _dot(cvt_ref[0, HEAD_DIM * h:HEAD_DIM * (h + 1), :], pb[:, h * hl:(h + 1) * hl])
        psum = p[:, h * hl:h * hl + tq]
        for g in range(GQA):
            hd = h * GQA + g
            oc_ref[0, HEAD_DIM * hd:HEAD_DIM * (hd + 1), :] = o_h[:, g * tq:(g + 1) * tq]
            if g > 0:
                psum = psum + p[:, hd * tq:(hd + 1) * tq]
        hi, lo = _split_bf16(psum)
        imp = _dot(mt, hi) + _dot(mt, lo)
        forced = (blk == 0.0) | (blk == cur) | (blk == cur - 1.0)
        sc = jnp.where(blk <= cur, jnp.where(forced, BIG, imp), NEG)
        sel = jnp.zeros((nsel, tq), jnp.bool_)
        for _ in range(N_SEL):
            mx = jnp.max(sc, axis=0, keepdims=True)
            idx = jnp.min(jnp.where(sc == mx, blk, 1e9), axis=0, keepdims=True)
            hit = blk == idx
            sel = sel | (hit & (mx > NEG / 2))
            sc = jnp.where(hit, -3e38, sc)
        bias_h = jnp.where(sel, 0.0, NEG).astype(BF16)
        biases.extend([bias_h] * GQA)
    qaug_ref[0, 0:KV_WIDTH, :] = q1
    qaug_ref[0, KV_WIDTH:, :] = jnp.concatenate(biases, axis=1)


def _cmp_select(qblk, ck, cvt, mt, tl, *, tq, nq_per_b):
    nq, _, L = qblk.shape
    b, nch, _ = ck.shape
    nsel = mt.shape[0]
    shared_t = tl.shape[0] == 1
    return pl.pallas_call(
        functools.partial(_cmp_select_kernel, tq=tq),
        grid=(b, nq_per_b),
        in_specs=[pl.BlockSpec((1, KV_WIDTH, L), lambda i, j: (i * nq_per_b + j, 0, 0)),
                  pl.BlockSpec((1, nch, KV_WIDTH), lambda i, j: (i, 0, 0)),
                  pl.BlockSpec((1, KV_WIDTH, nch), lambda i, j: (i, 0, 0)),
                  pl.BlockSpec(mt.shape, lambda i, j: (0, 0)),
                  pl.BlockSpec((1, 1, L), (lambda i, j: (0, 0, 0)) if shared_t else (lambda i, j: (j, 0, 0)))],
        out_specs=(pl.BlockSpec((1, N_HEADS * HEAD_DIM, tq), lambda i, j: (i * nq_per_b + j, 0, 0)),
                   pl.BlockSpec((1, KV_WIDTH + nsel, L), lambda i, j: (i * nq_per_b + j, 0, 0))),
        out_shape=(jax.ShapeDtypeStruct((nq, N_HEADS * HEAD_DIM, tq), F32),
                   jax.ShapeDtypeStruct((nq, KV_WIDTH + nsel, L), BF16)),
        compiler_params=_cparams(("parallel", "parallel"), VMEM_LIMIT), name="cmp_select",
    )(qblk, ck, cvt, mt, tl)


def _flash_init(m_sc, l_sc, acc_sc):
    m_sc[...] = jnp.full(m_sc.shape, NEG, F32)
    l_sc[...] = jnp.zeros(l_sc.shape, F32)
    acc_sc[...] = jnp.zeros(acc_sc.shape, F32)


def _flash_update(s, vt, m_sc, l_sc, acc_sc):
    hl = s.shape[1] // N_KV_HEADS
    m_old = m_sc[...]
    m_new = jnp.maximum(m_old, jnp.max(s, axis=0, keepdims=True))
    alpha = jnp.exp2(m_old - m_new)
    p = jnp.exp2(s - m_new)
    l_sc[...] = alpha * l_sc[...] + jnp.sum(p, axis=0, keepdims=True)
    m_sc[...] = m_new
    pb = p.astype(BF16)
    for h in range(N_KV_HEADS):
        ls = slice(h * hl, (h + 1) * hl)
        acc_sc[h] = acc_sc[h] * alpha[:, ls] + _dot(vt[HEAD_DIM * h:HEAD_DIM * (h + 1), :], pb[:, ls])


def _flash_finish(o_ref, l_sc, acc_sc, tq):
    linv = 1.0 / jnp.maximum(l_sc[...], TINY)
    for hd in range(N_HEADS):
        h, g = divmod(hd, GQA)
        o_ref[0, HEAD_DIM * hd:HEAD_DIM * (hd + 1), :] = (
            acc_sc[h][:, g * tq:(g + 1) * tq] * linv[:, hd * tq:(hd + 1) * tq])


def _sel_scores(q1, sb, k, oh, kpos0, t, causal=True):
    s = _dot(k.astype(BF16), q1) + _dot(oh, sb)
    if not causal:
        return s
    kpos = kpos0 + lax.broadcasted_iota(I32, s.shape, 0)
    return jnp.where(kpos <= t, s, NEG)


VSUM_ROWS = 16
VT_ROWS = HEAD_DIM + VSUM_ROWS


def _sel_prompt_kernel(qaug_ref, kaug_ref, vt_ref, t_ref, o_ref, m_sc, acc_sc, *, tq, tk):
    qi = pl.program_id(1)
    L = N_HEADS * tq
    hl = L // N_KV_HEADS
    qa = qaug_ref[0]
    t = t_ref[0]
    m_sc[...] = jnp.full(m_sc.shape, NEG, F32)
    acc_sc[...] = jnp.zeros(acc_sc.shape, F32)
    last = (qi * tq + tq - 1) // tk

    def tile(ki, causal):
        off = pl.multiple_of(ki * tk, tk)
        s = _dot(kaug_ref[0, pl.ds(off, tk), :], qa)
        if causal:
            kpos = off + lax.broadcasted_iota(I32, s.shape, 0)
            s = jnp.where(kpos <= t, s, NEG)
        m_old = m_sc[...]
        m_new = jnp.maximum(m_old, jnp.max(s, axis=0, keepdims=True))
        m_sc[...] = m_new
        alpha = jnp.exp2(m_old - m_new)
        pb = jnp.exp2(s - m_new).astype(BF16)
        for h in range(N_KV_HEADS):
            ls = slice(h * hl, (h + 1) * hl)
            acc_sc[h] = acc_sc[h] * alpha[:, ls] + _dot(vt_ref[ki, h * VT_ROWS:(h + 1) * VT_ROWS, :], pb[:, ls])

    def body(ki, carry):
        tile(ki, False)
        return carry

    lax.fori_loop(0, last, body, 0)
    tile(last, True)
    for hd in range(N_HEADS):
        h, g = divmod(hd, GQA)
        gs = slice(g * tq, (g + 1) * tq)
        linv = 1.0 / jnp.maximum(acc_sc[h, HEAD_DIM:HEAD_DIM + 1, gs], TINY)
        o_ref[0, HEAD_DIM * hd:HEAD_DIM * (hd + 1), :] = acc_sc[h, 0:HEAD_DIM, gs] * linv


def _sel_prompt(qaug, kaug, vt, tl, *, tq, tk, bsz):
    nq, r, L = qaug.shape
    nqb = nq // bsz
    tlen = kaug.shape[0] // bsz
    return pl.pallas_call(
        functools.partial(_sel_prompt_kernel, tq=tq, tk=tk),
        grid=(bsz, nqb),
        in_specs=[pl.BlockSpec((1, r, L), lambda i, j: (i * nqb + j, 0, 0)),
                  pl.BlockSpec((1, tlen, r), lambda i, j: (i, 0, 0)),
                  pl.BlockSpec((tlen // tk, N_KV_HEADS * VT_ROWS, tk), lambda i, j: (i, 0, 0)),
                  pl.BlockSpec((1, 1, L), lambda i, j: (j, 0, 0))],
        out_specs=pl.BlockSpec((1, N_HEADS * HEAD_DIM, tq), lambda i, j: (i * nqb + j, 0, 0)),
        out_shape=jax.ShapeDtypeStruct((nq, N_HEADS * HEAD_DIM, tq), F32),
        scratch_shapes=[pltpu.VMEM((1, L), F32), pltpu.VMEM((N_KV_HEADS, VT_ROWS, L // N_KV_HEADS), F32)],
        compiler_params=_cparams(("parallel", "parallel"), VMEM_LIMIT), name="sel_prompt",
    )(qaug, kaug.reshape(bsz, tlen, r), vt, tl)


def _sel_prep_kernel(k_ref, v_ref, oh_ref, kaug_ref, vt_ref):
    kaug_ref[:, 0:KV_WIDTH] = k_ref[...].astype(BF16)
    kaug_ref[:, KV_WIDTH:] = oh_ref[...]
    vt = v_ref[...].T.astype(BF16)
    ones = jnp.ones((VSUM_ROWS, vt.shape[1]), BF16)
    for h in range(N_KV_HEADS):
        vt_ref[0, h * VT_ROWS:h * VT_ROWS + HEAD_DIM, :] = vt[h * HEAD_DIM:(h + 1) * HEAD_DIM, :]
        vt_ref[0, h * VT_ROWS + HEAD_DIM:(h + 1) * VT_ROWS, :] = ones


def _sel_prep(k, v, oh, *, tk):
    n = k.shape[0]
    tiles_per_seq = oh.shape[0] // tk
    nsel = oh.shape[1]
    return pl.pallas_call(
        _sel_prep_kernel, grid=(n // tk,),
        in_specs=[pl.BlockSpec((tk, KV_WIDTH), lambda i: (i, 0)), pl.BlockSpec((tk, KV_WIDTH), lambda i: (i, 0)),
                  pl.BlockSpec((tk, nsel), lambda i: (i % tiles_per_seq, 0))],
        out_specs=(pl.BlockSpec((tk, KV_WIDTH + nsel), lambda i: (i, 0)),
                   pl.BlockSpec((1, N_KV_HEADS * VT_ROWS, tk), lambda i: (i, 0, 0))),
        out_shape=(jax.ShapeDtypeStruct((n, KV_WIDTH + nsel), BF16),
                   jax.ShapeDtypeStruct((n // tk, N_KV_HEADS * VT_ROWS, tk), BF16)),
        compiler_params=_cparams(("parallel",)), name="sel_prep",
    )(k, v, oh)


def _sel_sample_kernel(qaug_ref, k_ref, oh_ref, v_ref, kn_ref, ohn_ref, vn_ref, t_ref, o_ref,
                       m_sc, l_sc, acc_sc, *, tq, tk, past_len):
    ki = pl.program_id(1)
    q1 = qaug_ref[0, 0:KV_WIDTH, :]
    sb = qaug_ref[0, KV_WIDTH:, :]
    t = t_ref[0]

    @pl.when(ki == 0)
    def _():
        _flash_init(m_sc, l_sc, acc_sc)

    s = _sel_scores(q1, sb, k_ref[0], oh_ref[...], ki * tk, t, causal=False)
    _flash_update(s, v_ref[0].T.astype(BF16), m_sc, l_sc, acc_sc)

    @pl.when(ki == pl.num_programs(1) - 1)
    def _():
        sn = _sel_scores(q1, sb, kn_ref[0], ohn_ref[...], past_len, t)
        _flash_update(sn, vn_ref[0].T.astype(BF16), m_sc, l_sc, acc_sc)
        _flash_finish(o_ref, l_sc, acc_sc, tq)


def _sel_sample(qaug, k, oh, v, kn, ohn, vn, tl, *, tq, tk, past_len):
    b, r, L = qaug.shape
    nk = past_len // tk
    nsel = oh.shape[1]
    tn = kn.shape[1]
    return pl.pallas_call(
        functools.partial(_sel_sample_kernel, tq=tq, tk=tk, past_len=past_len),
        grid=(b, nk),
        in_specs=[pl.BlockSpec((1, r, L), lambda i, j: (i, 0, 0)),
                  pl.BlockSpec((1, tk, KV_WIDTH), lambda i, j: (i, j, 0)),
                  pl.BlockSpec((tk, nsel), lambda i, j: (j, 0)),
                  pl.BlockSpec((1, tk, KV_WIDTH), lambda i, j: (i, j, 0)),
                  pl.BlockSpec((1, tn, KV_WIDTH), lambda i, j: (i, 0, 0)),
                  pl.BlockSpec((tn, nsel), lambda i, j: (0, 0)),
                  pl.BlockSpec((1, tn, KV_WIDTH), lambda i, j: (i, 0, 0)),
                  pl.BlockSpec((1, 1, L), lambda i, j: (0, 0, 0))],
        out_specs=pl.BlockSpec((1, N_HEADS * HEAD_DIM, tq), lambda i, j: (i, 0, 0)),
        out_shape=jax.ShapeDtypeStruct((b, N_HEADS * HEAD_DIM, tq), F32),
        scratch_shapes=[pltpu.VMEM((1, L), F32), pltpu.VMEM((1, L), F32),
                        pltpu.VMEM((N_KV_HEADS, HEAD_DIM, L // N_KV_HEADS), F32)],
        compiler_params=_cparams(("parallel", "arbitrary"), VMEM_LIMIT), name="sel_sample",
    )(qaug, k, oh, v, kn, ohn, vn, tl)


def _window_kernel(q_ref, k_ref, v_ref, t_ref, o_ref, *, tq, band, kpos_base):
    qi = pl.program_id(1)
    L = N_HEADS * tq
    hl = L // N_KV_HEADS
    q1 = q_ref[0]
    t = t_ref[0]
    start = pl.multiple_of(jnp.maximum(qi * tq + tq - band, 0), LANE)
    kb = k_ref[0, pl.ds(start, band), :].astype(BF16)
    s = _dot(kb, q1)
    kpos = kpos_base + start + lax.broadcasted_iota(I32, s.shape, 0)
    mask = (kpos <= t) & (kpos > t - WINDOW)
    s = jnp.where(mask, s, NEG)
    m = jnp.max(s, axis=0, keepdims=True)
    p = jnp.where(mask, jnp.exp2(s - m), 0.0)
    linv = 1.0 / jnp.maximum(jnp.sum(p, axis=0, keepdims=True), TINY)
    pb = p.astype(BF16)
    vt = v_ref[0, pl.ds(start, band), :].T.astype(BF16)
    for h in range(N_KV_HEADS):
        o_h = _dot(vt[HEAD_DIM * h:HEAD_DIM * (h + 1), :], pb[:, h * hl:(h + 1) * hl])
        for g in range(GQA):
            hd = h * GQA + g
            o_ref[0, HEAD_DIM * hd:HEAD_DIM * (hd + 1), :] = o_h[:, g * tq:(g + 1) * tq] * linv[:, hd * tq:(hd + 1) * tq]


def _window(qblk, k, v, tl, *, tq, band, kpos_base):
    nq, _, L = qblk.shape
    b, tlen, _ = k.shape
    nqb = nq // b
    shared_t = tl.shape[0] == 1
    return pl.pallas_call(
        functools.partial(_window_kernel, tq=tq, band=band, kpos_base=kpos_base),
        grid=(b, nqb),
        in_specs=[pl.BlockSpec((1, KV_WIDTH, L), lambda i, j: (i * nqb + j, 0, 0)),
                  pl.BlockSpec((1, tlen, KV_WIDTH), lambda i, j: (i, 0, 0)),
                  pl.BlockSpec((1, tlen, KV_WIDTH), lambda i, j: (i, 0, 0)),
                  pl.BlockSpec((1, 1, L), (lambda i, j: (0, 0, 0)) if shared_t else (lambda i, j: (j, 0, 0)))],
        out_specs=pl.BlockSpec((1, N_HEADS * HEAD_DIM, tq), lambda i, j: (i * nqb + j, 0, 0)),
        out_shape=jax.ShapeDtypeStruct((nq, N_HEADS * HEAD_DIM, tq), F32),
        compiler_params=_cparams(("parallel", "parallel"), VMEM_LIMIT), name="window",
    )(qblk, k, v, tl)


def _combine_kernel(x_ref, a_ref, oc_ref, os_ref, ow_ref, gt_ref, e_ref, woa_ref, wob_ref, o_ref, *, tm):
    hi, lo = _split_bf16(gt_ref[...])
    ge = _dot(e_ref[...], hi) + _dot(e_ref[...], lo)
    bw = N_HEADS * HEAD_DIM
    parts = []
    for j in range(tm // LANE):
        ls = slice(j * LANE, (j + 1) * LANE)
        parts.append(ge[0:bw, ls] * oc_ref[j] + ge[bw:2 * bw, ls] * os_ref[j] + ge[2 * bw:3 * bw, ls] * ow_ref[j])
    mixt = parts[0] if len(parts) == 1 else jnp.concatenate(parts, axis=1)
    mix = mixt.T.astype(BF16)
    o_ref[...] = x_ref[...] + _dot(a_ref[...], woa_ref[...]) + _dot(mix, wob_ref[...])


def _combine(x, a, oc, os_, ow, gt, e, woa, wob, *, tm):
    n, dm = x.shape
    bw = N_HEADS * HEAD_DIM
    full = lambda arr: pl.BlockSpec(arr.shape, lambda i: (0,) * arr.ndim)
    ospec = pl.BlockSpec((tm // LANE, bw, LANE), lambda i: (i, 0, 0))
    return pl.pallas_call(
        functools.partial(_combine_kernel, tm=tm),
        grid=(n // tm,),
        in_specs=[pl.BlockSpec((tm, dm), lambda i: (i, 0)), pl.BlockSpec((tm, a.shape[1]), lambda i: (i, 0)),
                  ospec, ospec, ospec, pl.BlockSpec((32, tm), lambda i: (0, i)), full(e), full(woa), full(wob)],
        out_specs=pl.BlockSpec((tm, dm), lambda i: (i, 0)),
        out_shape=jax.ShapeDtypeStruct((n, dm), F32),
        compiler_params=_cparams(("parallel",), VMEM_LIMIT), name="combine",
    )(x, a, oc, os_, ow, gt, e, woa, wob)


def _ffn_dense_kernel(x_ref, g_ref, wg_ref, wu_ref, wd_ref, o_ref):
    xf = x_ref[...]
    h = _rms(xf, g_ref[...]).astype(BF16)
    gate = _dot(h, wg_ref[...])
    up = _dot(h, wu_ref[...])
    act = (gate * _sigmoid(gate) * up).astype(BF16)
    o_ref[...] = xf + _dot(act, wd_ref[...])


def _ffn_dense(x, g, wg, wu, wd, *, tm):
    n, dm = x.shape
    full = lambda arr: pl.BlockSpec(arr.shape, lambda i: (0,) * arr.ndim)
    return pl.pallas_call(
        _ffn_dense_kernel, grid=(n // tm,),
        in_specs=[pl.BlockSpec((tm, dm), lambda i: (i, 0)), full(g), full(wg), full(wu), full(wd)],
        out_specs=pl.BlockSpec((tm, dm), lambda i: (i, 0)),
        out_shape=jax.ShapeDtypeStruct((n, dm), F32),
        compiler_params=_cparams(("parallel",), VMEM_LIMIT), name="ffn_dense",
    )(x, g, wg, wu, wd)


def _router_kernel(x_ref, g_ref, wrh_ref, wrl_ref, tri_ref, h_ref, meta_ref, cnt_ref, *, tm, n_exp):
    h = _rms(x_ref[...], g_ref[...])
    h_ref[...] = h
    hi, lo = _split_bf16(h)
    logits = _dot(hi, wrh_ref[...]) + _dot(lo, wrh_ref[...]) + _dot(hi, wrl_ref[...])
    lane = lax.broadcasted_iota(I32, (tm, LANE), 1)
    lanef = lane.astype(F32)
    logits = jnp.where(lane < n_exp, logits, NEG)
    m1 = jnp.max(logits, axis=1, keepdims=True)
    i1 = jnp.min(jnp.where(logits == m1, lanef, 1e9), axis=1, keepdims=True)
    rest = jnp.where(lanef == i1, -3e38, logits)
    m2 = jnp.max(rest, axis=1, keepdims=True)
    i2 = jnp.min(jnp.where(rest == m2, lanef, 1e9), axis=1, keepdims=True)
    e2 = jnp.exp(m2 - m1)
    w1 = 1.0 / (1.0 + e2)
    w2 = e2 / (1.0 + e2)
    hit1 = lanef == i1
    hit2 = lanef == i2
    msel = jnp.where(hit1 | hit2, 1.0, 0.0)
    ranks = _dot(tri_ref[...], msel.astype(BF16))
    r1 = jnp.sum(jnp.where(hit1, ranks, 0.0), axis=1, keepdims=True)
    r2 = jnp.sum(jnp.where(hit2, ranks, 0.0), axis=1, keepdims=True)
    meta = jnp.where(lane == 0, i1, 0.0)
    for k, val in enumerate((i2, w1, w2, r1, r2)):
        meta = jnp.where(lane == k + 1, val, meta)
    meta_ref[...] = meta
    cnt_ref[0] = jnp.sum(msel, axis=0, keepdims=True)


def _router(x, g, wrh, wrl, tri, *, tm, n_exp):
    n, dm = x.shape
    full = lambda arr: pl.BlockSpec(arr.shape, lambda i: (0,) * arr.ndim)
    return pl.pallas_call(
        functools.partial(_router_kernel, tm=tm, n_exp=n_exp), grid=(n // tm,),
        in_specs=[pl.BlockSpec((tm, dm), lambda i: (i, 0)), full(g), full(wrh), full(wrl), full(tri)],
        out_specs=(pl.BlockSpec((tm, dm), lambda i: (i, 0)), pl.BlockSpec((tm, LANE), lambda i: (i, 0)),
                   pl.BlockSpec((1, 1, LANE), lambda i: (i, 0, 0))),
        out_shape=(jax.ShapeDtypeStruct((n, dm), F32), jax.ShapeDtypeStruct((n, LANE), F32),
                   jax.ShapeDtypeStruct((n // tm, 1, LANE), F32)),
        compiler_params=_cparams(("parallel",), VMEM_LIMIT), name="router",
    )(x, g, wrh, wrl, tri)


def _row_copy(src, src_row, dst, dst_row, sem):
    return pltpu.make_async_copy(src.at[pl.ds(src_row, 1)], dst.at[pl.ds(dst_row, 1)], sem)


def _dispatch_kernel(dest_ref, h_ref, xg_in_ref, xg_ref, sem, *, tm):
    del xg_in_ref

    def issue(r, carry):
        for k in range(TOP_K):
            _row_copy(h_ref, r, xg_ref, dest_ref[TOP_K * r + k], sem).start()
        return carry

    lax.fori_loop(0, tm, issue, 0)

    def drain(r, carry):
        for k in range(TOP_K):
            _row_copy(h_ref, 0, xg_ref, 0, sem).wait()
        return carry

    lax.fori_loop(0, tm, drain, 0)


def _dispatch(dest, h, xg0, *, tm):
    n, dm = h.shape
    return pl.pallas_call(
        functools.partial(_dispatch_kernel, tm=tm), grid=(n // tm,),
        in_specs=[pl.BlockSpec((TOP_K * tm,), lambda i: (i,), memory_space=pltpu.SMEM),
                  pl.BlockSpec((tm, dm), lambda i: (i, 0)),
                  pl.BlockSpec(memory_space=pl.ANY)],
        out_specs=pl.BlockSpec(memory_space=pl.ANY),
        out_shape=jax.ShapeDtypeStruct(xg0.shape, F32),
        scratch_shapes=[pltpu.SemaphoreType.DMA],
        input_output_aliases={2: 0},
        compiler_params=_cparams(("arbitrary",), VMEM_LIMIT), name="moe_dispatch",
    )(dest, h, xg0)


def _expert_ffn_kernel(te_ref, nused_ref, x_ref, wg_ref, wu_ref, wd_ref, o_ref):
    s = pl.program_id(0)

    @pl.when(s < nused_ref[0])
    def _():
        h = x_ref[...].astype(BF16)
        gate = _dot(h, wg_ref[0])
        up = _dot(h, wu_ref[0])
        act = (gate * _sigmoid(gate) * up).astype(BF16)
        o_ref[...] = _dot(act, wd_ref[0])

    @pl.when(s >= nused_ref[0])
    def _():
        o_ref[...] = jnp.zeros(o_ref.shape, F32)


def _expert_ffn(te, nused, xg, wg, wu, wd, *, ts):
    s_rows, dm = xg.shape
    dff = wg.shape[2]
    grid_spec = pltpu.PrefetchScalarGridSpec(
        num_scalar_prefetch=2, grid=(s_rows // ts,),
        in_specs=[pl.BlockSpec((ts, dm), lambda i, te, nu: (i, 0)),
                  pl.BlockSpec((1, dm, dff), lambda i, te, nu: (te[i], 0, 0)),
                  pl.BlockSpec((1, dm, dff), lambda i, te, nu: (te[i], 0, 0)),
                  pl.BlockSpec((1, dff, dm), lambda i, te, nu: (te[i], 0, 0))],
        out_specs=pl.BlockSpec((ts, dm), lambda i, te, nu: (i, 0)))
    return pl.pallas_call(
        _expert_ffn_kernel, grid_spec=grid_spec,
        out_shape=jax.ShapeDtypeStruct((s_rows, dm), F32),
        compiler_params=_cparams(("arbitrary",), VMEM_LIMIT), name="expert_ffn",
    )(te, nused, xg, wg, wu, wd)


def _moe_combine_kernel(dest_ref, x_ref, meta_ref, y_ref, o_ref, buf, sem, *, tm):
    def issue(r, carry):
        for k in range(TOP_K):
            _row_copy(y_ref, dest_ref[TOP_K * r + k], buf.at[k], r, sem).start()
        return carry

    lax.fori_loop(0, tm, issue, 0)

    def drain(r, carry):
        for k in range(TOP_K):
            _row_copy(y_ref, 0, buf.at[k], 0, sem).wait()
        return carry

    lax.fori_loop(0, tm, drain, 0)
    meta = meta_ref[...]
    o_ref[...] = x_ref[...] + meta[:, 2:3] * buf[0] + meta[:, 3:4] * buf[1]


def _moe_combine(dest, x, meta, y, *, tm):
    n, dm = x.shape
    return pl.pallas_call(
        functools.partial(_moe_combine_kernel, tm=tm), grid=(n // tm,),
        in_specs=[pl.BlockSpec((TOP_K * tm,), lambda i: (i,), memory_space=pltpu.SMEM),
                  pl.BlockSpec((tm, dm), lambda i: (i, 0)),
                  pl.BlockSpec((tm, LANE), lambda i: (i, 0)),
                  pl.BlockSpec(memory_space=pl.ANY)],
        out_specs=pl.BlockSpec((tm, dm), lambda i: (i, 0)),
        out_shape=jax.ShapeDtypeStruct((n, dm), F32),
        scratch_shapes=[pltpu.VMEM((TOP_K, tm, dm), F32), pltpu.SemaphoreType.DMA],
        compiler_params=_cparams(("arbitrary",), VMEM_LIMIT), name="moe_combine",
    )(dest, x, meta, y)


def _moe(x, g, wr, wg, wu, wd, *, tm, ts):
    n, dm = x.shape
    n_exp = wr.shape[1]
    wr_pad = jnp.zeros((dm, LANE), F32).at[:, :n_exp].set(wr)
    wrh, wrl = _split_bf16(wr_pad)
    tri = jnp.tril(jnp.ones((tm, tm), F32), -1).astype(BF16)
    h, meta, cnt = _router(x, g, wrh, wrl, tri, tm=tm, n_exp=n_exp)
    cnt = cnt[:, 0, :n_exp].astype(I32)
    tile_off = jnp.cumsum(cnt, axis=0) - cnt
    tot = jnp.sum(cnt, axis=0)
    padded = ((tot + ts - 1) // ts) * ts
    gend = jnp.cumsum(padded)
    gstart = gend - padded
    ei = meta[:, 0:2].astype(I32)
    rk = meta[:, 4:6].astype(I32)
    base = jnp.broadcast_to((gstart[None, :] + tile_off)[:, None, :], (n // tm, tm, n_exp)).reshape(n, 1, n_exp)
    pick = ei[:, :, None] == jnp.arange(n_exp, dtype=I32)[None, None, :]
    dest = (jnp.sum(jnp.where(pick, base, 0), axis=-1) + rk).reshape(-1)
    n_slots = TOP_K * n + n_exp * ts
    n_slot_tiles = n_slots // ts
    nused = (gend[-1] // ts).astype(I32).reshape(1)
    tile_start = jnp.minimum(jnp.arange(n_slot_tiles, dtype=I32), jnp.maximum(nused[0] - 1, 0)) * ts
    te = jnp.minimum(jnp.sum((gend[None, :] <= tile_start[:, None]).astype(I32), axis=1), n_exp - 1)
    xg = _dispatch(dest, h, jnp.zeros((n_slots, dm), F32), tm=tm)
    y = _expert_ffn(te, nused, xg, wg, wu, wd, ts=ts)
    return _moe_combine(dest, x, meta, y, tm=tm)


def _norm_kernel(x_ref, g_ref, o_ref):
    o_ref[...] = _rms(x_ref[...], g_ref[...])


def _final_norm(x, g, *, tm):
    n, dm = x.shape
    return pl.pallas_call(
        _norm_kernel, grid=(n // tm,),
        in_specs=[pl.BlockSpec((tm, dm), lambda i: (i, 0)), pl.BlockSpec(g.shape, lambda i: (0, 0))],
        out_specs=pl.BlockSpec((tm, dm), lambda i: (i, 0)),
        out_shape=jax.ShapeDtypeStruct((n, dm), F32),
        compiler_params=_cparams(("parallel",)), name="final_norm",
    )(x, g)


PAGES_PER_STEP = 8


def _page_gather_kernel(pt_ref, *refs, n_chunked):
    del pt_ref
    g = PAGES_PER_STEP
    n_arr = (len(refs) - 1) // (g + 1)
    pages, outs, scr = refs[:n_arr * g], refs[n_arr * g:n_arr * (g + 1)], refs[-1]
    for c in range(n_arr):
        for i in range(g):
            p = pages[c * g + i][0, 0].T
            npos = p.shape[0]
            if c < n_chunked:
                scr[...] = p
                rows = npos // CMP_STRIDE
                for s in range(CMP_STRIDE):
                    outs[c][0, i * rows:(i + 1) * rows, s * KV_WIDTH:(s + 1) * KV_WIDTH] = scr[pl.ds(s, rows, stride=CMP_STRIDE), :]
            else:
                outs[c][0, i * npos:(i + 1) * npos, :] = p


def _page_gather(page_table, pools_t, layer, *, n_chunked):
    b, n_pages = page_table.shape
    page = pools_t[0].shape[3]
    g = PAGES_PER_STEP
    n_arr = len(pools_t)
    in_specs, operands = [], []
    for p in pools_t:
        for i in range(g):
            in_specs.append(pl.BlockSpec((1, 1, KV_WIDTH, page), lambda bi, j, pt, i=i: (layer, pt[bi, j * g + i], 0, 0)))
            operands.append(p)
    rows = page // CMP_STRIDE
    out_specs, out_shape = [], []
    for c in range(n_arr):
        if c < n_chunked:
            out_specs.append(pl.BlockSpec((1, g * rows, CMP_STRIDE * KV_WIDTH), lambda bi, j, pt: (bi, j, 0)))
            out_shape.append(jax.ShapeDtypeStruct((b, n_pages * rows, CMP_STRIDE * KV_WIDTH), F32))
        else:
            out_specs.append(pl.BlockSpec((1, g * page, KV_WIDTH), lambda bi, j, pt: (bi, j, 0)))
            out_shape.append(jax.ShapeDtypeStruct((b, n_pages * page, KV_WIDTH), F32))
    grid_spec = pltpu.PrefetchScalarGridSpec(
        num_scalar_prefetch=1, grid=(b, n_pages // g), in_specs=in_specs, out_specs=out_specs,
        scratch_shapes=[pltpu.VMEM((page, KV_WIDTH), F32)])
    return pl.pallas_call(
        functools.partial(_page_gather_kernel, n_chunked=n_chunked),
        grid_spec=grid_spec, out_shape=out_shape,
        compiler_params=_cparams(("parallel", "parallel"), VMEM_LIMIT), name="page_gather",
    )(page_table, *operands)


def _rope_tables(pos):
    half = HEAD_DIM // 2
    inv = ROPE_THETA ** (-jnp.arange(half, dtype=F32) / half)
    ang = pos.astype(F32)[:, None] * inv[None, :]
    cos, sin = jnp.cos(ang), jnp.sin(ang)
    cos_std = jnp.tile(cos, (1, 4))
    sin_std = jnp.tile(jnp.concatenate([-sin, sin], axis=1), (1, 2))
    return cos_std, sin_std, cos.T, sin.T


def _cmp_to_sel_t(n_cmp_pad, n_sel_pad):
    i0 = np.arange(n_cmp_pad, dtype=np.int64)[None, :] * CMP_STRIDE
    j0 = np.arange(n_sel_pad, dtype=np.int64)[:, None] * SEL_LEN
    ov = np.clip(np.minimum(i0 + CMP_LEN, j0 + SEL_LEN) - np.maximum(i0, j0), 0, CMP_LEN)
    return jnp.asarray(ov.astype(np.float32) / CMP_LEN, dtype=BF16)


def _block_onehot(n_keys, n_sel_pad):
    blk = np.arange(n_keys)[:, None] // SEL_LEN
    return jnp.asarray((blk == np.arange(n_sel_pad)[None, :]).astype(np.float32), dtype=BF16)


def _lane_positions(pos_tiles, tq):
    return jnp.tile(pos_tiles.astype(I32), (1, N_HEADS))[:, None, :]


def _compress_weights(w1, w2, pe):
    nr = CMP_LEN // CMP_STRIDE
    w1r = w1.reshape(nr, CMP_STRIDE, HEAD_DIM, CMP_HIDDEN)
    eye = jnp.eye(N_KV_HEADS, dtype=F32)
    wbig = jnp.einsum("rsde,gh->sgdrhe", w1r, eye)
    wbig = wbig.reshape(CMP_STRIDE * KV_WIDTH, nr * N_KV_HEADS * CMP_HIDDEN).astype(BF16)
    w2big = jnp.einsum("ed,gh->gehd", w2, eye).reshape(N_KV_HEADS * CMP_HIDDEN, KV_WIDTH).astype(BF16)
    pe8 = jnp.tile(pe.reshape(1, CMP_LEN * HEAD_DIM), (8, 1)).astype(BF16)
    return wbig, pe8, w1.astype(BF16), w2big


def _layer_weights(l, g_mix, w_in, g_sgu, w_sgu, b_sgu, w_cmpk1, w_cmpk2, pe_cmpk, w_cmpv1, w_cmpv2, pe_cmpv, w_o, dec_t):
    aw = A_GROUPS * CHUNK
    bw = N_HEADS * HEAD_DIM
    w = w_in[l]
    wstd = jnp.concatenate([w[:, :2 * aw], w[:, 2 * aw + bw:2 * aw + bw + 6 * KV_WIDTH]], axis=1).astype(BF16)
    n_gate = 3 * N_HEADS
    wt = jnp.concatenate([w[:, 2 * aw:2 * aw + bw], w[:, -n_gate:], jnp.zeros((w.shape[0], 32 - n_gate), F32)], axis=1).T.astype(BF16)
    causal = jnp.tril(jnp.ones((CHUNK, CHUNK), bool))
    wsgu_p = jnp.where(causal[None], w_sgu[l], 0.0).astype(BF16)
    bsgu_p = jnp.broadcast_to(b_sgu[l][:, :, None], (A_GROUPS, CHUNK, CHUNK)).astype(F32)
    reps = CHUNK // dec_t
    wsmall = jnp.where(causal[None, :dec_t, :dec_t], w_sgu[l][:, :dec_t, :dec_t], 0.0)
    wsgu_s = jnp.einsum("ab,gts->gatbs", jnp.eye(reps, dtype=F32), wsmall).reshape(A_GROUPS, CHUNK, CHUNK).astype(BF16)
    bsgu_s = jnp.broadcast_to(jnp.tile(b_sgu[l][:, :dec_t], (1, reps))[:, :, None], (A_GROUPS, CHUNK, CHUNK)).astype(F32)
    ck_w = _compress_weights(w_cmpk1[l], w_cmpk2[l], pe_cmpk[l])
    cv_w = _compress_weights(w_cmpv1[l], w_cmpv2[l], pe_cmpv[l])
    cv_w = cv_w[:3] + (cv_w[3].T,)
    e = np.zeros((3 * bw, 32), np.float32)
    for br in range(3):
        for hd in range(N_HEADS):
            e[br * bw + hd * HEAD_DIM:br * bw + (hd + 1) * HEAD_DIM, br * N_HEADS + hd] = 1.0
    return dict(gmix=g_mix[l][None, :], wstd=wstd, wt=wt, gsgu=g_sgu[l][None, :],
                wsgu_p=wsgu_p, bsgu_p=bsgu_p, wsgu_s=wsgu_s, bsgu_s=bsgu_s, ck_w=ck_w, cv_w=cv_w,
                e=jnp.asarray(e, dtype=BF16), woa=w_o[l][:aw].astype(BF16), wob=w_o[l][aw:].astype(BF16))


def _channel_mixer(l, x, g_ffn, dense_w, w_router, moe_w, *, tm_dense, tm_moe):
    g = g_ffn[l][None, :]
    i = l // 2
    if l % 2 == 0:
        return _ffn_dense(x, g, *(w[i] for w in dense_w), tm=tm_dense)
    return _moe(x, g, w_router[i], *(w[i] for w in moe_w), tm=tm_moe, ts=256)


def kernel(x_prompt, x_sample, cache_cmp_k, cache_cmp_v, cache_sel_k, cache_sel_v, cache_win_k, cache_win_v, page_table,
           g_mix, w_in, g_sgu, w_sgu, b_sgu, w_cmpk1, w_cmpk2, pe_cmpk, w_cmpv1, w_cmpv2, pe_cmpv, w_o, g_ffn,
           w_ff_gate, w_ff_up, w_ff_down, w_router, w_moe_gate, w_moe_up, w_moe_down, g_final):
    bsz, seq, dm = x_prompt.shape
    dec_b, dec_t, _ = x_sample.shape
    depth = g_mix.shape[0]
    page = cache_cmp_k.shape[2]
    past_len = page_table.shape[1] * page
    wbuf = cache_win_k.shape[2]
    n_p, n_s = bsz * seq, dec_b * dec_t
    assert n_s == CHUNK and seq % 512 == 0 and wbuf == WINDOW and past_len % 2048 == 0
    tq_p, tq_s = LANE, LANE // N_HEADS
    nq_b = seq // tq_p
    band = WINDOW + tq_p

    pos_p = jnp.tile(jnp.arange(seq, dtype=I32), bsz)
    pos_s = jnp.tile(past_len + jnp.arange(dec_t, dtype=I32), dec_b)
    rope_p = _rope_tables(pos_p)
    rope_s = _rope_tables(pos_s)
    tl_p = _lane_positions(jnp.arange(seq, dtype=I32).reshape(nq_b, tq_p), tq_p)
    tok_s = past_len + jnp.minimum(jnp.arange(tq_s, dtype=I32), dec_t - 1)
    tl_s = _lane_positions(tok_s[None, :], tq_s)
    nch_p = seq // CMP_STRIDE
    nsel_p = seq // SEL_LEN
    nch_s = past_len // CMP_STRIDE
    nsel_s = -(-(past_len // SEL_LEN + 1) // LANE) * LANE
    mt_p = _cmp_to_sel_t(nch_p, nsel_p)
    mt_s = _cmp_to_sel_t(nch_s, nsel_s)
    oh_p = _block_onehot(seq, nsel_p)
    oh_s = _block_onehot(past_len + SEL_LEN, nsel_s)
    head_of_row = jnp.asarray((np.arange(N_HEADS)[None, :] // GQA == np.arange(N_KV_HEADS)[:, None]).astype(np.float32))

    pools_t = [jnp.transpose(c, (0, 1, 3, 4, 2)).reshape(depth, c.shape[1], KV_WIDTH, page)
               for c in (cache_cmp_k, cache_cmp_v, cache_sel_k, cache_sel_v)]
    flat = lambda a: a.reshape(a.shape[0], a.shape[1], KV_WIDTH)

    dense_w = tuple(w.astype(BF16) for w in (w_ff_gate, w_ff_up, w_ff_down))
    moe_w = tuple(w.astype(BF16) for w in (w_moe_gate, w_moe_up, w_moe_down))
    xp = x_prompt.reshape(n_p, dm)
    xs = x_sample.reshape(n_s, dm)
    outs = [[] for _ in range(13)]
    for l in range(depth):
        lw = _layer_weights(l, g_mix, w_in, g_sgu, w_sgu, b_sgu, w_cmpk1, w_cmpk2, pe_cmpk, w_cmpv1, w_cmpv2, pe_cmpv, w_o, dec_t)
        a, _, kc, vc, ks, vs, kw, vw, qblk, gt = _inproj(
            xp, lw["gmix"], lw["wstd"], lw["wt"], lw["gsgu"], lw["wsgu_p"], lw["bsgu_p"], *rope_p, tm=512, emit_qblk=True)
        ck = _compress(kc.reshape(bsz, nch_p, CMP_STRIDE * KV_WIDTH), *lw["ck_w"], transpose_out=False)
        cvt = _compress(vc.reshape(bsz, nch_p, CMP_STRIDE * KV_WIDTH), *lw["cv_w"], transpose_out=True)
        oc, qaug = _cmp_select(qblk, ck, cvt, mt_p, tl_p, tq=tq_p, nq_per_b=nq_b)
        kaug, vst = _sel_prep(ks, vs, oh_p, tk=512)
        osel = _sel_prompt(qaug, kaug, vst, tl_p, tq=tq_p, tk=512, bsz=bsz)
        ow = _window(qblk, kw.reshape(bsz, seq, KV_WIDTH), vw.reshape(bsz, seq, KV_WIDTH), tl_p, tq=tq_p, band=band, kpos_base=0)
        xp = _combine(xp, a, oc, osel, ow, gt, lw["e"], lw["woa"], lw["wob"], tm=512)
        kv4 = lambda t: t.reshape(bsz, seq, N_KV_HEADS, HEAD_DIM)
        for idx, t in enumerate((kc, vc, ks, vs)):
            outs[idx].append(kv4(t))
        nwin_p = min(WINDOW, seq)
        outs[4].append(kv4(kw)[:, seq - nwin_p:])
        outs[5].append(kv4(vw)[:, seq - nwin_p:])

        a, v_s, kc, vc, ks, vs, kw, vw, qt, gt = _inproj(
            xs, lw["gmix"], lw["wstd"], lw["wt"], lw["gsgu"], lw["wsgu_s"], lw["bsgu_s"], *rope_s, tm=CHUNK, emit_qblk=False)
        q4 = qt.reshape(N_HEADS, HEAD_DIM, dec_b, dec_t)
        q4 = jnp.pad(q4, ((0, 0), (0, 0), (0, 0), (0, tq_s - dec_t)))
        qb = jnp.einsum("hdbt,gh->bgdht", q4.astype(F32), head_of_row).astype(BF16)
        qblk_s = qb.reshape(dec_b, KV_WIDTH, N_HEADS * tq_s)
        past_ck, past_cv, past_sk, past_sv = _page_gather(page_table, pools_t, l, n_chunked=2)
        ck = _compress(past_ck, *lw["ck_w"], transpose_out=False)
        cvt = _compress(past_cv, *lw["cv_w"], transpose_out=True)
        oc, qaug = _cmp_select(qblk_s, ck, cvt, mt_s, tl_s, tq=tq_s, nq_per_b=1)
        pad_new = lambda t: jnp.pad(t.reshape(dec_b, dec_t, KV_WIDTH), ((0, 0), (0, SEL_LEN - dec_t), (0, 0)))
        osel = _sel_sample(qaug, past_sk, oh_s, past_sv, pad_new(ks), oh_s[past_len:], pad_new(vs), tl_s,
                           tq=tq_s, tk=2048, past_len=past_len)
        kw_all = jnp.concatenate([flat(cache_win_k[l]), kw.reshape(dec_b, dec_t, KV_WIDTH)], axis=1)
        vw_all = jnp.concatenate([flat(cache_win_v[l]), vw.reshape(dec_b, dec_t, KV_WIDTH)], axis=1)
        wpad = ((0, 0), (0, band - wbuf - dec_t), (0, 0))
        ow = _window(qblk_s, jnp.pad(kw_all, wpad), jnp.pad(vw_all, wpad), tl_s, tq=tq_s, band=band, kpos_base=past_len - wbuf)
        untile = lambda o: o[:, :, :dec_t].transpose(1, 0, 2).reshape(1, N_HEADS * HEAD_DIM, n_s)
        xs = _combine(xs, a, untile(oc), untile(osel), untile(ow), gt, lw["e"], lw["woa"], lw["wob"], tm=CHUNK)
        kv4s = lambda t: t.reshape(dec_b, dec_t, N_KV_HEADS, HEAD_DIM)
        for idx, t in enumerate((kc, vc, ks, vs)):
            outs[6 + idx].append(kv4s(t))
        nwin_s = min(WINDOW, wbuf + dec_t)
        outs[10].append(kw_all[:, wbuf + dec_t - nwin_s:].reshape(dec_b, nwin_s, N_KV_HEADS, HEAD_DIM))
        outs[11].append(vw_all[:, wbuf + dec_t - nwin_s:].reshape(dec_b, nwin_s, N_KV_HEADS, HEAD_DIM))
        outs[12].append(v_s.reshape(dec_b, dec_t, A_GROUPS * CHUNK))

        xp = _channel_mixer(l, xp, g_ffn, dense_w, w_router, moe_w, tm_dense=256, tm_moe=512)
        xs = _channel_mixer(l, xs, g_ffn, dense_w, w_router, moe_w, tm_dense=CHUNK, tm_moe=CHUNK)

    gf = g_final[None, :]
    y_prompt = _final_norm(xp, gf, tm=512).reshape(bsz, seq, dm)
    y_sample = _final_norm(xs, gf, tm=CHUNK).reshape(dec_b, dec_t, dm)
    return (y_prompt, y_sample) + tuple(jnp.stack(o, axis=0) for o in outs)
```

```python
import functools
import math

import numpy as np
import jax
import jax.numpy as jnp
from jax import lax
from jax.experimental import pallas as pl
from jax.experimental.pallas import tpu as pltpu

F32 = jnp.float32
BF16 = jnp.bfloat16
I32 = jnp.int32

A_GROUPS = 4
N_HEADS = 8
N_KV_HEADS = 2
HEAD_DIM = 64
CMP_LEN = 32
CMP_STRIDE = 16
CMP_HIDDEN = 128
SEL_LEN = 64
N_SEL = 16
WINDOW = 512
CHUNK = 128
ROPE_THETA = 10000.0
TOP_K = 2
EPS = 1e-6
NEG = -1e30
BIG = 1e30
TINY = 1e-30
PICKED = -3e38
N_FORCED = 3
SCALE = HEAD_DIM ** -0.5
QSCALE = SCALE * math.log2(math.e)

LANE = 128
KV_WIDTH = N_KV_HEADS * HEAD_DIM
GQA = N_HEADS // N_KV_HEADS
VMEM_LIMIT = 56 * 1024 * 1024


def _cparams(sem, vmem=None):
    return pltpu.CompilerParams(dimension_semantics=sem, vmem_limit_bytes=vmem)


def _rms(xf, g):
    return xf * lax.rsqrt(jnp.mean(xf * xf, axis=-1, keepdims=True) + EPS) * g


def _gelu(x):
    c = math.sqrt(2.0 / math.pi)
    return x * (0.5 * (1.0 + jnp.tanh(c * (x + 0.044715 * (x * x * x)))))


def _sigmoid(x):
    return 1.0 / (1.0 + jnp.exp(-x))


def _dot(a, b):
    return jnp.dot(a, b, preferred_element_type=F32)


def _dot_nt(a, b):
    return lax.dot_general(a, b, (((1,), (1,)), ((), ())), preferred_element_type=F32)


def _split_bf16(x):
    hi = x.astype(BF16)
    lo = (x - hi.astype(F32)).astype(BF16)
    return hi, lo


def _inproj_kernel(x_ref, gmix_ref, wstd_ref, wt_ref, gsgu_ref, wsgu_ref, bsgu_ref,
                   cos_ref, sin_ref, cost_ref, sint_ref,
                   a_ref, v_ref, kc_ref, vc_ref, ks_ref, vs_ref, kw_ref, vw_ref, q_ref, gt_ref,
                   *, tm, emit_qblk):
    xf = x_ref[...]
    h = _rms(xf, gmix_ref[...]).astype(BF16)
    z = _dot(h, wstd_ref[...])
    zt = _dot_nt(wt_ref[...], h)
    aw = A_GROUPS * CHUNK
    u = _gelu(z[:, 0:aw])
    vv = _gelu(z[:, aw:2 * aw])
    mu = jnp.mean(vv, axis=-1, keepdims=True)
    d = vv - mu
    var = jnp.mean(d * d, axis=-1, keepdims=True)
    v = d * lax.rsqrt(var + EPS) * gsgu_ref[...]
    v_ref[...] = v
    vb = v.astype(BF16)
    nc = tm // CHUNK
    for g in range(A_GROUPS):
        gs = slice(g * CHUNK, (g + 1) * CHUNK)
        parts = [vb[c * CHUNK:(c + 1) * CHUNK, gs] for c in range(nc)]
        xg = parts[0] if nc == 1 else jnp.concatenate(parts, axis=1)
        yg = _dot(wsgu_ref[g], xg)
        for c in range(nc):
            cs = slice(c * CHUNK, (c + 1) * CHUNK)
            mixed = yg[:, cs] + bsgu_ref[g]
            a_ref[cs, gs] = (u[cs, gs] * mixed).astype(BF16)

    cosr = cos_ref[...]
    sinr = sin_ref[...]
    lane = lax.broadcasted_iota(I32, (tm, KV_WIDTH), 1)
    first = (lane % HEAD_DIM) < (HEAD_DIM // 2)

    def rope(x):
        rot = jnp.where(first, pltpu.roll(x, KV_WIDTH - HEAD_DIM // 2, 1), pltpu.roll(x, HEAD_DIM // 2, 1))
        return x * cosr + rot * sinr

    o = 2 * aw
    kc_ref[...] = rope(z[:, o:o + 128])
    vc_ref[...] = z[:, o + 128:o + 256]
    ks_ref[...] = rope(z[:, o + 256:o + 384])
    vs_ref[...] = z[:, o + 384:o + 512]
    kw_ref[...] = rope(z[:, o + 512:o + 640])
    vw_ref[...] = z[:, o + 640:o + 768]

    ct = cost_ref[...]
    st = sint_ref[...]
    half = HEAD_DIM // 2
    for hd in range(N_HEADS):
        x1 = zt[HEAD_DIM * hd:HEAD_DIM * hd + half]
        x2 = zt[HEAD_DIM * hd + half:HEAD_DIM * (hd + 1)]
        qh = jnp.concatenate([(x1 * ct - x2 * st) * QSCALE, (x2 * ct + x1 * st) * QSCALE], axis=0).astype(BF16)
        if emit_qblk:
            kvh = hd // GQA
            zero = jnp.zeros((HEAD_DIM, LANE), BF16)
            for j in range(tm // LANE):
                ls = slice(hd * LANE, (hd + 1) * LANE)
                q_ref[j, HEAD_DIM * kvh:HEAD_DIM * (kvh + 1), ls] = qh[:, j * LANE:(j + 1) * LANE]
                q_ref[j, HEAD_DIM * (1 - kvh):HEAD_DIM * (2 - kvh), ls] = zero
        else:
            q_ref[HEAD_DIM * hd:HEAD_DIM * (hd + 1), :] = qh
    nq = N_HEADS * HEAD_DIM
    gt_ref[...] = _sigmoid(zt[nq:nq + 32])


def _inproj(x, gmix, wstd, wt, gsgu, wsgu, bsgu, cos, sin, cost, sint, *, tm, emit_qblk):
    n, dm = x.shape
    nt = n // tm
    row = lambda w: pl.BlockSpec((tm, w), lambda i: (i, 0))
    full = lambda a: pl.BlockSpec(a.shape, lambda i: (0,) * a.ndim)
    if emit_qblk:
        q_shape = jax.ShapeDtypeStruct((n // LANE, KV_WIDTH, N_HEADS * LANE), BF16)
        q_spec = pl.BlockSpec((tm // LANE, KV_WIDTH, N_HEADS * LANE), lambda i: (i, 0, 0))
    else:
        q_shape = jax.ShapeDtypeStruct((N_HEADS * HEAD_DIM, n), BF16)
        q_spec = pl.BlockSpec((N_HEADS * HEAD_DIM, tm), lambda i: (0, i))
    kv = jax.ShapeDtypeStruct((n, KV_WIDTH), F32)
    out_shape = (jax.ShapeDtypeStruct((n, 512), BF16), jax.ShapeDtypeStruct((n, 512), F32),
                 kv, kv, kv, kv, kv, kv, q_shape, jax.ShapeDtypeStruct((32, n), F32))
    out_specs = (row(512), row(512), row(128), row(128), row(128), row(128), row(128), row(128), q_spec,
                 pl.BlockSpec((32, tm), lambda i: (0, i)))
    in_specs = [row(dm), full(gmix), full(wstd), full(wt), full(gsgu), full(wsgu), full(bsgu),
                row(128), row(128), pl.BlockSpec((32, tm), lambda i: (0, i)), pl.BlockSpec((32, tm), lambda i: (0, i))]
    return pl.pallas_call(
        functools.partial(_inproj_kernel, tm=tm, emit_qblk=emit_qblk),
        grid=(nt,), in_specs=in_specs, out_specs=out_specs, out_shape=out_shape,
        compiler_params=_cparams(("parallel",), VMEM_LIMIT), name="inproj",
    )(x, gmix, wstd, wt, gsgu, wsgu, bsgu, cos, sin, cost, sint)


def _compress_kernel(c_ref, wbig_ref, pe_ref, w1_ref, w2_ref, o_ref, *, transpose_out):
    c = c_ref[0].astype(BF16)
    ab = _dot(c, wbig_ref[...])
    nch = ab.shape[0]
    hw = N_KV_HEADS * CMP_HIDDEN
    peb = _dot(pe_ref[...], w1_ref[...])
    bias = jnp.concatenate([peb[0:1]] * N_KV_HEADS, axis=1)
    hh = ab[:, :hw] + pltpu.roll(ab[:, hw:], nch - 1, 0) + bias
    g = _gelu(hh).astype(BF16)
    if transpose_out:
        o_ref[0] = _dot_nt(w2_ref[...], g).astype(BF16)
    else:
        o_ref[0] = _dot(g, w2_ref[...]).astype(BF16)


def _compress(c, wbig, pe8, w1, w2, *, transpose_out):
    b, nch, cw = c.shape
    if transpose_out:
        out_shape = jax.ShapeDtypeStruct((b, KV_WIDTH, nch), BF16)
        out_spec = pl.BlockSpec((1, KV_WIDTH, nch), lambda i: (i, 0, 0))
    else:
        out_shape = jax.ShapeDtypeStruct((b, nch, KV_WIDTH), BF16)
        out_spec = pl.BlockSpec((1, nch, KV_WIDTH), lambda i: (i, 0, 0))
    full = lambda a: pl.BlockSpec(a.shape, lambda i: (0,) * a.ndim)
    return pl.pallas_call(
        functools.partial(_compress_kernel, transpose_out=transpose_out),
        grid=(b,), in_specs=[pl.BlockSpec((1, nch, cw), lambda i: (i, 0, 0)), full(wbig), full(pe8), full(w1), full(w2)],
        out_specs=out_spec, out_shape=out_shape,
        compiler_params=_cparams(("parallel",), VMEM_LIMIT), name="compress",
    )(c, wbig, pe8, w1, w2)


def _cmp_select_kernel(q_ref, ck_ref, cvt_ref, mt_ref, t_ref, oc_ref, qaug_ref, *, tq):
    L = N_HEADS * tq
    hl = L // N_KV_HEADS
    q1 = q_ref[0]
    t = t_ref[0]
    s = _dot(ck_ref[0], q1)
    nch = s.shape[0]
    n_last = (t - (CMP_LEN - 1)) // CMP_STRIDE
    s = jnp.where(lax.broadcasted_iota(I32, (nch, L), 0) <= n_last, s, NEG)
    m = jnp.max(s, axis=0, keepdims=True)
    p = jnp.exp2(s - m)
    inv = jnp.where(m > NEG / 2, 1.0 / jnp.maximum(jnp.sum(p, axis=0, keepdims=True), TINY), 0.0)
    p = p * inv
    pb = p.astype(BF16)
    mt = mt_ref[...]
    nsel = mt.shape[0]
    blk = lax.broadcasted_iota(I32, (nsel, tq), 0).astype(F32)
    cur = (t[:, 0:tq] // SEL_LEN).astype(F32)
    biases = []
    for h in range(N_KV_HEADS):
        o_h = _dot(cvt_ref[0, HEAD_DIM * h:HEAD_DIM * (h + 1), :], pb[:, h * hl:(h + 1) * hl])
        psum = p[:, h * hl:h * hl + tq]
        for g in range(GQA):
            hd = h * GQA + g
            oc_ref[0, HEAD_DIM * hd:HEAD_DIM * (hd + 1), :] = o_h[:, g * tq:(g + 1) * tq]
            if g > 0:
                psum = psum + p[:, hd * tq:(hd + 1) * tq]
        hi, lo = _split_bf16(psum)
        imp = _dot(mt, hi) + _dot(mt, lo)
        forced = (blk == 0.0) | (blk == cur) | (blk == cur - 1.0)
        past = blk <= cur
        sc = jnp.where(past & ~forced, imp, NEG)
        for _ in range(N_SEL - N_FORCED):
            mx = jnp.max(sc, axis=0, keepdims=True)
            idx = jnp.min(jnp.where(sc == mx, blk, 1e9), axis=0, keepdims=True)
            sc = jnp.where(blk == idx, PICKED, sc)
        sel = past & (forced | (sc < PICKED / 2) | (cur < float(N_SEL)))
        bias_h = jnp.where(sel, 0.0, NEG).astype(BF16)
        biases.extend([bias_h] * GQA)
    qaug_ref[0, 0:KV_WIDTH, :] = q1
    qaug_ref[0, KV_WIDTH:, :] = jnp.concatenate(biases, axis=1)


def _cmp_select(qblk, ck, cvt, mt, tl, *, tq, nq_per_b):
    nq, _, L = qblk.shape
    b, nch, _ = ck.shape
    nsel = mt.shape[0]
    shared_t = tl.shape[0] == 1
    return pl.pallas_call(
        functools.partial(_cmp_select_kernel, tq=tq),
        grid=(b, nq_per_b),
        in_specs=[pl.BlockSpec((1, KV_WIDTH, L), lambda i, j: (i * nq_per_b + j, 0, 0)),
                  pl.BlockSpec((1, nch, KV_WIDTH), lambda i, j: (i, 0, 0)),
                  pl.BlockSpec((1, KV_WIDTH, nch), lambda i, j: (i, 0, 0)),
                  pl.BlockSpec(mt.shape, lambda i, j: (0, 0)),
                  pl.BlockSpec((1, 1, L), (lambda i, j: (0, 0, 0)) if shared_t else (lambda i, j: (j, 0, 0)))],
        out_specs=(pl.BlockSpec((1, N_HEADS * HEAD_DIM, tq), lambda i, j: (i * nq_per_b + j, 0, 0)),
                   pl.BlockSpec((1, KV_WIDTH + nsel, L), lambda i, j: (i * nq_per_b + j, 0, 0))),
        out_shape=(jax.ShapeDtypeStruct((nq, N_HEADS * HEAD_DIM, tq), F32),
                   jax.ShapeDtypeStruct((nq, KV_WIDTH + nsel, L), BF16)),
        compiler_params=_cparams(("parallel", "parallel"), VMEM_LIMIT), name="cmp_select",
    )(qblk, ck, cvt, mt, tl)


def _flash_init(m_sc, l_sc, acc_sc):
    m_sc[...] = jnp.full(m_sc.shape, NEG, F32)
    l_sc[...] = jnp.zeros(l_sc.shape, F32)
    acc_sc[...] = jnp.zeros(acc_sc.shape, F32)


def _flash_update(s, vt, m_sc, l_sc, acc_sc):
    hl = s.shape[1] // N_KV_HEADS
    m_old = m_sc[...]
    m_new = jnp.maximum(m_old, jnp.max(s, axis=0, keepdims=True))
    alpha = jnp.exp2(m_old - m_new)
    p = jnp.exp2(s - m_new)
    l_sc[...] = alpha * l_sc[...] + jnp.sum(p, axis=0, keepdims=True)
    m_sc[...] = m_new
    pb = p.astype(BF16)
    for h in range(N_KV_HEADS):
        ls = slice(h * hl, (h + 1) * hl)
        acc_sc[h] = acc_sc[h] * alpha[:, ls] + _dot(vt[HEAD_DIM * h:HEAD_DIM * (h + 1), :], pb[:, ls])


def _flash_finish(o_ref, l_sc, acc_sc, tq):
    linv = 1.0 / jnp.maximum(l_sc[...], TINY)
    for hd in range(N_HEADS):
        h, g = divmod(hd, GQA)
        o_ref[0, HEAD_DIM * hd:HEAD_DIM * (hd + 1), :] = (
            acc_sc[h][:, g * tq:(g + 1) * tq] * linv[:, hd * tq:(hd + 1) * tq])


def _sel_scores(q1, sb, k, oh, kpos0, t, causal=True):
    s = _dot(k.astype(BF16), q1) + _dot(oh, sb)
    if not causal:
        return s
    kpos = kpos0 + lax.broadcasted_iota(I32, s.shape, 0)
    return jnp.where(kpos <= t, s, NEG)


VSUM_ROWS = 16
VT_ROWS = HEAD_DIM + VSUM_ROWS
SEL_TK = 512


def _sel_prompt_kernel(qaug_ref, kaug_ref, vt_ref, t_ref, o_ref, m_sc, acc_sc, sa_sc, sb_sc, *, tq, tk):
    qi = pl.program_id(1)
    L = N_HEADS * tq
    hl = L // N_KV_HEADS
    m_sc[...] = jnp.full(m_sc.shape, NEG, F32)
    acc_sc[...] = jnp.zeros(acc_sc.shape, F32)
    last = (qi * tq + tq - 1) // tk

    def scores(ki, buf):
        buf[...] = _dot(kaug_ref[0, pl.ds(pl.multiple_of(ki * tk, tk), tk), :], qaug_ref[0])

    def consume(ki, buf, causal):
        s = buf[...]
        if causal:
            kpos = ki * tk + lax.broadcasted_iota(I32, s.shape, 0)
            s = jnp.where(kpos <= t_ref[0], s, NEG)
        m_old = m_sc[...]
        m_new = jnp.maximum(m_old, jnp.max(s, axis=0, keepdims=True))
        m_sc[...] = m_new
        alpha = jnp.exp2(m_old - m_new)
        pb = jnp.exp2(s - m_new).astype(BF16)
        for h in range(N_KV_HEADS):
            ls = slice(h * hl, (h + 1) * hl)
            acc_sc[h] = acc_sc[h] * alpha[:, ls] + _dot(vt_ref[ki, h * VT_ROWS:(h + 1) * VT_ROWS, :], pb[:, ls])

    scores(0, sa_sc)

    def body(j, carry):
        scores(2 * j + 1, sb_sc)
        consume(2 * j, sa_sc, False)
        scores(2 * j + 2, sa_sc)
        consume(2 * j + 1, sb_sc, False)
        return carry

    lax.fori_loop(0, last // 2, body, 0)

    @pl.when(last % 2 == 1)
    def _():
        scores(last, sb_sc)
        consume(last - 1, sa_sc, False)
        consume(last, sb_sc, True)

    @pl.when(last % 2 == 0)
    def _():
        consume(last, sa_sc, True)

    for hd in range(N_HEADS):
        h, g = divmod(hd, GQA)
        gs = slice(g * tq, (g + 1) * tq)
        linv = 1.0 / jnp.maximum(acc_sc[h, HEAD_DIM:HEAD_DIM + 1, gs], TINY)
        o_ref[0, HEAD_DIM * hd:HEAD_DIM * (hd + 1), :] = acc_sc[h, 0:HEAD_DIM, gs] * linv


def _sel_prompt(qaug, kaug, vt, tl, *, tq, tk, bsz):
    nq, r, L = qaug.shape
    nqb = nq // bsz
    tlen = kaug.shape[0] // bsz
    return pl.pallas_call(
        functools.partial(_sel_prompt_kernel, tq=tq, tk=tk),
        grid=(bsz, nqb),
        in_specs=[pl.BlockSpec((1, r, L), lambda i, j: (i * nqb + j, 0, 0)),
                  pl.BlockSpec((1, tlen, r), lambda i, j: (i, 0, 0)),
                  pl.BlockSpec((tlen // tk, N_KV_HEADS * VT_ROWS, tk), lambda i, j: (i, 0, 0)),
                  pl.BlockSpec((1, 1, L), lambda i, j: (j, 0, 0))],
        out_specs=pl.BlockSpec((1, N_HEADS * HEAD_DIM, tq), lambda i, j: (i * nqb + j, 0, 0)),
        out_shape=jax.ShapeDtypeStruct((nq, N_HEADS * HEAD_DIM, tq), F32),
        scratch_shapes=[pltpu.VMEM((1, L), F32), pltpu.VMEM((N_KV_HEADS, VT_ROWS, L // N_KV_HEADS), F32),
                        pltpu.VMEM((tk, L), F32), pltpu.VMEM((tk, L), F32)],
        compiler_params=_cparams(("parallel", "parallel"), VMEM_LIMIT), name="sel_prompt",
    )(qaug, kaug.reshape(bsz, tlen, r), vt, tl)


def _sel_prep_kernel(k_ref, v_ref, oh_ref, kaug_ref, vt_ref):
    kaug_ref[:, 0:KV_WIDTH] = k_ref[...].astype(BF16)
    kaug_ref[:, KV_WIDTH:] = oh_ref[...]
    vt = v_ref[...].T.astype(BF16)
    ones = jnp.ones((VSUM_ROWS, vt.shape[1]), BF16)
    for h in range(N_KV_HEADS):
        vt_ref[0, h * VT_ROWS:h * VT_ROWS + HEAD_DIM, :] = vt[h * HEAD_DIM:(h + 1) * HEAD_DIM, :]
        vt_ref[0, h * VT_ROWS + HEAD_DIM:(h + 1) * VT_ROWS, :] = ones


def _sel_prep(k, v, oh, *, tk):
    n = k.shape[0]
    tiles_per_seq = oh.shape[0] // tk
    nsel = oh.shape[1]
    return pl.pallas_call(
        _sel_prep_kernel, grid=(n // tk,),
        in_specs=[pl.BlockSpec((tk, KV_WIDTH), lambda i: (i, 0)), pl.BlockSpec((tk, KV_WIDTH), lambda i: (i, 0)),
                  pl.BlockSpec((tk, nsel), lambda i: (i % tiles_per_seq, 0))],
        out_specs=(pl.BlockSpec((tk, KV_WIDTH + nsel), lambda i: (i, 0)),
                   pl.BlockSpec((1, N_KV_HEADS * VT_ROWS, tk), lambda i: (i, 0, 0))),
        out_shape=(jax.ShapeDtypeStruct((n, KV_WIDTH + nsel), BF16),
                   jax.ShapeDtypeStruct((n // tk, N_KV_HEADS * VT_ROWS, tk), BF16)),
        compiler_params=_cparams(("parallel",)), name="sel_prep",
    )(k, v, oh)


def _sel_sample_kernel(qaug_ref, k_ref, oh_ref, v_ref, kn_ref, ohn_ref, vn_ref, t_ref, o_ref,
                       m_sc, l_sc, acc_sc, *, tq, tk, past_len):
    ki = pl.program_id(1)
    q1 = qaug_ref[0, 0:KV_WIDTH, :]
    sb = qaug_ref[0, KV_WIDTH:, :]
    t = t_ref[0]

    @pl.when(ki == 0)
    def _():
        _flash_init(m_sc, l_sc, acc_sc)

    s = _sel_scores(q1, sb, k_ref[0], oh_ref[...], ki * tk, t, causal=False)
    _flash_update(s, v_ref[0].T.astype(BF16), m_sc, l_sc, acc_sc)

    @pl.when(ki == pl.num_programs(1) - 1)
    def _():
        sn = _sel_scores(q1, sb, kn_ref[0], ohn_ref[...], past_len, t)
        _flash_update(sn, vn_ref[0].T.astype(BF16), m_sc, l_sc, acc_sc)
        _flash_finish(o_ref, l_sc, acc_sc, tq)


def _sel_sample(qaug, k, oh, v, kn, ohn, vn, tl, *, tq, tk, past_len):
    b, r, L = qaug.shape
    nk = past_len // tk
    nsel = oh.shape[1]
    tn = kn.shape[1]
    return pl.pallas_call(
        functools.partial(_sel_sample_kernel, tq=tq, tk=tk, past_len=past_len),
        grid=(b, nk),
        in_specs=[pl.BlockSpec((1, r, L), lambda i, j: (i, 0, 0)),
                  pl.BlockSpec((1, tk, KV_WIDTH), lambda i, j: (i, j, 0)),
                  pl.BlockSpec((tk, nsel), lambda i, j: (j, 0)),
                  pl.BlockSpec((1, tk, KV_WIDTH), lambda i, j: (i, j, 0)),
                  pl.BlockSpec((1, tn, KV_WIDTH), lambda i, j: (i, 0, 0)),
                  pl.BlockSpec((tn, nsel), lambda i, j: (0, 0)),
                  pl.BlockSpec((1, tn, KV_WIDTH), lambda i, j: (i, 0, 0)),
                  pl.BlockSpec((1, 1, L), lambda i, j: (0, 0, 0))],
        out_specs=pl.BlockSpec((1, N_HEADS * HEAD_DIM, tq), lambda i, j: (i, 0, 0)),
        out_shape=jax.ShapeDtypeStruct((b, N_HEADS * HEAD_DIM, tq), F32),
        scratch_shapes=[pltpu.VMEM((1, L), F32), pltpu.VMEM((1, L), F32),
                        pltpu.VMEM((N_KV_HEADS, HEAD_DIM, L // N_KV_HEADS), F32)],
        compiler_params=_cparams(("parallel", "arbitrary"), VMEM_LIMIT), name="sel_sample",
    )(qaug, k, oh, v, kn, ohn, vn, tl)


def _window_kernel(q_ref, k_ref, v_ref, t_ref, o_ref, *, tq, band, kpos_base):
    qi = pl.program_id(1)
    L = N_HEADS * tq
    hl = L // N_KV_HEADS
    q1 = q_ref[0]
    t = t_ref[0]
    start = pl.multiple_of(jnp.maximum(qi * tq + tq - band, 0), LANE)
    kb = k_ref[0, pl.ds(start, band), :].astype(BF16)
    s = _dot(kb, q1)
    r_hi = t - (kpos_base + start)
    row = lax.broadcasted_iota(I32, s.shape, 0)
    s = jnp.where((row <= r_hi) & (row > r_hi - WINDOW), s, NEG)
    m = jnp.max(s, axis=0, keepdims=True)
    pb = jnp.exp2(s - m).astype(BF16)
    keep = m > NEG / 2
    vt = v_ref[0, pl.ds(start, band), :].T.astype(BF16)
    ones = jnp.ones((VSUM_ROWS, band), BF16)
    for h in range(N_KV_HEADS):
        vth = jnp.concatenate([vt[HEAD_DIM * h:HEAD_DIM * (h + 1), :], ones], axis=0)
        o_h = _dot(vth, pb[:, h * hl:(h + 1) * hl])
        for g in range(GQA):
            hd = h * GQA + g
            gs = slice(g * tq, (g + 1) * tq)
            linv = jnp.where(keep[:, hd * tq:(hd + 1) * tq], 1.0 / jnp.maximum(o_h[HEAD_DIM:HEAD_DIM + 1, gs], TINY), 0.0)
            o_ref[0, HEAD_DIM * hd:HEAD_DIM * (hd + 1), :] = o_h[0:HEAD_DIM, gs] * linv


def _window(qblk, k, v, tl, *, tq, band, kpos_base):
    nq, _, L = qblk.shape
    b, tlen, _ = k.shape
    nqb = nq // b
    shared_t = tl.shape[0] == 1
    return pl.pallas_call(
        functools.partial(_window_kernel, tq=tq, band=band, kpos_base=kpos_base),
        grid=(b, nqb),
        in_specs=[pl.BlockSpec((1, KV_WIDTH, L), lambda i, j: (i * nqb + j, 0, 0)),
                  pl.BlockSpec((1, tlen, KV_WIDTH), lambda i, j: (i, 0, 0)),
                  pl.BlockSpec((1, tlen, KV_WIDTH), lambda i, j: (i, 0, 0)),
                  pl.BlockSpec((1, 1, L), (lambda i, j: (0, 0, 0)) if shared_t else (lambda i, j: (j, 0, 0)))],
        out_specs=pl.BlockSpec((1, N_HEADS * HEAD_DIM, tq), lambda i, j: (i * nqb + j, 0, 0)),
        out_shape=jax.ShapeDtypeStruct((nq, N_HEADS * HEAD_DIM, tq), F32),
        compiler_params=_cparams(("parallel", "parallel"), VMEM_LIMIT), name="window",
    )(qblk, k, v, tl)


def _combine_kernel(x_ref, a_ref, oc_ref, os_ref, ow_ref, gt_ref, e_ref, woa_ref, wob_ref, o_ref, *, tm):
    hi, lo = _split_bf16(gt_ref[...])
    ge = _dot(e_ref[...], hi) + _dot(e_ref[...], lo)
    bw = N_HEADS * HEAD_DIM
    parts = []
    for j in range(tm // LANE):
        ls = slice(j * LANE, (j + 1) * LANE)
        parts.append(ge[0:bw, ls] * oc_ref[j] + ge[bw:2 * bw, ls] * os_ref[j] + ge[2 * bw:3 * bw, ls] * ow_ref[j])
    mixt = parts[0] if len(parts) == 1 else jnp.concatenate(parts, axis=1)
    mix = mixt.T.astype(BF16)
    o_ref[...] = x_ref[...] + _dot(a_ref[...], woa_ref[...]) + _dot(mix, wob_ref[...])


def _combine(x, a, oc, os_, ow, gt, e, woa, wob, *, tm):
    n, dm = x.shape
    bw = N_HEADS * HEAD_DIM
    full = lambda arr: pl.BlockSpec(arr.shape, lambda i: (0,) * arr.ndim)
    ospec = pl.BlockSpec((tm // LANE, bw, LANE), lambda i: (i, 0, 0))
    return pl.pallas_call(
        functools.partial(_combine_kernel, tm=tm),
        grid=(n // tm,),
        in_specs=[pl.BlockSpec((tm, dm), lambda i: (i, 0)), pl.BlockSpec((tm, a.shape[1]), lambda i: (i, 0)),
                  ospec, ospec, ospec, pl.BlockSpec((32, tm), lambda i: (0, i)), full(e), full(woa), full(wob)],
        out_specs=pl.BlockSpec((tm, dm), lambda i: (i, 0)),
        out_shape=jax.ShapeDtypeStruct((n, dm), F32),
        compiler_params=_cparams(("parallel",), VMEM_LIMIT), name="combine",
    )(x, a, oc, os_, ow, gt, e, woa, wob)


def _ffn_dense_kernel(x_ref, g_ref, wg_ref, wu_ref, wd_ref, o_ref):
    xf = x_ref[...]
    h = _rms(xf, g_ref[...]).astype(BF16)
    gate = _dot(h, wg_ref[...])
    up = _dot(h, wu_ref[...])
    act = (gate * _sigmoid(gate) * up).astype(BF16)
    o_ref[...] = xf + _dot(act, wd_ref[...])


def _ffn_dense(x, g, wg, wu, wd, *, tm):
    n, dm = x.shape
    full = lambda arr: pl.BlockSpec(arr.shape, lambda i: (0,) * arr.ndim)
    return pl.pallas_call(
        _ffn_dense_kernel, grid=(n // tm,),
        in_specs=[pl.BlockSpec((tm, dm), lambda i: (i, 0)), full(g), full(wg), full(wu), full(wd)],
        out_specs=pl.BlockSpec((tm, dm), lambda i: (i, 0)),
        out_shape=jax.ShapeDtypeStruct((n, dm), F32),
        compiler_params=_cparams(("parallel",), VMEM_LIMIT), name="ffn_dense",
    )(x, g, wg, wu, wd)


def _router_kernel(x_ref, g_ref, wrh_ref, wrl_ref, tri_ref, h_ref, meta_ref, cnt_ref, *, tm, n_exp):
    h = _rms(x_ref[...], g_ref[...])
    h_ref[...] = h
    hi, lo = _split_bf16(h)
    logits = _dot(hi, wrh_ref[...]) + _dot(lo, wrh_ref[...]) + _dot(hi, wrl_ref[...])
    lane = lax.broadcasted_iota(I32, (tm, LANE), 1)
    lanef = lane.astype(F32)
    logits = jnp.where(lane < n_exp, logits, NEG)
    m1 = jnp.max(logits, axis=1, keepdims=True)
    i1 = jnp.min(jnp.where(logits == m1, lanef, 1e9), axis=1, keepdims=True)
    rest = jnp.where(lanef == i1, -3e38, logits)
    m2 = jnp.max(rest, axis=1, keepdims=True)
    i2 = jnp.min(jnp.where(rest == m2, lanef, 1e9), axis=1, keepdims=True)
    e2 = jnp.exp(m2 - m1)
    w1 = 1.0 / (1.0 + e2)
    w2 = e2 / (1.0 + e2)
    hit1 = lanef == i1
    hit2 = lanef == i2
    msel = jnp.where(hit1 | hit2, 1.0, 0.0)
    ranks = _dot(tri_ref[...], msel.astype(BF16))
    r1 = jnp.sum(jnp.where(hit1, ranks, 0.0), axis=1, keepdims=True)
    r2 = jnp.sum(jnp.where(hit2, ranks, 0.0), axis=1, keepdims=True)
    meta = jnp.where(lane == 0, i1, 0.0)
    for k, val in enumerate((i2, w1, w2, r1, r2)):
        meta = jnp.where(lane == k + 1, val, meta)
    meta_ref[...] = meta
    cnt_ref[0] = jnp.sum(msel, axis=0, keepdims=True)


def _router(x, g, wrh, wrl, tri, *, tm, n_exp):
    n, dm = x.shape
    full = lambda arr: pl.BlockSpec(arr.shape, lambda i: (0,) * arr.ndim)
    return pl.pallas_call(
        functools.partial(_router_kernel, tm=tm, n_exp=n_exp), grid=(n // tm,),
        in_specs=[pl.BlockSpec((tm, dm), lambda i: (i, 0)), full(g), full(wrh), full(wrl), full(tri)],
        out_specs=(pl.BlockSpec((tm, dm), lambda i: (i, 0)), pl.BlockSpec((tm, LANE), lambda i: (i, 0)),
                   pl.BlockSpec((1, 1, LANE), lambda i: (i, 0, 0))),
        out_shape=(jax.ShapeDtypeStruct((n, dm), F32), jax.ShapeDtypeStruct((n, LANE), F32),
                   jax.ShapeDtypeStruct((n // tm, 1, LANE), F32)),
        compiler_params=_cparams(("parallel",), VMEM_LIMIT), name="router",
    )(x, g, wrh, wrl, tri)


def _row_copy(src, src_row, dst, dst_row, sem):
    return pltpu.make_async_copy(src.at[pl.ds(src_row, 1)], dst.at[pl.ds(dst_row, 1)], sem)


def _dispatch_kernel(dest_ref, h_ref, xg_in_ref, xg_ref, sem, *, tm):
    del xg_in_ref

    def issue(r, carry):
        for k in range(TOP_K):
            _row_copy(h_ref, r, xg_ref, dest_ref[TOP_K * r + k], sem).start()
        return carry

    lax.fori_loop(0, tm, issue, 0)

    def drain(r, carry):
        for k in range(TOP_K):
            _row_copy(h_ref, 0, xg_ref, 0, sem).wait()
        return carry

    lax.fori_loop(0, tm, drain, 0)


def _dispatch(dest, h, xg0, *, tm):
    n, dm = h.shape
    return pl.pallas_call(
        functools.partial(_dispatch_kernel, tm=tm), grid=(n // tm,),
        in_specs=[pl.BlockSpec((TOP_K * tm,), lambda i: (i,), memory_space=pltpu.SMEM),
                  pl.BlockSpec((tm, dm), lambda i: (i, 0)),
                  pl.BlockSpec(memory_space=pl.ANY)],
        out_specs=pl.BlockSpec(memory_space=pl.ANY),
        out_shape=jax.ShapeDtypeStruct(xg0.shape, F32),
        scratch_shapes=[pltpu.SemaphoreType.DMA],
        input_output_aliases={2: 0},
        compiler_params=_cparams(("arbitrary",), VMEM_LIMIT), name="moe_dispatch",
    )(dest, h, xg0)


def _expert_ffn_kernel(te_ref, nused_ref, x_ref, wg_ref, wu_ref, wd_ref, o_ref):
    s = pl.program_id(0)

    @pl.when(s < nused_ref[0])
    def _():
        h = x_ref[...].astype(BF16)
        gate = _dot(h, wg_ref[0])
        up = _dot(h, wu_ref[0])
        act = (gate * _sigmoid(gate) * up).astype(BF16)
        o_ref[...] = _dot(act, wd_ref[0])

    @pl.when(s >= nused_ref[0])
    def _():
        o_ref[...] = jnp.zeros(o_ref.shape, F32)


def _expert_ffn(te, nused, xg, wg, wu, wd, *, ts):
    s_rows, dm = xg.shape
    dff = wg.shape[2]
    grid_spec = pltpu.PrefetchScalarGridSpec(
        num_scalar_prefetch=2, grid=(s_rows // ts,),
        in_specs=[pl.BlockSpec((ts, dm), lambda i, te, nu: (i, 0)),
                  pl.BlockSpec((1, dm, dff), lambda i, te, nu: (te[i], 0, 0)),
                  pl.BlockSpec((1, dm, dff), lambda i, te, nu: (te[i], 0, 0)),
                  pl.BlockSpec((1, dff, dm), lambda i, te, nu: (te[i], 0, 0))],
        out_specs=pl.BlockSpec((ts, dm), lambda i, te, nu: (i, 0)))
    return pl.pallas_call(
        _expert_ffn_kernel, grid_spec=grid_spec,
        out_shape=jax.ShapeDtypeStruct((s_rows, dm), F32),
        compiler_params=_cparams(("arbitrary",), VMEM_LIMIT), name="expert_ffn",
    )(te, nused, xg, wg, wu, wd)


def _moe_combine_kernel(dest_ref, x_ref, meta_ref, y_ref, o_ref, buf, sem, *, tm):
    def issue(r, carry):
        for k in range(TOP_K):
            _row_copy(y_ref, dest_ref[TOP_K * r + k], buf.at[k], r, sem).start()
        return carry

    lax.fori_loop(0, tm, issue, 0)

    def drain(r, carry):
        for k in range(TOP_K):
            _row_copy(y_ref, 0, buf.at[k], 0, sem).wait()
        return carry

    lax.fori_loop(0, tm, drain, 0)
    meta = meta_ref[...]
    o_ref[...] = x_ref[...] + meta[:, 2:3] * buf[0] + meta[:, 3:4] * buf[1]


def _moe_combine(dest, x, meta, y, *, tm):
    n, dm = x.shape
    return pl.pallas_call(
        functools.partial(_moe_combine_kernel, tm=tm), grid=(n // tm,),
        in_specs=[pl.BlockSpec((TOP_K * tm,), lambda i: (i,), memory_space=pltpu.SMEM),
                  pl.BlockSpec((tm, dm), lambda i: (i, 0)),
                  pl.BlockSpec((tm, LANE), lambda i: (i, 0)),
                  pl.BlockSpec(memory_space=pl.ANY)],
        out_specs=pl.BlockSpec((tm, dm), lambda i: (i, 0)),
        out_shape=jax.ShapeDtypeStruct((n, dm), F32),
        scratch_shapes=[pltpu.VMEM((TOP_K, tm, dm), F32), pltpu.SemaphoreType.DMA],
        compiler_params=_cparams(("arbitrary",), VMEM_LIMIT), name="moe_combine",
    )(dest, x, meta, y)


def _moe(x, g, wr, wg, wu, wd, *, tm, ts):
    n, dm = x.shape
    n_exp = wr.shape[1]
    wr_pad = jnp.zeros((dm, LANE), F32).at[:, :n_exp].set(wr)
    wrh, wrl = _split_bf16(wr_pad)
    tri = jnp.tril(jnp.ones((tm, tm), F32), -1).astype(BF16)
    h, meta, cnt = _router(x, g, wrh, wrl, tri, tm=tm, n_exp=n_exp)
    cnt = cnt[:, 0, :n_exp].astype(I32)
    tile_off = jnp.cumsum(cnt, axis=0) - cnt
    tot = jnp.sum(cnt, axis=0)
    padded = ((tot + ts - 1) // ts) * ts
    gend = jnp.cumsum(padded)
    gstart = gend - padded
    ei = meta[:, 0:2].astype(I32)
    rk = meta[:, 4:6].astype(I32)
    base = jnp.broadcast_to((gstart[None, :] + tile_off)[:, None, :], (n // tm, tm, n_exp)).reshape(n, 1, n_exp)
    pick = ei[:, :, None] == jnp.arange(n_exp, dtype=I32)[None, None, :]
    dest = (jnp.sum(jnp.where(pick, base, 0), axis=-1) + rk).reshape(-1)
    n_slots = TOP_K * n + n_exp * ts
    n_slot_tiles = n_slots // ts
    nused = (gend[-1] // ts).astype(I32).reshape(1)
    tile_start = jnp.minimum(jnp.arange(n_slot_tiles, dtype=I32), jnp.maximum(nused[0] - 1, 0)) * ts
    te = jnp.minimum(jnp.sum((gend[None, :] <= tile_start[:, None]).astype(I32), axis=1), n_exp - 1)
    xg = _dispatch(dest, h, jnp.zeros((n_slots, dm), F32), tm=tm)
    y = _expert_ffn(te, nused, xg, wg, wu, wd, ts=ts)
    return _moe_combine(dest, x, meta, y, tm=tm)


def _norm_kernel(x_ref, g_ref, o_ref):
    o_ref[...] = _rms(x_ref[...], g_ref[...])


def _final_norm(x, g, *, tm):
    n, dm = x.shape
    return pl.pallas_call(
        _norm_kernel, grid=(n // tm,),
        in_specs=[pl.BlockSpec((tm, dm), lambda i: (i, 0)), pl.BlockSpec(g.shape, lambda i: (0, 0))],
        out_specs=pl.BlockSpec((tm, dm), lambda i: (i, 0)),
        out_shape=jax.ShapeDtypeStruct((n, dm), F32),
        compiler_params=_cparams(("parallel",)), name="final_norm",
    )(x, g)


PAGES_PER_STEP = 8


def _page_gather_kernel(pt_ref, *refs, n_chunked):
    del pt_ref
    g = PAGES_PER_STEP
    n_arr = (len(refs) - 1) // (g + 1)
    pages, outs, scr = refs[:n_arr * g], refs[n_arr * g:n_arr * (g + 1)], refs[-1]
    for c in range(n_arr):
        for i in range(g):
            p = pages[c * g + i][0, 0].T
            npos = p.shape[0]
            if c < n_chunked:
                scr[...] = p
                rows = npos // CMP_STRIDE
                for s in range(CMP_STRIDE):
                    outs[c][0, i * rows:(i + 1) * rows, s * KV_WIDTH:(s + 1) * KV_WIDTH] = scr[pl.ds(s, rows, stride=CMP_STRIDE), :]
            else:
                outs[c][0, i * npos:(i + 1) * npos, :] = p


def _page_gather(page_table, pools_t, layer, *, n_chunked):
    b, n_pages = page_table.shape
    page = pools_t[0].shape[3]
    g = PAGES_PER_STEP
    n_arr = len(pools_t)
    in_specs, operands = [], []
    for p in pools_t:
        for i in range(g):
            in_specs.append(pl.BlockSpec((1, 1, KV_WIDTH, page), lambda bi, j, pt, i=i: (layer, pt[bi, j * g + i], 0, 0)))
            operands.append(p)
    rows = page // CMP_STRIDE
    out_specs, out_shape = [], []
    for c in range(n_arr):
        if c < n_chunked:
            out_specs.append(pl.BlockSpec((1, g * rows, CMP_STRIDE * KV_WIDTH), lambda bi, j, pt: (bi, j, 0)))
            out_shape.append(jax.ShapeDtypeStruct((b, n_pages * rows, CMP_STRIDE * KV_WIDTH), F32))
        else:
            out_specs.append(pl.BlockSpec((1, g * page, KV_WIDTH), lambda bi, j, pt: (bi, j, 0)))
            out_shape.append(jax.ShapeDtypeStruct((b, n_pages * page, KV_WIDTH), F32))
    grid_spec = pltpu.PrefetchScalarGridSpec(
        num_scalar_prefetch=1, grid=(b, n_pages // g), in_specs=in_specs, out_specs=out_specs,
        scratch_shapes=[pltpu.VMEM((page, KV_WIDTH), F32)])
    return pl.pallas_call(
        functools.partial(_page_gather_kernel, n_chunked=n_chunked),
        grid_spec=grid_spec, out_shape=out_shape,
        compiler_params=_cparams(("parallel", "parallel"), VMEM_LIMIT), name="page_gather",
    )(page_table, *operands)


def _rope_tables(pos):
    half = HEAD_DIM // 2
    inv = ROPE_THETA ** (-jnp.arange(half, dtype=F32) / half)
    ang = pos.astype(F32)[:, None] * inv[None, :]
    cos, sin = jnp.cos(ang), jnp.sin(ang)
    cos_std = jnp.tile(cos, (1, 4))
    sin_std = jnp.tile(jnp.concatenate([-sin, sin], axis=1), (1, 2))
    return cos_std, sin_std, cos.T, sin.T


def _cmp_to_sel_t(n_cmp_pad, n_sel_pad):
    i0 = np.arange(n_cmp_pad, dtype=np.int64)[None, :] * CMP_STRIDE
    j0 = np.arange(n_sel_pad, dtype=np.int64)[:, None] * SEL_LEN
    ov = np.clip(np.minimum(i0 + CMP_LEN, j0 + SEL_LEN) - np.maximum(i0, j0), 0, CMP_LEN)
    return jnp.asarray(ov.astype(np.float32) / CMP_LEN, dtype=BF16)


def _block_onehot(n_keys, n_sel_pad):
    blk = np.arange(n_keys)[:, None] // SEL_LEN
    return jnp.asarray((blk == np.arange(n_sel_pad)[None, :]).astype(np.float32), dtype=BF16)


def _lane_positions(pos_tiles, tq):
    return jnp.tile(pos_tiles.astype(I32), (1, N_HEADS))[:, None, :]


def _compress_weights(w1, w2, pe):
    nr = CMP_LEN // CMP_STRIDE
    w1r = w1.reshape(nr, CMP_STRIDE, HEAD_DIM, CMP_HIDDEN)
    eye = jnp.eye(N_KV_HEADS, dtype=F32)
    wbig = jnp.einsum("rsde,gh->sgdrhe", w1r, eye)
    wbig = wbig.reshape(CMP_STRIDE * KV_WIDTH, nr * N_KV_HEADS * CMP_HIDDEN).astype(BF16)
    w2big = jnp.einsum("ed,gh->gehd", w2, eye).reshape(N_KV_HEADS * CMP_HIDDEN, KV_WIDTH).astype(BF16)
    pe8 = jnp.tile(pe.reshape(1, CMP_LEN * HEAD_DIM), (8, 1)).astype(BF16)
    return wbig, pe8, w1.astype(BF16), w2big


def _layer_weights(l, g_mix, w_in, g_sgu, w_sgu, b_sgu, w_cmpk1, w_cmpk2, pe_cmpk, w_cmpv1, w_cmpv2, pe_cmpv, w_o, dec_t):
    aw = A_GROUPS * CHUNK
    bw = N_HEADS * HEAD_DIM
    w = w_in[l]
    wstd = jnp.concatenate([w[:, :2 * aw], w[:, 2 * aw + bw:2 * aw + bw + 6 * KV_WIDTH]], axis=1).astype(BF16)
    n_gate = 3 * N_HEADS
    wt = jnp.concatenate([w[:, 2 * aw:2 * aw + bw], w[:, -n_gate:], jnp.zeros((w.shape[0], 32 - n_gate), F32)], axis=1).T.astype(BF16)
    causal = jnp.tril(jnp.ones((CHUNK, CHUNK), bool))
    wsgu_p = jnp.where(causal[None], w_sgu[l], 0.0).astype(BF16)
    bsgu_p = jnp.broadcast_to(b_sgu[l][:, :, None], (A_GROUPS, CHUNK, CHUNK)).astype(F32)
    reps = CHUNK // dec_t
    wsmall = jnp.where(causal[None, :dec_t, :dec_t], w_sgu[l][:, :dec_t, :dec_t], 0.0)
    wsgu_s = jnp.einsum("ab,gts->gatbs", jnp.eye(reps, dtype=F32), wsmall).reshape(A_GROUPS, CHUNK, CHUNK).astype(BF16)
    bsgu_s = jnp.broadcast_to(jnp.tile(b_sgu[l][:, :dec_t], (1, reps))[:, :, None], (A_GROUPS, CHUNK, CHUNK)).astype(F32)
    ck_w = _compress_weights(w_cmpk1[l], w_cmpk2[l], pe_cmpk[l])
    cv_w = _compress_weights(w_cmpv1[l], w_cmpv2[l], pe_cmpv[l])
    cv_w = cv_w[:3] + (cv_w[3].T,)
    e = np.zeros((3 * bw, 32), np.float32)
    for br in range(3):
        for hd in range(N_HEADS):
            e[br * bw + hd * HEAD_DIM:br * bw + (hd + 1) * HEAD_DIM, br * N_HEADS + hd] = 1.0
    return dict(gmix=g_mix[l][None, :], wstd=wstd, wt=wt, gsgu=g_sgu[l][None, :],
                wsgu_p=wsgu_p, bsgu_p=bsgu_p, wsgu_s=wsgu_s, bsgu_s=bsgu_s, ck_w=ck_w, cv_w=cv_w,
                e=jnp.asarray(e, dtype=BF16), woa=w_o[l][:aw].astype(BF16), wob=w_o[l][aw:].astype(BF16))


def _channel_mixer(l, x, g_ffn, dense_w, w_router, moe_w, *, tm_dense, tm_moe):
    g = g_ffn[l][None, :]
    i = l // 2
    if l % 2 == 0:
        return _ffn_dense(x, g, *(w[i] for w in dense_w), tm=tm_dense)
    return _moe(x, g, w_router[i], *(w[i] for w in moe_w), tm=tm_moe, ts=256)


def kernel(x_prompt, x_sample, cache_cmp_k, cache_cmp_v, cache_sel_k, cache_sel_v, cache_win_k, cache_win_v, page_table,
           g_mix, w_in, g_sgu, w_sgu, b_sgu, w_cmpk1, w_cmpk2, pe_cmpk, w_cmpv1, w_cmpv2, pe_cmpv, w_o, g_ffn,
           w_ff_gate, w_ff_up, w_ff_down, w_router, w_moe_gate, w_moe_up, w_moe_down, g_final):
    bsz, seq, dm = x_prompt.shape
    dec_b, dec_t, _ = x_sample.shape
    depth = g_mix.shape[0]
    page = cache_cmp_k.shape[2]
    past_len = page_table.shape[1] * page
    wbuf = cache_win_k.shape[2]
    n_p, n_s = bsz * seq, dec_b * dec_t
    assert n_s == CHUNK and seq % 512 == 0 and wbuf == WINDOW and past_len % 2048 == 0
    tq_p, tq_s = LANE, LANE // N_HEADS
    nq_b = seq // tq_p
    band = WINDOW + tq_p

    pos_p = jnp.tile(jnp.arange(seq, dtype=I32), bsz)
    pos_s = jnp.tile(past_len + jnp.arange(dec_t, dtype=I32), dec_b)
    rope_p = _rope_tables(pos_p)
    rope_s = _rope_tables(pos_s)
    tl_p = _lane_positions(jnp.arange(seq, dtype=I32).reshape(nq_b, tq_p), tq_p)
    tok_s = past_len + jnp.minimum(jnp.arange(tq_s, dtype=I32), dec_t - 1)
    tl_s = _lane_positions(tok_s[None, :], tq_s)
    nch_p = seq // CMP_STRIDE
    nsel_p = seq // SEL_LEN
    nch_s = past_len // CMP_STRIDE
    nsel_s = -(-(past_len // SEL_LEN + 1) // LANE) * LANE
    mt_p = _cmp_to_sel_t(nch_p, nsel_p)
    mt_s = _cmp_to_sel_t(nch_s, nsel_s)
    oh_p = _block_onehot(seq, nsel_p)
    oh_s = _block_onehot(past_len + SEL_LEN, nsel_s)
    head_of_row = jnp.asarray((np.arange(N_HEADS)[None, :] // GQA == np.arange(N_KV_HEADS)[:, None]).astype(np.float32))

    pools_t = [jnp.transpose(c, (0, 1, 3, 4, 2)).reshape(depth, c.shape[1], KV_WIDTH, page)
               for c in (cache_cmp_k, cache_cmp_v, cache_sel_k, cache_sel_v)]
    flat = lambda a: a.reshape(a.shape[0], a.shape[1], KV_WIDTH)

    dense_w = tuple(w.astype(BF16) for w in (w_ff_gate, w_ff_up, w_ff_down))
    moe_w = tuple(w.astype(BF16) for w in (w_moe_gate, w_moe_up, w_moe_down))
    xp = x_prompt.reshape(n_p, dm)
    xs = x_sample.reshape(n_s, dm)
    outs = [[] for _ in range(13)]
    for l in range(depth):
        lw = _layer_weights(l, g_mix, w_in, g_sgu, w_sgu, b_sgu, w_cmpk1, w_cmpk2, pe_cmpk, w_cmpv1, w_cmpv2, pe_cmpv, w_o, dec_t)
        a, _, kc, vc, ks, vs, kw, vw, qblk, gt = _inproj(
            xp, lw["gmix"], lw["wstd"], lw["wt"], lw["gsgu"], lw["wsgu_p"], lw["bsgu_p"], *rope_p, tm=512, emit_qblk=True)
        ck = _compress(kc.reshape(bsz, nch_p, CMP_STRIDE * KV_WIDTH), *lw["ck_w"], transpose_out=False)
        cvt = _compress(vc.reshape(bsz, nch_p, CMP_STRIDE * KV_WIDTH), *lw["cv_w"], transpose_out=True)
        oc, qaug = _cmp_select(qblk, ck, cvt, mt_p, tl_p, tq=tq_p, nq_per_b=nq_b)
        kaug, vst = _sel_prep(ks, vs, oh_p, tk=SEL_TK)
        osel = _sel_prompt(qaug, kaug, vst, tl_p, tq=tq_p, tk=SEL_TK, bsz=bsz)
        ow = _window(qblk, kw.reshape(bsz, seq, KV_WIDTH), vw.reshape(bsz, seq, KV_WIDTH), tl_p, tq=tq_p, band=band, kpos_base=0)
        xp = _combine(xp, a, oc, osel, ow, gt, lw["e"], lw["woa"], lw["wob"], tm=512)
        kv4 = lambda t: t.reshape(bsz, seq, N_KV_HEADS, HEAD_DIM)
        for idx, t in enumerate((kc, vc, ks, vs)):
            outs[idx].append(kv4(t))
        nwin_p = min(WINDOW, seq)
        outs[4].append(kv4(kw)[:, seq - nwin_p:])
        outs[5].append(kv4(vw)[:, seq - nwin_p:])

        a, v_s, kc, vc, ks, vs, kw, vw, qt, gt = _inproj(
            xs, lw["gmix"], lw["wstd"], lw["wt"], lw["gsgu"], lw["wsgu_s"], lw["bsgu_s"], *rope_s, tm=CHUNK, emit_qblk=False)
        q4 = qt.reshape(N_HEADS, HEAD_DIM, dec_b, dec_t)
        q4 = jnp.pad(q4, ((0, 0), (0, 0), (0, 0), (0, tq_s - dec_t)))
        qb = jnp.einsum("hdbt,gh->bgdht", q4.astype(F32), head_of_row).astype(BF16)
        qblk_s = qb.reshape(dec_b, KV_WIDTH, N_HEADS * tq_s)
        past_ck, past_cv, past_sk, past_sv = _page_gather(page_table, pools_t, l, n_chunked=2)
        ck = _compress(past_ck, *lw["ck_w"], transpose_out=False)
        cvt = _compress(past_cv, *lw["cv_w"], transpose_out=True)
        oc, qaug = _cmp_select(qblk_s, ck, cvt, mt_s, tl_s, tq=tq_s, nq_per_b=1)
        pad_new = lambda t: jnp.pad(t.reshape(dec_b, dec_t, KV_WIDTH), ((0, 0), (0, SEL_LEN - dec_t), (0, 0)))
        osel = _sel_sample(qaug, past_sk, oh_s, past_sv, pad_new(ks), oh_s[past_len:], pad_new(vs), tl_s,
                           tq=tq_s, tk=2048, past_len=past_len)
        kw_all = jnp.concatenate([flat(cache_win_k[l]), kw.reshape(dec_b, dec_t, KV_WIDTH)], axis=1)
        vw_all = jnp.concatenate([flat(cache_win_v[l]), vw.reshape(dec_b, dec_t, KV_WIDTH)], axis=1)
        wpad = ((0, 0), (0, band - wbuf - dec_t), (0, 0))
        ow = _window(qblk_s, jnp.pad(kw_all, wpad), jnp.pad(vw_all, wpad), tl_s, tq=tq_s, band=band, kpos_base=past_len - wbuf)
        untile = lambda o: o[:, :, :dec_t].transpose(1, 0, 2).reshape(1, N_HEADS * HEAD_DIM, n_s)
        xs = _combine(xs, a, untile(oc), untile(osel), untile(ow), gt, lw["e"], lw["woa"], lw["wob"], tm=CHUNK)
        kv4s = lambda t: t.reshape(dec_b, dec_t, N_KV_HEADS, HEAD_DIM)
        for idx, t in enumerate((kc, vc, ks, vs)):
            outs[6 + idx].append(kv4s(t))
        nwin_s = min(WINDOW, wbuf + dec_t)
        outs[10].append(kw_all[:, wbuf + dec_t - nwin_s:].reshape(dec_b, nwin_s, N_KV_HEADS, HEAD_DIM))
        outs[11].append(vw_all[:, wbuf + dec_t - nwin_s:].reshape(dec_b, nwin_s, N_KV_HEADS, HEAD_DIM))
        outs[12].append(v_s.reshape(dec_b, dec_t, A_GROUPS * CHUNK))

        xp = _channel_mixer(l, xp, g_ffn, dense_w, w_router, moe_w, tm_dense=256, tm_moe=512)
        xs = _channel_mixer(l, xs, g_ffn, dense_w, w_router, moe_w, tm_dense=CHUNK, tm_moe=CHUNK)

    gf = g_final[None, :]
    y_prompt = _final_norm(xp, gf, tm=512).reshape(bsz, seq, dm)
    y_sample = _final_norm(xs, gf, tm=CHUNK).reshape(dec_b, dec_t, dm)
    return (y_prompt, y_sample) + tuple(jnp.stack(o, axis=0) for o in outs)
```

```python
import functools
import math

import numpy as np
import jax
import jax.numpy as jnp
from jax import lax
from jax.experimental import pallas as pl
from jax.experimental.pallas import tpu as pltpu

F32 = jnp.float32
BF16 = jnp.bfloat16
I32 = jnp.int32

A_GROUPS = 4
N_HEADS = 8
N_KV_HEADS = 2
HEAD_DIM = 64
CMP_LEN = 32
CMP_STRIDE = 16
CMP_HIDDEN = 128
SEL_LEN = 64
N_SEL = 16
WINDOW = 512
CHUNK = 128
ROPE_THETA = 10000.0
TOP_K = 2
EPS = 1e-6
NEG = -1e30
BIG = 1e30
TINY = 1e-30
PICKED = -3e38
N_FORCED = 3
SCALE = HEAD_DIM ** -0.5
QSCALE = SCALE * math.log2(math.e)

LANE = 128
KV_WIDTH = N_KV_HEADS * HEAD_DIM
GQA = N_HEADS // N_KV_HEADS
VMEM_LIMIT = 56 * 1024 * 1024


def _cparams(sem, vmem=None):
    return pltpu.CompilerParams(dimension_semantics=sem, vmem_limit_bytes=vmem)


def _rms(xf, g):
    return xf * lax.rsqrt(jnp.mean(xf * xf, axis=-1, keepdims=True) + EPS) * g


def _gelu(x):
    c = math.sqrt(2.0 / math.pi)
    return x * (0.5 * (1.0 + jnp.tanh(c * (x + 0.044715 * (x * x * x)))))


def _sigmoid(x):
    return 1.0 / (1.0 + jnp.exp(-x))


def _dot(a, b):
    return jnp.dot(a, b, preferred_element_type=F32)


def _dot_nt(a, b):
    return lax.dot_general(a, b, (((1,), (1,)), ((), ())), preferred_element_type=F32)


def _split_bf16(x):
    hi = x.astype(BF16)
    lo = (x - hi.astype(F32)).astype(BF16)
    return hi, lo


def _inproj_kernel(x_ref, gmix_ref, wstd_ref, wt_ref, gsgu_ref, wsgu_ref, bsgu_ref,
                   cos_ref, sin_ref, cost_ref, sint_ref,
                   a_ref, v_ref, kc_ref, vc_ref, ks_ref, vs_ref, kw_ref, vw_ref, q_ref, gt_ref,
                   *, tm, emit_qblk):
    xf = x_ref[...]
    h = _rms(xf, gmix_ref[...]).astype(BF16)
    z = _dot(h, wstd_ref[...])
    zt = _dot_nt(wt_ref[...], h)
    aw = A_GROUPS * CHUNK
    u = _gelu(z[:, 0:aw])
    vv = _gelu(z[:, aw:2 * aw])
    mu = jnp.mean(vv, axis=-1, keepdims=True)
    d = vv - mu
    var = jnp.mean(d * d, axis=-1, keepdims=True)
    v = d * lax.rsqrt(var + EPS) * gsgu_ref[...]
    v_ref[...] = v
    vb = v.astype(BF16)
    nc = tm // CHUNK
    for g in range(A_GROUPS):
        gs = slice(g * CHUNK, (g + 1) * CHUNK)
        parts = [vb[c * CHUNK:(c + 1) * CHUNK, gs] for c in range(nc)]
        xg = parts[0] if nc == 1 else jnp.concatenate(parts, axis=1)
        yg = _dot(wsgu_ref[g], xg)
        for c in range(nc):
            cs = slice(c * CHUNK, (c + 1) * CHUNK)
            mixed = yg[:, cs] + bsgu_ref[g]
            a_ref[cs, gs] = (u[cs, gs] * mixed).astype(BF16)

    cosr = cos_ref[...]
    sinr = sin_ref[...]
    lane = lax.broadcasted_iota(I32, (tm, KV_WIDTH), 1)
    first = (lane % HEAD_DIM) < (HEAD_DIM // 2)

    def rope(x):
        rot = jnp.where(first, pltpu.roll(x, KV_WIDTH - HEAD_DIM // 2, 1), pltpu.roll(x, HEAD_DIM // 2, 1))
        return x * cosr + rot * sinr

    o = 2 * aw
    kc_ref[...] = rope(z[:, o:o + 128])
    vc_ref[...] = z[:, o + 128:o + 256]
    ks_ref[...] = rope(z[:, o + 256:o + 384])
    vs_ref[...] = z[:, o + 384:o + 512]
    kw_ref[...] = rope(z[:, o + 512:o + 640])
    vw_ref[...] = z[:, o + 640:o + 768]

    ct = cost_ref[...]
    st = sint_ref[...]
    half = HEAD_DIM // 2
    for hd in range(N_HEADS):
        x1 = zt[HEAD_DIM * hd:HEAD_DIM * hd + half]
        x2 = zt[HEAD_DIM * hd + half:HEAD_DIM * (hd + 1)]
        qh = jnp.concatenate([(x1 * ct - x2 * st) * QSCALE, (x2 * ct + x1 * st) * QSCALE], axis=0).astype(BF16)
        if emit_qblk:
            kvh = hd // GQA
            zero = jnp.zeros((HEAD_DIM, LANE), BF16)
            for j in range(tm // LANE):
                ls = slice(hd * LANE, (hd + 1) * LANE)
                q_ref[j, HEAD_DIM * kvh:HEAD_DIM * (kvh + 1), ls] = qh[:, j * LANE:(j + 1) * LANE]
                q_ref[j, HEAD_DIM * (1 - kvh):HEAD_DIM * (2 - kvh), ls] = zero
        else:
            q_ref[HEAD_DIM * hd:HEAD_DIM * (hd + 1), :] = qh
    nq = N_HEADS * HEAD_DIM
    gt_ref[...] = _sigmoid(zt[nq:nq + 32])


def _inproj(x, gmix, wstd, wt, gsgu, wsgu, bsgu, cos, sin, cost, sint, *, tm, emit_qblk):
    n, dm = x.shape
    nt = n // tm
    row = lambda w: pl.BlockSpec((tm, w), lambda i: (i, 0))
    full = lambda a: pl.BlockSpec(a.shape, lambda i: (0,) * a.ndim)
    if emit_qblk:
        q_shape = jax.ShapeDtypeStruct((n // LANE, KV_WIDTH, N_HEADS * LANE), BF16)
        q_spec = pl.BlockSpec((tm // LANE, KV_WIDTH, N_HEADS * LANE), lambda i: (i, 0, 0))
    else:
        q_shape = jax.ShapeDtypeStruct((N_HEADS * HEAD_DIM, n), BF16)
        q_spec = pl.BlockSpec((N_HEADS * HEAD_DIM, tm), lambda i: (0, i))
    kv = jax.ShapeDtypeStruct((n, KV_WIDTH), F32)
    out_shape = (jax.ShapeDtypeStruct((n, 512), BF16), jax.ShapeDtypeStruct((n, 512), F32),
                 kv, kv, kv, kv, kv, kv, q_shape, jax.ShapeDtypeStruct((32, n), F32))
    out_specs = (row(512), row(512), row(128), row(128), row(128), row(128), row(128), row(128), q_spec,
                 pl.BlockSpec((32, tm), lambda i: (0, i)))
    in_specs = [row(dm), full(gmix), full(wstd), full(wt), full(gsgu), full(wsgu), full(bsgu),
                row(128), row(128), pl.BlockSpec((32, tm), lambda i: (0, i)), pl.BlockSpec((32, tm), lambda i: (0, i))]
    return pl.pallas_call(
        functools.partial(_inproj_kernel, tm=tm, emit_qblk=emit_qblk),
        grid=(nt,), in_specs=in_specs, out_specs=out_specs, out_shape=out_shape,
        compiler_params=_cparams(("parallel",), VMEM_LIMIT), name="inproj",
    )(x, gmix, wstd, wt, gsgu, wsgu, bsgu, cos, sin, cost, sint)


def _compress_kernel(c_ref, wbig_ref, pe_ref, w1_ref, w2_ref, o_ref, *, transpose_out):
    c = c_ref[0].astype(BF16)
    ab = _dot(c, wbig_ref[...])
    nch = ab.shape[0]
    hw = N_KV_HEADS * CMP_HIDDEN
    peb = _dot(pe_ref[...], w1_ref[...])
    bias = jnp.concatenate([peb[0:1]] * N_KV_HEADS, axis=1)
    hh = ab[:, :hw] + pltpu.roll(ab[:, hw:], nch - 1, 0) + bias
    g = _gelu(hh).astype(BF16)
    if transpose_out:
        o_ref[0] = _dot_nt(w2_ref[...], g).astype(BF16)
    else:
        o_ref[0] = _dot(g, w2_ref[...]).astype(BF16)


def _compress(c, wbig, pe8, w1, w2, *, transpose_out):
    b, nch, cw = c.shape
    if transpose_out:
        out_shape = jax.ShapeDtypeStruct((b, KV_WIDTH, nch), BF16)
        out_spec = pl.BlockSpec((1, KV_WIDTH, nch), lambda i: (i, 0, 0))
    else:
        out_shape = jax.ShapeDtypeStruct((b, nch, KV_WIDTH), BF16)
        out_spec = pl.BlockSpec((1, nch, KV_WIDTH), lambda i: (i, 0, 0))
    full = lambda a: pl.BlockSpec(a.shape, lambda i: (0,) * a.ndim)
    return pl.pallas_call(
        functools.partial(_compress_kernel, transpose_out=transpose_out),
        grid=(b,), in_specs=[pl.BlockSpec((1, nch, cw), lambda i: (i, 0, 0)), full(wbig), full(pe8), full(w1), full(w2)],
        out_specs=out_spec, out_shape=out_shape,
        compiler_params=_cparams(("parallel",), VMEM_LIMIT), name="compress",
    )(c, wbig, pe8, w1, w2)


def _cmp_select_kernel(q_ref, ck_ref, cvt_ref, mt_ref, t_ref, oc_ref, qaug_ref, sbt_ref, *, tq):
    L = N_HEADS * tq
    hl = L // N_KV_HEADS
    q1 = q_ref[0]
    t = t_ref[0]
    s = _dot(ck_ref[0], q1)
    nch = s.shape[0]
    n_last = (t - (CMP_LEN - 1)) // CMP_STRIDE
    s = jnp.where(lax.broadcasted_iota(I32, (nch, L), 0) <= n_last, s, NEG)
    m = jnp.max(s, axis=0, keepdims=True)
    p = jnp.exp2(s - m)
    inv = jnp.where(m > NEG / 2, 1.0 / jnp.maximum(jnp.sum(p, axis=0, keepdims=True), TINY), 0.0)
    p = p * inv
    pb = p.astype(BF16)
    mt = mt_ref[...]
    nsel = mt.shape[0]
    blk = lax.broadcasted_iota(I32, (nsel, tq), 0).astype(F32)
    cur = (t[:, 0:tq] // SEL_LEN).astype(F32)
    biases = []
    for h in range(N_KV_HEADS):
        o_h = _dot(cvt_ref[0, HEAD_DIM * h:HEAD_DIM * (h + 1), :], pb[:, h * hl:(h + 1) * hl])
        psum = p[:, h * hl:h * hl + tq]
        for g in range(GQA):
            hd = h * GQA + g
            oc_ref[0, HEAD_DIM * hd:HEAD_DIM * (hd + 1), :] = o_h[:, g * tq:(g + 1) * tq]
            if g > 0:
                psum = psum + p[:, hd * tq:(hd + 1) * tq]
        hi, lo = _split_bf16(psum)
        imp = _dot(mt, hi) + _dot(mt, lo)
        forced = (blk == 0.0) | (blk == cur) | (blk == cur - 1.0)
        past = blk <= cur
        sc = jnp.where(past & ~forced, imp, NEG)
        for _ in range(N_SEL - N_FORCED):
            mx = jnp.max(sc, axis=0, keepdims=True)
            idx = jnp.min(jnp.where(sc == mx, blk, 1e9), axis=0, keepdims=True)
            sc = jnp.where(blk == idx, PICKED, sc)
        sel = past & (forced | (sc < PICKED / 2) | (cur < float(N_SEL)))
        bias_h = jnp.where(sel, 0.0, NEG).astype(BF16)
        biases.extend([bias_h] * GQA)
    bias = jnp.concatenate(biases, axis=1)
    qaug_ref[0, 0:KV_WIDTH, :] = q1
    qaug_ref[0, KV_WIDTH:, :] = bias
    if sbt_ref is not None:
        assert L == LANE
        pad = jnp.zeros((LANE - BLOCKS_PER_STEP, L), F32)
        for js in range(sbt_ref.shape[1]):
            grp = bias[js * BLOCKS_PER_STEP:(js + 1) * BLOCKS_PER_STEP, :].astype(F32)
            sbt_ref[0, js] = jnp.concatenate([grp, pad], axis=0).T.astype(BF16)


def _cmp_select(qblk, ck, cvt, mt, tl, *, tq, nq_per_b, n_bias_steps=0):
    nq, _, L = qblk.shape
    b, nch, _ = ck.shape
    nsel = mt.shape[0]
    shared_t = tl.shape[0] == 1
    out_specs = [pl.BlockSpec((1, N_HEADS * HEAD_DIM, tq), lambda i, j: (i * nq_per_b + j, 0, 0)),
                 pl.BlockSpec((1, KV_WIDTH + nsel, L), lambda i, j: (i * nq_per_b + j, 0, 0))]
    out_shape = [jax.ShapeDtypeStruct((nq, N_HEADS * HEAD_DIM, tq), F32),
                 jax.ShapeDtypeStruct((nq, KV_WIDTH + nsel, L), BF16)]
    if n_bias_steps:
        out_specs.append(pl.BlockSpec((1, n_bias_steps, L, LANE), lambda i, j: (i * nq_per_b + j, 0, 0, 0)))
        out_shape.append(jax.ShapeDtypeStruct((nq, n_bias_steps, L, LANE), BF16))
        body = functools.partial(_cmp_select_kernel, tq=tq)
    else:
        body = lambda *refs: _cmp_select_kernel(*refs, None, tq=tq)
    return pl.pallas_call(
        body,
        grid=(b, nq_per_b),
        in_specs=[pl.BlockSpec((1, KV_WIDTH, L), lambda i, j: (i * nq_per_b + j, 0, 0)),
                  pl.BlockSpec((1, nch, KV_WIDTH), lambda i, j: (i, 0, 0)),
                  pl.BlockSpec((1, KV_WIDTH, nch), lambda i, j: (i, 0, 0)),
                  pl.BlockSpec(mt.shape, lambda i, j: (0, 0)),
                  pl.BlockSpec((1, 1, L), (lambda i, j: (0, 0, 0)) if shared_t else (lambda i, j: (j, 0, 0)))],
        out_specs=out_specs, out_shape=out_shape,
        compiler_params=_cparams(("parallel", "parallel"), VMEM_LIMIT), name="cmp_select",
    )(qblk, ck, cvt, mt, tl)


VSUM_ROWS = 16
VT_ROWS = HEAD_DIM + VSUM_ROWS
SEL_TK = 512


def _sel_prompt_kernel(qaug_ref, kaug_ref, vt_ref, t_ref, o_ref, m_sc, acc_sc, sa_sc, sb_sc, *, tq, tk):
    qi = pl.program_id(1)
    L = N_HEADS * tq
    hl = L // N_KV_HEADS
    m_sc[...] = jnp.full(m_sc.shape, NEG, F32)
    acc_sc[...] = jnp.zeros(acc_sc.shape, F32)
    last = (qi * tq + tq - 1) // tk

    def scores(ki, buf):
        buf[...] = _dot(kaug_ref[0, pl.ds(pl.multiple_of(ki * tk, tk), tk), :], qaug_ref[0])

    def consume(ki, buf, causal):
        s = buf[...]
        if causal:
            kpos = ki * tk + lax.broadcasted_iota(I32, s.shape, 0)
            s = jnp.where(kpos <= t_ref[0], s, NEG)
        m_old = m_sc[...]
        m_new = jnp.maximum(m_old, jnp.max(s, axis=0, keepdims=True))
        m_sc[...] = m_new
        alpha = jnp.exp2(m_old - m_new)
        pb = jnp.exp2(s - m_new).astype(BF16)
        for h in range(N_KV_HEADS):
            ls = slice(h * hl, (h + 1) * hl)
            acc_sc[h] = acc_sc[h] * alpha[:, ls] + _dot(vt_ref[ki, h * VT_ROWS:(h + 1) * VT_ROWS, :], pb[:, ls])

    scores(0, sa_sc)

    def body(j, carry):
        scores(2 * j + 1, sb_sc)
        consume(2 * j, sa_sc, False)
        scores(2 * j + 2, sa_sc)
        consume(2 * j + 1, sb_sc, False)
        return carry

    lax.fori_loop(0, last // 2, body, 0)

    @pl.when(last % 2 == 1)
    def _():
        scores(last, sb_sc)
        consume(last - 1, sa_sc, False)
        consume(last, sb_sc, True)

    @pl.when(last % 2 == 0)
    def _():
        consume(last, sa_sc, True)

    for hd in range(N_HEADS):
        h, g = divmod(hd, GQA)
        gs = slice(g * tq, (g + 1) * tq)
        linv = 1.0 / jnp.maximum(acc_sc[h, HEAD_DIM:HEAD_DIM + 1, gs], TINY)
        o_ref[0, HEAD_DIM * hd:HEAD_DIM * (hd + 1), :] = acc_sc[h, 0:HEAD_DIM, gs] * linv


def _sel_prompt(qaug, kaug, vt, tl, *, tq, tk, bsz):
    nq, r, L = qaug.shape
    nqb = nq // bsz
    tlen = kaug.shape[0] // bsz
    return pl.pallas_call(
        functools.partial(_sel_prompt_kernel, tq=tq, tk=tk),
        grid=(bsz, nqb),
        in_specs=[pl.BlockSpec((1, r, L), lambda i, j: (i * nqb + j, 0, 0)),
                  pl.BlockSpec((1, tlen, r), lambda i, j: (i, 0, 0)),
                  pl.BlockSpec((tlen // tk, N_KV_HEADS * VT_ROWS, tk), lambda i, j: (i, 0, 0)),
                  pl.BlockSpec((1, 1, L), lambda i, j: (j, 0, 0))],
        out_specs=pl.BlockSpec((1, N_HEADS * HEAD_DIM, tq), lambda i, j: (i * nqb + j, 0, 0)),
        out_shape=jax.ShapeDtypeStruct((nq, N_HEADS * HEAD_DIM, tq), F32),
        scratch_shapes=[pltpu.VMEM((1, L), F32), pltpu.VMEM((N_KV_HEADS, VT_ROWS, L // N_KV_HEADS), F32),
                        pltpu.VMEM((tk, L), F32), pltpu.VMEM((tk, L), F32)],
        compiler_params=_cparams(("parallel", "parallel"), VMEM_LIMIT), name="sel_prompt",
    )(qaug, kaug.reshape(bsz, tlen, r), vt, tl)


def _sel_prep_kernel(k_ref, v_ref, oh_ref, kaug_ref, vt_ref):
    kaug_ref[:, 0:KV_WIDTH] = k_ref[...].astype(BF16)
    kaug_ref[:, KV_WIDTH:] = oh_ref[...]
    vt = v_ref[...].T.astype(BF16)
    ones = jnp.ones((VSUM_ROWS, vt.shape[1]), BF16)
    for h in range(N_KV_HEADS):
        vt_ref[0, h * VT_ROWS:h * VT_ROWS + HEAD_DIM, :] = vt[h * HEAD_DIM:(h + 1) * HEAD_DIM, :]
        vt_ref[0, h * VT_ROWS + HEAD_DIM:(h + 1) * VT_ROWS, :] = ones


def _sel_prep(k, v, oh, *, tk):
    n = k.shape[0]
    tiles_per_seq = oh.shape[0] // tk
    nsel = oh.shape[1]
    return pl.pallas_call(
        _sel_prep_kernel, grid=(n // tk,),
        in_specs=[pl.BlockSpec((tk, KV_WIDTH), lambda i: (i, 0)), pl.BlockSpec((tk, KV_WIDTH), lambda i: (i, 0)),
                  pl.BlockSpec((tk, nsel), lambda i: (i % tiles_per_seq, 0))],
        out_specs=(pl.BlockSpec((tk, KV_WIDTH + nsel), lambda i: (i, 0)),
                   pl.BlockSpec((1, N_KV_HEADS * VT_ROWS, tk), lambda i: (i, 0, 0))),
        out_shape=(jax.ShapeDtypeStruct((n, KV_WIDTH + nsel), BF16),
                   jax.ShapeDtypeStruct((n // tk, N_KV_HEADS * VT_ROWS, tk), BF16)),
        compiler_params=_cparams(("parallel",)), name="sel_prep",
    )(k, v, oh)


SEL_PAGES = 16
BLOCKS_PER_STEP = 32


def _sel_paged_kernel(pt_ref, q_ref, sbt_ref, sbt_tail_ref, ohc_ref, trow_ref, knew_ref, vnew_ref, *refs, past_len):
    del pt_ref
    g = SEL_PAGES
    kpages, vpages, o_ref, m_sc, l_sc, acc_sc, kcat_sc, vcat_sc = refs[:g], refs[g:2 * g], refs[2 * g], *refs[2 * g + 1:]
    js = pl.program_id(1)
    q = q_ref[0]

    @pl.when(js == 0)
    def _():
        m_sc[...] = jnp.full(m_sc.shape, NEG, F32)
        l_sc[...] = jnp.zeros(l_sc.shape, F32)
        acc_sc[...] = jnp.zeros(acc_sc.shape, F32)

    def update(s, vt):
        m_old = m_sc[...]
        m_new = jnp.maximum(m_old, jnp.max(s, axis=1, keepdims=True))
        alpha = jnp.exp2(m_old - m_new)
        p = jnp.exp2(s - m_new)
        l_sc[...] = alpha * l_sc[...] + jnp.sum(p, axis=1, keepdims=True)
        m_sc[...] = m_new
        acc_sc[...] = acc_sc[...] * alpha + _dot_nt(p.astype(BF16), vt)

    for i in range(g):
        kcat_sc[:, i * LANE:(i + 1) * LANE] = kpages[i][0, 0].astype(BF16)
        vcat_sc[:, i * LANE:(i + 1) * LANE] = vpages[i][0, 0].astype(BF16)
    ohc = ohc_ref[...]
    s = _dot(q, kcat_sc[...]) + _dot(sbt_ref[0, 0], ohc)
    update(s, vcat_sc[...])

    @pl.when(js == pl.num_programs(1) - 1)
    def _():
        sn = _dot(q, knew_ref[0].astype(BF16)) + _dot(sbt_tail_ref[0, 0], ohc[:, 0:LANE])
        kpos = past_len + lax.broadcasted_iota(I32, sn.shape, 1)
        sn = jnp.where(kpos <= trow_ref[...], sn, NEG)
        update(sn, vnew_ref[0].astype(BF16))
        o_ref[0] = acc_sc[...] * (1.0 / jnp.maximum(l_sc[...], TINY))


def _sel_paged(page_table, qstd, sbt, ohc, trow, knew_t, vnew_t, pool_k, pool_v, layer, *, past_len):
    b, rows, _ = qstd.shape
    g = SEL_PAGES
    page = pool_k.shape[3]
    n_steps = page_table.shape[1] // g
    pspec = lambda i: pl.BlockSpec((1, 1, KV_WIDTH, page), lambda bi, j, pt, i=i: (layer, pt[bi, j * g + i], 0, 0))
    per_b = lambda a: pl.BlockSpec((1,) + a.shape[1:], lambda bi, j, pt: (bi,) + (0,) * (a.ndim - 1))
    full = lambda a: pl.BlockSpec(a.shape, lambda bi, j, pt: (0,) * a.ndim)
    grid_spec = pltpu.PrefetchScalarGridSpec(
        num_scalar_prefetch=1, grid=(b, n_steps),
        in_specs=[per_b(qstd),
                  pl.BlockSpec((1, 1) + sbt.shape[2:], lambda bi, j, pt: (bi, j, 0, 0)),
                  pl.BlockSpec((1, 1) + sbt.shape[2:], lambda bi, j, pt: (bi, n_steps, 0, 0)),
                  full(ohc), full(trow), per_b(knew_t), per_b(vnew_t)]
                 + [pspec(i) for i in range(g)] + [pspec(i) for i in range(g)],
        out_specs=pl.BlockSpec((1, rows, KV_WIDTH), lambda bi, j, pt: (bi, 0, 0)),
        scratch_shapes=[pltpu.VMEM((rows, 1), F32), pltpu.VMEM((rows, 1), F32), pltpu.VMEM((rows, KV_WIDTH), F32),
                        pltpu.VMEM((KV_WIDTH, g * page), BF16), pltpu.VMEM((KV_WIDTH, g * page), BF16)])
    return pl.pallas_call(
        functools.partial(_sel_paged_kernel, past_len=past_len),
        grid_spec=grid_spec, out_shape=jax.ShapeDtypeStruct((b, rows, KV_WIDTH), F32),
        compiler_params=_cparams(("parallel", "arbitrary"), VMEM_LIMIT), name="sel_paged",
    )(page_table, qstd, sbt, sbt, ohc, trow, knew_t, vnew_t, *([pool_k] * g), *([pool_v] * g))


def _window_kernel(q_ref, k_ref, v_ref, t_ref, o_ref, *, tq, band, kpos_base):
    qi = pl.program_id(1)
    L = N_HEADS * tq
    hl = L // N_KV_HEADS
    q1 = q_ref[0]
    t = t_ref[0]
    start = pl.multiple_of(jnp.maximum(qi * tq + tq - band, 0), LANE)
    kb = k_ref[0, pl.ds(start, band), :].astype(BF16)
    s = _dot(kb, q1)
    r_hi = t - (kpos_base + start)
    row = lax.broadcasted_iota(I32, s.shape, 0)
    s = jnp.where((row <= r_hi) & (row > r_hi - WINDOW), s, NEG)
    m = jnp.max(s, axis=0, keepdims=True)
    pb = jnp.exp2(s - m).astype(BF16)
    keep = m > NEG / 2
    vt = v_ref[0, pl.ds(start, band), :].T.astype(BF16)
    ones = jnp.ones((VSUM_ROWS, band), BF16)
    for h in range(N_KV_HEADS):
        vth = jnp.concatenate([vt[HEAD_DIM * h:HEAD_DIM * (h + 1), :], ones], axis=0)
        o_h = _dot(vth, pb[:, h * hl:(h + 1) * hl])
        for g in range(GQA):
            hd = h * GQA + g
            gs = slice(g * tq, (g + 1) * tq)
            linv = jnp.where(keep[:, hd * tq:(hd + 1) * tq], 1.0 / jnp.maximum(o_h[HEAD_DIM:HEAD_DIM + 1, gs], TINY), 0.0)
            o_ref[0, HEAD_DIM * hd:HEAD_DIM * (hd + 1), :] = o_h[0:HEAD_DIM, gs] * linv


def _window(qblk, k, v, tl, *, tq, band, kpos_base):
    nq, _, L = qblk.shape
    b, tlen, _ = k.shape
    nqb = nq // b
    shared_t = tl.shape[0] == 1
    return pl.pallas_call(
        functools.partial(_window_kernel, tq=tq, band=band, kpos_base=kpos_base),
        grid=(b, nqb),
        in_specs=[pl.BlockSpec((1, KV_WIDTH, L), lambda i, j: (i * nqb + j, 0, 0)),
                  pl.BlockSpec((1, tlen, KV_WIDTH), lambda i, j: (i, 0, 0)),
                  pl.BlockSpec((1, tlen, KV_WIDTH), lambda i, j: (i, 0, 0)),
                  pl.BlockSpec((1, 1, L), (lambda i, j: (0, 0, 0)) if shared_t else (lambda i, j: (j, 0, 0)))],
        out_specs=pl.BlockSpec((1, N_HEADS * HEAD_DIM, tq), lambda i, j: (i * nqb + j, 0, 0)),
        out_shape=jax.ShapeDtypeStruct((nq, N_HEADS * HEAD_DIM, tq), F32),
        compiler_params=_cparams(("parallel", "parallel"), VMEM_LIMIT), name="window",
    )(qblk, k, v, tl)


def _combine_kernel(x_ref, a_ref, oc_ref, os_ref, ow_ref, gt_ref, e_ref, woa_ref, wob_ref, o_ref, *, tm):
    hi, lo = _split_bf16(gt_ref[...])
    ge = _dot(e_ref[...], hi) + _dot(e_ref[...], lo)
    bw = N_HEADS * HEAD_DIM
    parts = []
    for j in range(tm // LANE):
        ls = slice(j * LANE, (j + 1) * LANE)
        parts.append(ge[0:bw, ls] * oc_ref[j] + ge[bw:2 * bw, ls] * os_ref[j] + ge[2 * bw:3 * bw, ls] * ow_ref[j])
    mixt = parts[0] if len(parts) == 1 else jnp.concatenate(parts, axis=1)
    mix = mixt.T.astype(BF16)
    o_ref[...] = x_ref[...] + _dot(a_ref[...], woa_ref[...]) + _dot(mix, wob_ref[...])


def _combine(x, a, oc, os_, ow, gt, e, woa, wob, *, tm):
    n, dm = x.shape
    bw = N_HEADS * HEAD_DIM
    full = lambda arr: pl.BlockSpec(arr.shape, lambda i: (0,) * arr.ndim)
    ospec = pl.BlockSpec((tm // LANE, bw, LANE), lambda i: (i, 0, 0))
    return pl.pallas_call(
        functools.partial(_combine_kernel, tm=tm),
        grid=(n // tm,),
        in_specs=[pl.BlockSpec((tm, dm), lambda i: (i, 0)), pl.BlockSpec((tm, a.shape[1]), lambda i: (i, 0)),
                  ospec, ospec, ospec, pl.BlockSpec((32, tm), lambda i: (0, i)), full(e), full(woa), full(wob)],
        out_specs=pl.BlockSpec((tm, dm), lambda i: (i, 0)),
        out_shape=jax.ShapeDtypeStruct((n, dm), F32),
        compiler_params=_cparams(("parallel",), VMEM_LIMIT), name="combine",
    )(x, a, oc, os_, ow, gt, e, woa, wob)


def _ffn_dense_kernel(x_ref, g_ref, wg_ref, wu_ref, wd_ref, o_ref):
    xf = x_ref[...]
    h = _rms(xf, g_ref[...]).astype(BF16)
    gate = _dot(h, wg_ref[...])
    up = _dot(h, wu_ref[...])
    act = (gate * _sigmoid(gate) * up).astype(BF16)
    o_ref[...] = xf + _dot(act, wd_ref[...])


def _ffn_dense(x, g, wg, wu, wd, *, tm):
    n, dm = x.shape
    full = lambda arr: pl.BlockSpec(arr.shape, lambda i: (0,) * arr.ndim)
    return pl.pallas_call(
        _ffn_dense_kernel, grid=(n // tm,),
        in_specs=[pl.BlockSpec((tm, dm), lambda i: (i, 0)), full(g), full(wg), full(wu), full(wd)],
        out_specs=pl.BlockSpec((tm, dm), lambda i: (i, 0)),
        out_shape=jax.ShapeDtypeStruct((n, dm), F32),
        compiler_params=_cparams(("parallel",), VMEM_LIMIT), name="ffn_dense",
    )(x, g, wg, wu, wd)


def _router_kernel(x_ref, g_ref, wrh_ref, wrl_ref, tri_ref, h_ref, meta_ref, cnt_ref, *, tm, n_exp):
    h = _rms(x_ref[...], g_ref[...])
    h_ref[...] = h
    hi, lo = _split_bf16(h)
    logits = _dot(hi, wrh_ref[...]) + _dot(lo, wrh_ref[...]) + _dot(hi, wrl_ref[...])
    lane = lax.broadcasted_iota(I32, (tm, LANE), 1)
    lanef = lane.astype(F32)
    logits = jnp.where(lane < n_exp, logits, NEG)
    m1 = jnp.max(logits, axis=1, keepdims=True)
    i1 = jnp.min(jnp.where(logits == m1, lanef, 1e9), axis=1, keepdims=True)
    rest = jnp.where(lanef == i1, -3e38, logits)
    m2 = jnp.max(rest, axis=1, keepdims=True)
    i2 = jnp.min(jnp.where(rest == m2, lanef, 1e9), axis=1, keepdims=True)
    e2 = jnp.exp(m2 - m1)
    w1 = 1.0 / (1.0 + e2)
    w2 = e2 / (1.0 + e2)
    hit1 = lanef == i1
    hit2 = lanef == i2
    msel = jnp.where(hit1 | hit2, 1.0, 0.0)
    ranks = _dot(tri_ref[...], msel.astype(BF16))
    r1 = jnp.sum(jnp.where(hit1, ranks, 0.0), axis=1, keepdims=True)
    r2 = jnp.sum(jnp.where(hit2, ranks, 0.0), axis=1, keepdims=True)
    meta = jnp.where(lane == 0, i1, 0.0)
    for k, val in enumerate((i2, w1, w2, r1, r2)):
        meta = jnp.where(lane == k + 1, val, meta)
    meta_ref[...] = meta
    cnt_ref[0] = jnp.sum(msel, axis=0, keepdims=True)


def _router(x, g, wrh, wrl, tri, *, tm, n_exp):
    n, dm = x.shape
    full = lambda arr: pl.BlockSpec(arr.shape, lambda i: (0,) * arr.ndim)
    return pl.pallas_call(
        functools.partial(_router_kernel, tm=tm, n_exp=n_exp), grid=(n // tm,),
        in_specs=[pl.BlockSpec((tm, dm), lambda i: (i, 0)), full(g), full(wrh), full(wrl), full(tri)],
        out_specs=(pl.BlockSpec((tm, dm), lambda i: (i, 0)), pl.BlockSpec((tm, LANE), lambda i: (i, 0)),
                   pl.BlockSpec((1, 1, LANE), lambda i: (i, 0, 0))),
        out_shape=(jax.ShapeDtypeStruct((n, dm), F32), jax.ShapeDtypeStruct((n, LANE), F32),
                   jax.ShapeDtypeStruct((n // tm, 1, LANE), F32)),
        compiler_params=_cparams(("parallel",), VMEM_LIMIT), name="router",
    )(x, g, wrh, wrl, tri)


def _row_copy(src, src_row, dst, dst_row, sem):
    return pltpu.make_async_copy(src.at[pl.ds(src_row, 1)], dst.at[pl.ds(dst_row, 1)], sem)


def _dispatch_kernel(dest_ref, h_ref, xg_in_ref, xg_ref, sem, *, tm):
    del xg_in_ref

    def issue(r, carry):
        for k in range(TOP_K):
            _row_copy(h_ref, r, xg_ref, dest_ref[TOP_K * r + k], sem).start()
        return carry

    lax.fori_loop(0, tm, issue, 0)

    def drain(r, carry):
        for k in range(TOP_K):
            _row_copy(h_ref, 0, xg_ref, 0, sem).wait()
        return carry

    lax.fori_loop(0, tm, drain, 0)


def _dispatch(dest, h, xg0, *, tm):
    n, dm = h.shape
    return pl.pallas_call(
        functools.partial(_dispatch_kernel, tm=tm), grid=(n // tm,),
        in_specs=[pl.BlockSpec((TOP_K * tm,), lambda i: (i,), memory_space=pltpu.SMEM),
                  pl.BlockSpec((tm, dm), lambda i: (i, 0)),
                  pl.BlockSpec(memory_space=pl.ANY)],
        out_specs=pl.BlockSpec(memory_space=pl.ANY),
        out_shape=jax.ShapeDtypeStruct(xg0.shape, F32),
        scratch_shapes=[pltpu.SemaphoreType.DMA],
        input_output_aliases={2: 0},
        compiler_params=_cparams(("arbitrary",), VMEM_LIMIT), name="moe_dispatch",
    )(dest, h, xg0)


def _expert_ffn_kernel(te_ref, nused_ref, x_ref, wg_ref, wu_ref, wd_ref, o_ref):
    s = pl.program_id(0)

    @pl.when(s < nused_ref[0])
    def _():
        h = x_ref[...].astype(BF16)
        gate = _dot(h, wg_ref[0])
        up = _dot(h, wu_ref[0])
        act = (gate * _sigmoid(gate) * up).astype(BF16)
        o_ref[...] = _dot(act, wd_ref[0])

    @pl.when(s >= nused_ref[0])
    def _():
        o_ref[...] = jnp.zeros(o_ref.shape, F32)


def _expert_ffn(te, nused, xg, wg, wu, wd, *, ts):
    s_rows, dm = xg.shape
    dff = wg.shape[2]
    grid_spec = pltpu.PrefetchScalarGridSpec(
        num_scalar_prefetch=2, grid=(s_rows // ts,),
        in_specs=[pl.BlockSpec((ts, dm), lambda i, te, nu: (i, 0)),
                  pl.BlockSpec((1, dm, dff), lambda i, te, nu: (te[i], 0, 0)),
                  pl.BlockSpec((1, dm, dff), lambda i, te, nu: (te[i], 0, 0)),
                  pl.BlockSpec((1, dff, dm), lambda i, te, nu: (te[i], 0, 0))],
        out_specs=pl.BlockSpec((ts, dm), lambda i, te, nu: (i, 0)))
    return pl.pallas_call(
        _expert_ffn_kernel, grid_spec=grid_spec,
        out_shape=jax.ShapeDtypeStruct((s_rows, dm), F32),
        compiler_params=_cparams(("arbitrary",), VMEM_LIMIT), name="expert_ffn",
    )(te, nused, xg, wg, wu, wd)


def _moe_combine_kernel(dest_ref, x_ref, meta_ref, y_ref, o_ref, buf, sem, *, tm):
    def issue(r, carry):
        for k in range(TOP_K):
            _row_copy(y_ref, dest_ref[TOP_K * r + k], buf.at[k], r, sem).start()
        return carry

    lax.fori_loop(0, tm, issue, 0)

    def drain(r, carry):
        for k in range(TOP_K):
            _row_copy(y_ref, 0, buf.at[k], 0, sem).wait()
        return carry

    lax.fori_loop(0, tm, drain, 0)
    meta = meta_ref[...]
    o_ref[...] = x_ref[...] + meta[:, 2:3] * buf[0] + meta[:, 3:4] * buf[1]


def _moe_combine(dest, x, meta, y, *, tm):
    n, dm = x.shape
    return pl.pallas_call(
        functools.partial(_moe_combine_kernel, tm=tm), grid=(n // tm,),
        in_specs=[pl.BlockSpec((TOP_K * tm,), lambda i: (i,), memory_space=pltpu.SMEM),
                  pl.BlockSpec((tm, dm), lambda i: (i, 0)),
                  pl.BlockSpec((tm, LANE), lambda i: (i, 0)),
                  pl.BlockSpec(memory_space=pl.ANY)],
        out_specs=pl.BlockSpec((tm, dm), lambda i: (i, 0)),
        out_shape=jax.ShapeDtypeStruct((n, dm), F32),
        scratch_shapes=[pltpu.VMEM((TOP_K, tm, dm), F32), pltpu.SemaphoreType.DMA],
        compiler_params=_cparams(("arbitrary",), VMEM_LIMIT), name="moe_combine",
    )(dest, x, meta, y)


def _moe(x, g, wr, wg, wu, wd, *, tm, ts):
    n, dm = x.shape
    n_exp = wr.shape[1]
    wr_pad = jnp.zeros((dm, LANE), F32).at[:, :n_exp].set(wr)
    wrh, wrl = _split_bf16(wr_pad)
    tri = jnp.tril(jnp.ones((tm, tm), F32), -1).astype(BF16)
    h, meta, cnt = _router(x, g, wrh, wrl, tri, tm=tm, n_exp=n_exp)
    cnt = cnt[:, 0, :n_exp].astype(I32)
    tile_off = jnp.cumsum(cnt, axis=0) - cnt
    tot = jnp.sum(cnt, axis=0)
    padded = ((tot + ts - 1) // ts) * ts
    gend = jnp.cumsum(padded)
    gstart = gend - padded
    ei = meta[:, 0:2].astype(I32)
    rk = meta[:, 4:6].astype(I32)
    base = jnp.broadcast_to((gstart[None, :] + tile_off)[:, None, :], (n // tm, tm, n_exp)).reshape(n, 1, n_exp)
    pick = ei[:, :, None] == jnp.arange(n_exp, dtype=I32)[None, None, :]
    dest = (jnp.sum(jnp.where(pick, base, 0), axis=-1) + rk).reshape(-1)
    n_slots = TOP_K * n + n_exp * ts
    n_slot_tiles = n_slots // ts
    nused = (gend[-1] // ts).astype(I32).reshape(1)
    tile_start = jnp.minimum(jnp.arange(n_slot_tiles, dtype=I32), jnp.maximum(nused[0] - 1, 0)) * ts
    te = jnp.minimum(jnp.sum((gend[None, :] <= tile_start[:, None]).astype(I32), axis=1), n_exp - 1)
    xg = _dispatch(dest, h, jnp.zeros((n_slots, dm), F32), tm=tm)
    y = _expert_ffn(te, nused, xg, wg, wu, wd, ts=ts)
    return _moe_combine(dest, x, meta, y, tm=tm)


def _norm_kernel(x_ref, g_ref, o_ref):
    o_ref[...] = _rms(x_ref[...], g_ref[...])


def _final_norm(x, g, *, tm):
    n, dm = x.shape
    return pl.pallas_call(
        _norm_kernel, grid=(n // tm,),
        in_specs=[pl.BlockSpec((tm, dm), lambda i: (i, 0)), pl.BlockSpec(g.shape, lambda i: (0, 0))],
        out_specs=pl.BlockSpec((tm, dm), lambda i: (i, 0)),
        out_shape=jax.ShapeDtypeStruct((n, dm), F32),
        compiler_params=_cparams(("parallel",)), name="final_norm",
    )(x, g)


PAGES_PER_STEP = 8


def _page_gather_kernel(pt_ref, *refs, n_chunked):
    del pt_ref
    g = PAGES_PER_STEP
    n_arr = (len(refs) - 1) // (g + 1)
    pages, outs, scr = refs[:n_arr * g], refs[n_arr * g:n_arr * (g + 1)], refs[-1]
    for c in range(n_arr):
        for i in range(g):
            p = pages[c * g + i][0, 0].T
            npos = p.shape[0]
            if c < n_chunked:
                scr[...] = p
                rows = npos // CMP_STRIDE
                for s in range(CMP_STRIDE):
                    outs[c][0, i * rows:(i + 1) * rows, s * KV_WIDTH:(s + 1) * KV_WIDTH] = scr[pl.ds(s, rows, stride=CMP_STRIDE), :]
            else:
                outs[c][0, i * npos:(i + 1) * npos, :] = p


def _page_gather(page_table, pools_t, layer, *, n_chunked):
    b, n_pages = page_table.shape
    page = pools_t[0].shape[3]
    g = PAGES_PER_STEP
    n_arr = len(pools_t)
    in_specs, operands = [], []
    for p in pools_t:
        for i in range(g):
            in_specs.append(pl.BlockSpec((1, 1, KV_WIDTH, page), lambda bi, j, pt, i=i: (layer, pt[bi, j * g + i], 0, 0)))
            operands.append(p)
    rows = page // CMP_STRIDE
    out_specs, out_shape = [], []
    for c in range(n_arr):
        if c < n_chunked:
            out_specs.append(pl.BlockSpec((1, g * rows, CMP_STRIDE * KV_WIDTH), lambda bi, j, pt: (bi, j, 0)))
            out_shape.append(jax.ShapeDtypeStruct((b, n_pages * rows, CMP_STRIDE * KV_WIDTH), F32))
        else:
            out_specs.append(pl.BlockSpec((1, g * page, KV_WIDTH), lambda bi, j, pt: (bi, j, 0)))
            out_shape.append(jax.ShapeDtypeStruct((b, n_pages * page, KV_WIDTH), F32))
    grid_spec = pltpu.PrefetchScalarGridSpec(
        num_scalar_prefetch=1, grid=(b, n_pages // g), in_specs=in_specs, out_specs=out_specs,
        scratch_shapes=[pltpu.VMEM((page, KV_WIDTH), F32)])
    return pl.pallas_call(
        functools.partial(_page_gather_kernel, n_chunked=n_chunked),
        grid_spec=grid_spec, out_shape=out_shape,
        compiler_params=_cparams(("parallel", "parallel"), VMEM_LIMIT), name="page_gather",
    )(page_table, *operands)


def _rope_tables(pos):
    half = HEAD_DIM // 2
    inv = ROPE_THETA ** (-jnp.arange(half, dtype=F32) / half)
    ang = pos.astype(F32)[:, None] * inv[None, :]
    cos, sin = jnp.cos(ang), jnp.sin(ang)
    cos_std = jnp.tile(cos, (1, 4))
    sin_std = jnp.tile(jnp.concatenate([-sin, sin], axis=1), (1, 2))
    return cos_std, sin_std, cos.T, sin.T


def _cmp_to_sel_t(n_cmp_pad, n_sel_pad):
    i0 = np.arange(n_cmp_pad, dtype=np.int64)[None, :] * CMP_STRIDE
    j0 = np.arange(n_sel_pad, dtype=np.int64)[:, None] * SEL_LEN
    ov = np.clip(np.minimum(i0 + CMP_LEN, j0 + SEL_LEN) - np.maximum(i0, j0), 0, CMP_LEN)
    return jnp.asarray(ov.astype(np.float32) / CMP_LEN, dtype=BF16)


def _block_onehot(n_keys, n_sel_pad):
    blk = np.arange(n_keys)[:, None] // SEL_LEN
    return jnp.asarray((blk == np.arange(n_sel_pad)[None, :]).astype(np.float32), dtype=BF16)


def _lane_positions(pos_tiles, tq):
    return jnp.tile(pos_tiles.astype(I32), (1, N_HEADS))[:, None, :]


def _compress_weights(w1, w2, pe):
    nr = CMP_LEN // CMP_STRIDE
    w1r = w1.reshape(nr, CMP_STRIDE, HEAD_DIM, CMP_HIDDEN)
    eye = jnp.eye(N_KV_HEADS, dtype=F32)
    wbig = jnp.einsum("rsde,gh->sgdrhe", w1r, eye)
    wbig = wbig.reshape(CMP_STRIDE * KV_WIDTH, nr * N_KV_HEADS * CMP_HIDDEN).astype(BF16)
    w2big = jnp.einsum("ed,gh->gehd", w2, eye).reshape(N_KV_HEADS * CMP_HIDDEN, KV_WIDTH).astype(BF16)
    pe8 = jnp.tile(pe.reshape(1, CMP_LEN * HEAD_DIM), (8, 1)).astype(BF16)
    return wbig, pe8, w1.astype(BF16), w2big


def _layer_weights(l, g_mix, w_in, g_sgu, w_sgu, b_sgu, w_cmpk1, w_cmpk2, pe_cmpk, w_cmpv1, w_cmpv2, pe_cmpv, w_o, dec_t):
    aw = A_GROUPS * CHUNK
    bw = N_HEADS * HEAD_DIM
    w = w_in[l]
    wstd = jnp.concatenate([w[:, :2 * aw], w[:, 2 * aw + bw:2 * aw + bw + 6 * KV_WIDTH]], axis=1).astype(BF16)
    n_gate = 3 * N_HEADS
    wt = jnp.concatenate([w[:, 2 * aw:2 * aw + bw], w[:, -n_gate:], jnp.zeros((w.shape[0], 32 - n_gate), F32)], axis=1).T.astype(BF16)
    causal = jnp.tril(jnp.ones((CHUNK, CHUNK), bool))
    wsgu_p = jnp.where(causal[None], w_sgu[l], 0.0).astype(BF16)
    bsgu_p = jnp.broadcast_to(b_sgu[l][:, :, None], (A_GROUPS, CHUNK, CHUNK)).astype(F32)
    reps = CHUNK // dec_t
    wsmall = jnp.where(causal[None, :dec_t, :dec_t], w_sgu[l][:, :dec_t, :dec_t], 0.0)
    wsgu_s = jnp.einsum("ab,gts->gatbs", jnp.eye(reps, dtype=F32), wsmall).reshape(A_GROUPS, CHUNK, CHUNK).astype(BF16)
    bsgu_s = jnp.broadcast_to(jnp.tile(b_sgu[l][:, :dec_t], (1, reps))[:, :, None], (A_GROUPS, CHUNK, CHUNK)).astype(F32)
    ck_w = _compress_weights(w_cmpk1[l], w_cmpk2[l], pe_cmpk[l])
    cv_w = _compress_weights(w_cmpv1[l], w_cmpv2[l], pe_cmpv[l])
    cv_w = cv_w[:3] + (cv_w[3].T,)
    e = np.zeros((3 * bw, 32), np.float32)
    for br in range(3):
        for hd in range(N_HEADS):
            e[br * bw + hd * HEAD_DIM:br * bw + (hd + 1) * HEAD_DIM, br * N_HEADS + hd] = 1.0
    return dict(gmix=g_mix[l][None, :], wstd=wstd, wt=wt, gsgu=g_sgu[l][None, :],
                wsgu_p=wsgu_p, bsgu_p=bsgu_p, wsgu_s=wsgu_s, bsgu_s=bsgu_s, ck_w=ck_w, cv_w=cv_w,
                e=jnp.asarray(e, dtype=BF16), woa=w_o[l][:aw].astype(BF16), wob=w_o[l][aw:].astype(BF16))


def _channel_mixer(l, x, g_ffn, dense_w, w_router, moe_w, *, tm_dense, tm_moe):
    g = g_ffn[l][None, :]
    i = l // 2
    if l % 2 == 0:
        return _ffn_dense(x, g, *(w[i] for w in dense_w), tm=tm_dense)
    return _moe(x, g, w_router[i], *(w[i] for w in moe_w), tm=tm_moe, ts=256)


def kernel(x_prompt, x_sample, cache_cmp_k, cache_cmp_v, cache_sel_k, cache_sel_v, cache_win_k, cache_win_v, page_table,
           g_mix, w_in, g_sgu, w_sgu, b_sgu, w_cmpk1, w_cmpk2, pe_cmpk, w_cmpv1, w_cmpv2, pe_cmpv, w_o, g_ffn,
           w_ff_gate, w_ff_up, w_ff_down, w_router, w_moe_gate, w_moe_up, w_moe_down, g_final):
    bsz, seq, dm = x_prompt.shape
    dec_b, dec_t, _ = x_sample.shape
    depth = g_mix.shape[0]
    page = cache_cmp_k.shape[2]
    past_len = page_table.shape[1] * page
    wbuf = cache_win_k.shape[2]
    n_p, n_s = bsz * seq, dec_b * dec_t
    assert n_s == CHUNK and seq % 512 == 0 and wbuf == WINDOW and past_len % 2048 == 0
    tq_p, tq_s = LANE, LANE // N_HEADS
    nq_b = seq // tq_p
    band = WINDOW + tq_p

    pos_p = jnp.tile(jnp.arange(seq, dtype=I32), bsz)
    pos_s = jnp.tile(past_len + jnp.arange(dec_t, dtype=I32), dec_b)
    rope_p = _rope_tables(pos_p)
    rope_s = _rope_tables(pos_s)
    tl_p = _lane_positions(jnp.arange(seq, dtype=I32).reshape(nq_b, tq_p), tq_p)
    tok_s = past_len + jnp.minimum(jnp.arange(tq_s, dtype=I32), dec_t - 1)
    tl_s = _lane_positions(tok_s[None, :], tq_s)
    nch_p = seq // CMP_STRIDE
    nsel_p = seq // SEL_LEN
    nch_s = past_len // CMP_STRIDE
    nsel_s = -(-(past_len // SEL_LEN + 1) // LANE) * LANE
    mt_p = _cmp_to_sel_t(nch_p, nsel_p)
    mt_s = _cmp_to_sel_t(nch_s, nsel_s)
    oh_p = _block_onehot(seq, nsel_p)
    n_sel_steps = past_len // (SEL_PAGES * page) + 1
    assert nsel_s >= n_sel_steps * BLOCKS_PER_STEP and SEL_PAGES * page == BLOCKS_PER_STEP * SEL_LEN
    ohc_s = _block_onehot(SEL_PAGES * page, LANE).T
    trow_s = jnp.broadcast_to(jnp.tile(tok_s, N_HEADS)[:, None], (N_HEADS * tq_s, LANE)).astype(I32)
    head_of_row = jnp.asarray((np.arange(N_HEADS)[None, :] // GQA == np.arange(N_KV_HEADS)[:, None]).astype(np.float32))

    pools_t = [jnp.transpose(c, (0, 1, 3, 4, 2)).reshape(depth, c.shape[1], KV_WIDTH, page)
               for c in (cache_cmp_k, cache_cmp_v, cache_sel_k, cache_sel_v)]
    flat = lambda a: a.reshape(a.shape[0], a.shape[1], KV_WIDTH)

    dense_w = tuple(w.astype(BF16) for w in (w_ff_gate, w_ff_up, w_ff_down))
    moe_w = tuple(w.astype(BF16) for w in (w_moe_gate, w_moe_up, w_moe_down))
    xp = x_prompt.reshape(n_p, dm)
    xs = x_sample.reshape(n_s, dm)
    outs = [[] for _ in range(13)]
    for l in range(depth):
        lw = _layer_weights(l, g_mix, w_in, g_sgu, w_sgu, b_sgu, w_cmpk1, w_cmpk2, pe_cmpk, w_cmpv1, w_cmpv2, pe_cmpv, w_o, dec_t)
        a, _, kc, vc, ks, vs, kw, vw, qblk, gt = _inproj(
            xp, lw["gmix"], lw["wstd"], lw["wt"], lw["gsgu"], lw["wsgu_p"], lw["bsgu_p"], *rope_p, tm=512, emit_qblk=True)
        ck = _compress(kc.reshape(bsz, nch_p, CMP_STRIDE * KV_WIDTH), *lw["ck_w"], transpose_out=False)
        cvt = _compress(vc.reshape(bsz, nch_p, CMP_STRIDE * KV_WIDTH), *lw["cv_w"], transpose_out=True)
        oc, qaug = _cmp_select(qblk, ck, cvt, mt_p, tl_p, tq=tq_p, nq_per_b=nq_b)
        kaug, vst = _sel_prep(ks, vs, oh_p, tk=SEL_TK)
        osel = _sel_prompt(qaug, kaug, vst, tl_p, tq=tq_p, tk=SEL_TK, bsz=bsz)
        ow = _window(qblk, kw.reshape(bsz, seq, KV_WIDTH), vw.reshape(bsz, seq, KV_WIDTH), tl_p, tq=tq_p, band=band, kpos_base=0)
        xp = _combine(xp, a, oc, osel, ow, gt, lw["e"], lw["woa"], lw["wob"], tm=512)
        kv4 = lambda t: t.reshape(bsz, seq, N_KV_HEADS, HEAD_DIM)
        for idx, t in enumerate((kc, vc, ks, vs)):
            outs[idx].append(kv4(t))
        nwin_p = min(WINDOW, seq)
        outs[4].append(kv4(kw)[:, seq - nwin_p:])
        outs[5].append(kv4(vw)[:, seq - nwin_p:])

        a, v_s, kc, vc, ks, vs, kw, vw, qt, gt = _inproj(
            xs, lw["gmix"], lw["wstd"], lw["wt"], lw["gsgu"], lw["wsgu_s"], lw["bsgu_s"], *rope_s, tm=CHUNK, emit_qblk=False)
        q4 = qt.reshape(N_HEADS, HEAD_DIM, dec_b, dec_t)
        q4 = jnp.pad(q4, ((0, 0), (0, 0), (0, 0), (0, tq_s - dec_t)))
        qb = jnp.einsum("hdbt,gh->bgdht", q4.astype(F32), head_of_row).astype(BF16)
        qblk_s = qb.reshape(dec_b, KV_WIDTH, N_HEADS * tq_s)
        past_ck, past_cv = _page_gather(page_table, pools_t[:2], l, n_chunked=2)
        ck = _compress(past_ck, *lw["ck_w"], transpose_out=False)
        cvt = _compress(past_cv, *lw["cv_w"], transpose_out=True)
        oc, _, sbt = _cmp_select(qblk_s, ck, cvt, mt_s, tl_s, tq=tq_s, nq_per_b=1, n_bias_steps=n_sel_steps)
        new_t = lambda t: jnp.pad(t.reshape(dec_b, dec_t, KV_WIDTH).transpose(0, 2, 1), ((0, 0), (0, 0), (0, page - dec_t)))
        osel_rows = _sel_paged(page_table, jnp.swapaxes(qblk_s, 1, 2), sbt, ohc_s, trow_s, new_t(ks), new_t(vs),
                               pools_t[2], pools_t[3], l, past_len=past_len)
        osel = jnp.einsum("bhtgd,gh->bhdt", osel_rows.reshape(dec_b, N_HEADS, tq_s, N_KV_HEADS, HEAD_DIM),
                          head_of_row).reshape(dec_b, N_HEADS * HEAD_DIM, tq_s)
        kw_all = jnp.concatenate([flat(cache_win_k[l]), kw.reshape(dec_b, dec_t, KV_WIDTH)], axis=1)
        vw_all = jnp.concatenate([flat(cache_win_v[l]), vw.reshape(dec_b, dec_t, KV_WIDTH)], axis=1)
        wpad = ((0, 0), (0, band - wbuf - dec_t), (0, 0))
        ow = _window(qblk_s, jnp.pad(kw_all, wpad), jnp.pad(vw_all, wpad), tl_s, tq=tq_s, band=band, kpos_base=past_len - wbuf)
        untile = lambda o: o[:, :, :dec_t].transpose(1, 0, 2).reshape(1, N_HEADS * HEAD_DIM, n_s)
        xs = _combine(xs, a, untile(oc), untile(osel), untile(ow), gt, lw["e"], lw["woa"], lw["wob"], tm=CHUNK)
        kv4s = lambda t: t.reshape(dec_b, dec_t, N_KV_HEADS, HEAD_DIM)
        for idx, t in enumerate((kc, vc, ks, vs)):
            outs[6 + idx].append(kv4s(t))
        nwin_s = min(WINDOW, wbuf + dec_t)
        outs[10].append(kw_all[:, wbuf + dec_t - nwin_s:].reshape(dec_b, nwin_s, N_KV_HEADS, HEAD_DIM))
        outs[11].append(vw_all[:, wbuf + dec_t - nwin_s:].reshape(dec_b, nwin_s, N_KV_HEADS, HEAD_DIM))
        outs[12].append(v_s.reshape(dec_b, dec_t, A_GROUPS * CHUNK))

        xp = _channel_mixer(l, xp, g_ffn, dense_w, w_router, moe_w, tm_dense=256, tm_moe=512)
        xs = _channel_mixer(l, xs, g_ffn, dense_w, w_router, moe_w, tm_dense=CHUNK, tm_moe=CHUNK)

    gf = g_final[None, :]
    y_prompt = _final_norm(xp, gf, tm=512).reshape(bsz, seq, dm)
    y_sample = _final_norm(xs, gf, tm=CHUNK).reshape(dec_b, dec_t, dm)
    return (y_prompt, y_sample) + tuple(jnp.stack(o, axis=0) for o in outs)
```

```python
import functools
import math

import numpy as np
import jax
import jax.numpy as jnp
from jax import lax
from jax.experimental import pallas as pl
from jax.experimental.pallas import tpu as pltpu

F32 = jnp.float32
BF16 = jnp.bfloat16
I32 = jnp.int32

A_GROUPS = 4
N_HEADS = 8
N_KV_HEADS = 2
HEAD_DIM = 64
CMP_LEN = 32
CMP_STRIDE = 16
CMP_HIDDEN = 128
SEL_LEN = 64
N_SEL = 16
WINDOW = 512
CHUNK = 128
ROPE_THETA = 10000.0
TOP_K = 2
EPS = 1e-6
NEG = -1e30
BIG = 1e30
TINY = 1e-30
PICKED = -3e38
N_FORCED = 3
SCALE = HEAD_DIM ** -0.5
QSCALE = SCALE * math.log2(math.e)

LANE = 128
KV_WIDTH = N_KV_HEADS * HEAD_DIM
GQA = N_HEADS // N_KV_HEADS
VMEM_LIMIT = 56 * 1024 * 1024


def _cparams(sem, vmem=None):
    return pltpu.CompilerParams(dimension_semantics=sem, vmem_limit_bytes=vmem)


def _rms(xf, g):
    return xf * lax.rsqrt(jnp.mean(xf * xf, axis=-1, keepdims=True) + EPS) * g


def _gelu(x):
    c = math.sqrt(2.0 / math.pi)
    return x * (0.5 * (1.0 + jnp.tanh(c * (x + 0.044715 * (x * x * x)))))


def _sigmoid(x):
    return 1.0 / (1.0 + jnp.exp(-x))


def _dot(a, b):
    return jnp.dot(a, b, preferred_element_type=F32)


def _dot_nt(a, b):
    return lax.dot_general(a, b, (((1,), (1,)), ((), ())), preferred_element_type=F32)


def _split_bf16(x):
    hi = x.astype(BF16)
    lo = (x - hi.astype(F32)).astype(BF16)
    return hi, lo


def _inproj_kernel(x_ref, gmix_ref, wstd_ref, wt_ref, gsgu_ref, wsgu_ref, bsgu_ref,
                   cos_ref, sin_ref, cost_ref, sint_ref,
                   a_ref, v_ref, kc_ref, vc_ref, ks_ref, vs_ref, kw_ref, vw_ref, q_ref, gt_ref,
                   *, tm, emit_qblk):
    xf = x_ref[...]
    h = _rms(xf, gmix_ref[...]).astype(BF16)
    z = _dot(h, wstd_ref[...])
    zt = _dot_nt(wt_ref[...], h)
    aw = A_GROUPS * CHUNK
    u = _gelu(z[:, 0:aw])
    vv = _gelu(z[:, aw:2 * aw])
    mu = jnp.mean(vv, axis=-1, keepdims=True)
    d = vv - mu
    var = jnp.mean(d * d, axis=-1, keepdims=True)
    v = d * lax.rsqrt(var + EPS) * gsgu_ref[...]
    v_ref[...] = v
    vb = v.astype(BF16)
    nc = tm // CHUNK
    for g in range(A_GROUPS):
        gs = slice(g * CHUNK, (g + 1) * CHUNK)
        parts = [vb[c * CHUNK:(c + 1) * CHUNK, gs] for c in range(nc)]
        xg = parts[0] if nc == 1 else jnp.concatenate(parts, axis=1)
        yg = _dot(wsgu_ref[g], xg)
        for c in range(nc):
            cs = slice(c * CHUNK, (c + 1) * CHUNK)
            mixed = yg[:, cs] + bsgu_ref[g]
            a_ref[cs, gs] = (u[cs, gs] * mixed).astype(BF16)

    cosr = cos_ref[...]
    sinr = sin_ref[...]
    lane = lax.broadcasted_iota(I32, (tm, KV_WIDTH), 1)
    first = (lane % HEAD_DIM) < (HEAD_DIM // 2)

    def rope(x):
        rot = jnp.where(first, pltpu.roll(x, KV_WIDTH - HEAD_DIM // 2, 1), pltpu.roll(x, HEAD_DIM // 2, 1))
        return x * cosr + rot * sinr

    o = 2 * aw
    kc_ref[...] = rope(z[:, o:o + 128])
    vc_ref[...] = z[:, o + 128:o + 256]
    ks_ref[...] = rope(z[:, o + 256:o + 384])
    vs_ref[...] = z[:, o + 384:o + 512]
    kw_ref[...] = rope(z[:, o + 512:o + 640])
    vw_ref[...] = z[:, o + 640:o + 768]

    ct = cost_ref[...]
    st = sint_ref[...]
    half = HEAD_DIM // 2
    for hd in range(N_HEADS):
        x1 = zt[HEAD_DIM * hd:HEAD_DIM * hd + half]
        x2 = zt[HEAD_DIM * hd + half:HEAD_DIM * (hd + 1)]
        qh = jnp.concatenate([(x1 * ct - x2 * st) * QSCALE, (x2 * ct + x1 * st) * QSCALE], axis=0).astype(BF16)
        if emit_qblk:
            kvh = hd // GQA
            zero = jnp.zeros((HEAD_DIM, LANE), BF16)
            for j in range(tm // LANE):
                ls = slice(hd * LANE, (hd + 1) * LANE)
                q_ref[j, HEAD_DIM * kvh:HEAD_DIM * (kvh + 1), ls] = qh[:, j * LANE:(j + 1) * LANE]
                q_ref[j, HEAD_DIM * (1 - kvh):HEAD_DIM * (2 - kvh), ls] = zero
        else:
            q_ref[HEAD_DIM * hd:HEAD_DIM * (hd + 1), :] = qh
    nq = N_HEADS * HEAD_DIM
    gt_ref[...] = _sigmoid(zt[nq:nq + 32])


def _inproj(x, gmix, wstd, wt, gsgu, wsgu, bsgu, cos, sin, cost, sint, *, tm, emit_qblk):
    n, dm = x.shape
    nt = n // tm
    row = lambda w: pl.BlockSpec((tm, w), lambda i: (i, 0))
    full = lambda a: pl.BlockSpec(a.shape, lambda i: (0,) * a.ndim)
    if emit_qblk:
        q_shape = jax.ShapeDtypeStruct((n // LANE, KV_WIDTH, N_HEADS * LANE), BF16)
        q_spec = pl.BlockSpec((tm // LANE, KV_WIDTH, N_HEADS * LANE), lambda i: (i, 0, 0))
    else:
        q_shape = jax.ShapeDtypeStruct((N_HEADS * HEAD_DIM, n), BF16)
        q_spec = pl.BlockSpec((N_HEADS * HEAD_DIM, tm), lambda i: (0, i))
    kv = jax.ShapeDtypeStruct((n, KV_WIDTH), F32)
    out_shape = (jax.ShapeDtypeStruct((n, 512), BF16), jax.ShapeDtypeStruct((n, 512), F32),
                 kv, kv, kv, kv, kv, kv, q_shape, jax.ShapeDtypeStruct((32, n), F32))
    out_specs = (row(512), row(512), row(128), row(128), row(128), row(128), row(128), row(128), q_spec,
                 pl.BlockSpec((32, tm), lambda i: (0, i)))
    in_specs = [row(dm), full(gmix), full(wstd), full(wt), full(gsgu), full(wsgu), full(bsgu),
                row(128), row(128), pl.BlockSpec((32, tm), lambda i: (0, i)), pl.BlockSpec((32, tm), lambda i: (0, i))]
    return pl.pallas_call(
        functools.partial(_inproj_kernel, tm=tm, emit_qblk=emit_qblk),
        grid=(nt,), in_specs=in_specs, out_specs=out_specs, out_shape=out_shape,
        compiler_params=_cparams(("parallel",), VMEM_LIMIT), name="inproj",
    )(x, gmix, wstd, wt, gsgu, wsgu, bsgu, cos, sin, cost, sint)


def _compress_kernel(c_ref, wbig_ref, pe_ref, w1_ref, w2_ref, o_ref, *, transpose_out):
    c = c_ref[0].astype(BF16)
    ab = _dot(c, wbig_ref[...])
    nch = ab.shape[0]
    hw = N_KV_HEADS * CMP_HIDDEN
    peb = _dot(pe_ref[...], w1_ref[...])
    bias = jnp.concatenate([peb[0:1]] * N_KV_HEADS, axis=1)
    hh = ab[:, :hw] + pltpu.roll(ab[:, hw:], nch - 1, 0) + bias
    g = _gelu(hh).astype(BF16)
    if transpose_out:
        o_ref[0] = _dot_nt(w2_ref[...], g).astype(BF16)
    else:
        o_ref[0] = _dot(g, w2_ref[...]).astype(BF16)


def _compress(c, wbig, pe8, w1, w2, *, transpose_out):
    b, nch, cw = c.shape
    if transpose_out:
        out_shape = jax.ShapeDtypeStruct((b, KV_WIDTH, nch), BF16)
        out_spec = pl.BlockSpec((1, KV_WIDTH, nch), lambda i: (i, 0, 0))
    else:
        out_shape = jax.ShapeDtypeStruct((b, nch, KV_WIDTH), BF16)
        out_spec = pl.BlockSpec((1, nch, KV_WIDTH), lambda i: (i, 0, 0))
    full = lambda a: pl.BlockSpec(a.shape, lambda i: (0,) * a.ndim)
    return pl.pallas_call(
        functools.partial(_compress_kernel, transpose_out=transpose_out),
        grid=(b,), in_specs=[pl.BlockSpec((1, nch, cw), lambda i: (i, 0, 0)), full(wbig), full(pe8), full(w1), full(w2)],
        out_specs=out_spec, out_shape=out_shape,
        compiler_params=_cparams(("parallel",), VMEM_LIMIT), name="compress",
    )(c, wbig, pe8, w1, w2)


def _cmp_select_kernel(q_ref, ck_ref, cvt_ref, mt_ref, t_ref, oc_ref, qaug_ref, sbt_ref, *, tq):
    L = N_HEADS * tq
    hl = L // N_KV_HEADS
    q1 = q_ref[0]
    t = t_ref[0]
    s = _dot(ck_ref[0], q1)
    nch = s.shape[0]
    n_last = (t - (CMP_LEN - 1)) // CMP_STRIDE
    s = jnp.where(lax.broadcasted_iota(I32, (nch, L), 0) <= n_last, s, NEG)
    m = jnp.max(s, axis=0, keepdims=True)
    p = jnp.exp2(s - m)
    inv = jnp.where(m > NEG / 2, 1.0 / jnp.maximum(jnp.sum(p, axis=0, keepdims=True), TINY), 0.0)
    p = p * inv
    pb = p.astype(BF16)
    mt = mt_ref[...]
    nsel = mt.shape[0]
    blk = lax.broadcasted_iota(I32, (nsel, tq), 0).astype(F32)
    cur = (t[:, 0:tq] // SEL_LEN).astype(F32)
    biases = []
    for h in range(N_KV_HEADS):
        o_h = _dot(cvt_ref[0, HEAD_DIM * h:HEAD_DIM * (h + 1), :], pb[:, h * hl:(h + 1) * hl])
        psum = p[:, h * hl:h * hl + tq]
        for g in range(GQA):
            hd = h * GQA + g
            oc_ref[0, HEAD_DIM * hd:HEAD_DIM * (hd + 1), :] = o_h[:, g * tq:(g + 1) * tq]
            if g > 0:
                psum = psum + p[:, hd * tq:(hd + 1) * tq]
        hi, lo = _split_bf16(psum)
        imp = _dot(mt, hi) + _dot(mt, lo)
        forced = (blk == 0.0) | (blk == cur) | (blk == cur - 1.0)
        past = blk <= cur
        sc = jnp.where(past & ~forced, imp, NEG)
        for _ in range(N_SEL - N_FORCED):
            mx = jnp.max(sc, axis=0, keepdims=True)
            idx = jnp.min(jnp.where(sc == mx, blk, 1e9), axis=0, keepdims=True)
            sc = jnp.where(blk == idx, PICKED, sc)
        sel = past & (forced | (sc < PICKED / 2) | (cur < float(N_SEL)))
        bias_h = jnp.where(sel, 0.0, NEG).astype(BF16)
        biases.extend([bias_h] * GQA)
    bias = jnp.concatenate(biases, axis=1)
    qaug_ref[0, 0:KV_WIDTH, :] = q1
    qaug_ref[0, KV_WIDTH:, :] = bias
    if sbt_ref is not None:
        assert L == LANE
        pad = jnp.zeros((LANE - BLOCKS_PER_STEP, L), F32)
        for js in range(sbt_ref.shape[1]):
            grp = bias[js * BLOCKS_PER_STEP:(js + 1) * BLOCKS_PER_STEP, :].astype(F32)
            sbt_ref[0, js] = jnp.concatenate([grp, pad], axis=0).T.astype(BF16)


def _cmp_select(qblk, ck, cvt, mt, tl, *, tq, nq_per_b, n_bias_steps=0):
    nq, _, L = qblk.shape
    b, nch, _ = ck.shape
    nsel = mt.shape[0]
    shared_t = tl.shape[0] == 1
    out_specs = [pl.BlockSpec((1, N_HEADS * HEAD_DIM, tq), lambda i, j: (i * nq_per_b + j, 0, 0)),
                 pl.BlockSpec((1, KV_WIDTH + nsel, L), lambda i, j: (i * nq_per_b + j, 0, 0))]
    out_shape = [jax.ShapeDtypeStruct((nq, N_HEADS * HEAD_DIM, tq), F32),
                 jax.ShapeDtypeStruct((nq, KV_WIDTH + nsel, L), BF16)]
    if n_bias_steps:
        out_specs.append(pl.BlockSpec((1, n_bias_steps, L, LANE), lambda i, j: (i * nq_per_b + j, 0, 0, 0)))
        out_shape.append(jax.ShapeDtypeStruct((nq, n_bias_steps, L, LANE), BF16))
        body = functools.partial(_cmp_select_kernel, tq=tq)
    else:
        body = lambda *refs: _cmp_select_kernel(*refs, None, tq=tq)
    return pl.pallas_call(
        body,
        grid=(b, nq_per_b),
        in_specs=[pl.BlockSpec((1, KV_WIDTH, L), lambda i, j: (i * nq_per_b + j, 0, 0)),
                  pl.BlockSpec((1, nch, KV_WIDTH), lambda i, j: (i, 0, 0)),
                  pl.BlockSpec((1, KV_WIDTH, nch), lambda i, j: (i, 0, 0)),
                  pl.BlockSpec(mt.shape, lambda i, j: (0, 0)),
                  pl.BlockSpec((1, 1, L), (lambda i, j: (0, 0, 0)) if shared_t else (lambda i, j: (j, 0, 0)))],
        out_specs=out_specs, out_shape=out_shape,
        compiler_params=_cparams(("parallel", "parallel"), VMEM_LIMIT), name="cmp_select",
    )(qblk, ck, cvt, mt, tl)


VSUM_ROWS = 16
VT_ROWS = HEAD_DIM + VSUM_ROWS
SEL_TK = 512


def _sel_prompt_kernel(qaug_ref, kaug_ref, vt_ref, t_ref, o_ref, m_sc, acc_sc, sa_sc, sb_sc, *, tq, tk):
    qi = pl.program_id(1)
    L = N_HEADS * tq
    hl = L // N_KV_HEADS
    m_sc[...] = jnp.full(m_sc.shape, NEG, F32)
    acc_sc[...] = jnp.zeros(acc_sc.shape, F32)
    last = (qi * tq + tq - 1) // tk

    def scores(ki, buf):
        buf[...] = _dot(kaug_ref[0, pl.ds(pl.multiple_of(ki * tk, tk), tk), :], qaug_ref[0])

    def consume(ki, buf, causal):
        s = buf[...]
        if causal:
            kpos = ki * tk + lax.broadcasted_iota(I32, s.shape, 0)
            s = jnp.where(kpos <= t_ref[0], s, NEG)
        m_old = m_sc[...]
        m_new = jnp.maximum(m_old, jnp.max(s, axis=0, keepdims=True))
        m_sc[...] = m_new
        alpha = jnp.exp2(m_old - m_new)
        pb = jnp.exp2(s - m_new).astype(BF16)
        for h in range(N_KV_HEADS):
            ls = slice(h * hl, (h + 1) * hl)
            acc_sc[h] = acc_sc[h] * alpha[:, ls] + _dot(vt_ref[ki, h * VT_ROWS:(h + 1) * VT_ROWS, :], pb[:, ls])

    scores(0, sa_sc)

    def body(j, carry):
        scores(2 * j + 1, sb_sc)
        consume(2 * j, sa_sc, False)
        scores(2 * j + 2, sa_sc)
        consume(2 * j + 1, sb_sc, False)
        return carry

    lax.fori_loop(0, last // 2, body, 0)

    @pl.when(last % 2 == 1)
    def _():
        scores(last, sb_sc)
        consume(last - 1, sa_sc, False)
        consume(last, sb_sc, True)

    @pl.when(last % 2 == 0)
    def _():
        consume(last, sa_sc, True)

    for hd in range(N_HEADS):
        h, g = divmod(hd, GQA)
        gs = slice(g * tq, (g + 1) * tq)
        linv = 1.0 / jnp.maximum(acc_sc[h, HEAD_DIM:HEAD_DIM + 1, gs], TINY)
        o_ref[0, HEAD_DIM * hd:HEAD_DIM * (hd + 1), :] = acc_sc[h, 0:HEAD_DIM, gs] * linv


def _sel_prompt(qaug, kaug, vt, tl, *, tq, tk, bsz):
    nq, r, L = qaug.shape
    nqb = nq // bsz
    tlen = kaug.shape[0] // bsz
    return pl.pallas_call(
        functools.partial(_sel_prompt_kernel, tq=tq, tk=tk),
        grid=(bsz, nqb),
        in_specs=[pl.BlockSpec((1, r, L), lambda i, j: (i * nqb + j, 0, 0)),
                  pl.BlockSpec((1, tlen, r), lambda i, j: (i, 0, 0)),
                  pl.BlockSpec((tlen // tk, N_KV_HEADS * VT_ROWS, tk), lambda i, j: (i, 0, 0)),
                  pl.BlockSpec((1, 1, L), lambda i, j: (j, 0, 0))],
        out_specs=pl.BlockSpec((1, N_HEADS * HEAD_DIM, tq), lambda i, j: (i * nqb + j, 0, 0)),
        out_shape=jax.ShapeDtypeStruct((nq, N_HEADS * HEAD_DIM, tq), F32),
        scratch_shapes=[pltpu.VMEM((1, L), F32), pltpu.VMEM((N_KV_HEADS, VT_ROWS, L // N_KV_HEADS), F32),
                        pltpu.VMEM((tk, L), F32), pltpu.VMEM((tk, L), F32)],
        compiler_params=_cparams(("parallel", "parallel"), VMEM_LIMIT), name="sel_prompt",
    )(qaug, kaug.reshape(bsz, tlen, r), vt, tl)


def _sel_prep_kernel(k_ref, v_ref, oh_ref, kaug_ref, vt_ref):
    kaug_ref[:, 0:KV_WIDTH] = k_ref[...].astype(BF16)
    kaug_ref[:, KV_WIDTH:] = oh_ref[...]
    vt = v_ref[...].T.astype(BF16)
    ones = jnp.ones((VSUM_ROWS, vt.shape[1]), BF16)
    for h in range(N_KV_HEADS):
        vt_ref[0, h * VT_ROWS:h * VT_ROWS + HEAD_DIM, :] = vt[h * HEAD_DIM:(h + 1) * HEAD_DIM, :]
        vt_ref[0, h * VT_ROWS + HEAD_DIM:(h + 1) * VT_ROWS, :] = ones


def _sel_prep(k, v, oh, *, tk):
    n = k.shape[0]
    tiles_per_seq = oh.shape[0] // tk
    nsel = oh.shape[1]
    return pl.pallas_call(
        _sel_prep_kernel, grid=(n // tk,),
        in_specs=[pl.BlockSpec((tk, KV_WIDTH), lambda i: (i, 0)), pl.BlockSpec((tk, KV_WIDTH), lambda i: (i, 0)),
                  pl.BlockSpec((tk, nsel), lambda i: (i % tiles_per_seq, 0))],
        out_specs=(pl.BlockSpec((tk, KV_WIDTH + nsel), lambda i: (i, 0)),
                   pl.BlockSpec((1, N_KV_HEADS * VT_ROWS, tk), lambda i: (i, 0, 0))),
        out_shape=(jax.ShapeDtypeStruct((n, KV_WIDTH + nsel), BF16),
                   jax.ShapeDtypeStruct((n // tk, N_KV_HEADS * VT_ROWS, tk), BF16)),
        compiler_params=_cparams(("parallel",)), name="sel_prep",
    )(k, v, oh)


SEL_PAGES = 32
BLOCKS_PER_STEP = 64


def _sel_paged_kernel(pt_ref, q_ref, sbt_ref, sbt_tail_ref, ohc_ref, trow_ref, knew_ref, vnew_ref, *refs, past_len):
    del pt_ref
    g = SEL_PAGES
    kpages, vpages, o_ref, m_sc, l_sc, acc_sc, kcat_sc, vcat_sc = refs[:g], refs[g:2 * g], refs[2 * g], *refs[2 * g + 1:]
    js = pl.program_id(1)
    q = q_ref[0]

    @pl.when(js == 0)
    def _():
        m_sc[...] = jnp.full(m_sc.shape, NEG, F32)
        l_sc[...] = jnp.zeros(l_sc.shape, F32)
        acc_sc[...] = jnp.zeros(acc_sc.shape, F32)

    def update(s, vt):
        m_old = m_sc[...]
        m_new = jnp.maximum(m_old, jnp.max(s, axis=1, keepdims=True))
        alpha = jnp.exp2(m_old - m_new)
        p = jnp.exp2(s - m_new)
        l_sc[...] = alpha * l_sc[...] + jnp.sum(p, axis=1, keepdims=True)
        m_sc[...] = m_new
        acc_sc[...] = acc_sc[...] * alpha + _dot_nt(p.astype(BF16), vt)

    for i in range(g):
        kcat_sc[:, i * LANE:(i + 1) * LANE] = kpages[i][0, 0].astype(BF16)
        vcat_sc[:, i * LANE:(i + 1) * LANE] = vpages[i][0, 0].astype(BF16)
    ohc = ohc_ref[...]
    s = _dot(q, kcat_sc[...]) + _dot(sbt_ref[0, 0], ohc)
    update(s, vcat_sc[...])

    @pl.when(js == pl.num_programs(1) - 1)
    def _():
        sn = _dot(q, knew_ref[0].astype(BF16)) + _dot(sbt_tail_ref[0, 0], ohc[:, 0:LANE])
        kpos = past_len + lax.broadcasted_iota(I32, sn.shape, 1)
        sn = jnp.where(kpos <= trow_ref[...], sn, NEG)
        update(sn, vnew_ref[0].astype(BF16))
        o_ref[0] = acc_sc[...] * (1.0 / jnp.maximum(l_sc[...], TINY))


def _sel_paged(page_table, qstd, sbt, ohc, trow, knew_t, vnew_t, pool_k, pool_v, layer, *, past_len):
    b, rows, _ = qstd.shape
    g = SEL_PAGES
    page = pool_k.shape[3]
    n_steps = page_table.shape[1] // g
    pspec = lambda i: pl.BlockSpec((1, 1, KV_WIDTH, page), lambda bi, j, pt, i=i: (layer, pt[bi, j * g + i], 0, 0))
    per_b = lambda a: pl.BlockSpec((1,) + a.shape[1:], lambda bi, j, pt: (bi,) + (0,) * (a.ndim - 1))
    full = lambda a: pl.BlockSpec(a.shape, lambda bi, j, pt: (0,) * a.ndim)
    grid_spec = pltpu.PrefetchScalarGridSpec(
        num_scalar_prefetch=1, grid=(b, n_steps),
        in_specs=[per_b(qstd),
                  pl.BlockSpec((1, 1) + sbt.shape[2:], lambda bi, j, pt: (bi, j, 0, 0)),
                  pl.BlockSpec((1, 1) + sbt.shape[2:], lambda bi, j, pt: (bi, n_steps, 0, 0)),
                  full(ohc), full(trow), per_b(knew_t), per_b(vnew_t)]
                 + [pspec(i) for i in range(g)] + [pspec(i) for i in range(g)],
        out_specs=pl.BlockSpec((1, rows, KV_WIDTH), lambda bi, j, pt: (bi, 0, 0)),
        scratch_shapes=[pltpu.VMEM((rows, 1), F32), pltpu.VMEM((rows, 1), F32), pltpu.VMEM((rows, KV_WIDTH), F32),
                        pltpu.VMEM((KV_WIDTH, g * page), BF16), pltpu.VMEM((KV_WIDTH, g * page), BF16)])
    return pl.pallas_call(
        functools.partial(_sel_paged_kernel, past_len=past_len),
        grid_spec=grid_spec, out_shape=jax.ShapeDtypeStruct((b, rows, KV_WIDTH), F32),
        compiler_params=_cparams(("parallel", "arbitrary"), VMEM_LIMIT), name="sel_paged",
    )(page_table, qstd, sbt, sbt, ohc, trow, knew_t, vnew_t, *([pool_k] * g), *([pool_v] * g))


def _window_kernel(q_ref, k_ref, v_ref, t_ref, o_ref, *, tq, band, kpos_base):
    qi = pl.program_id(1)
    L = N_HEADS * tq
    hl = L // N_KV_HEADS
    q1 = q_ref[0]
    t = t_ref[0]
    start = pl.multiple_of(jnp.maximum(qi * tq + tq - band, 0), LANE)
    kb = k_ref[0, pl.ds(start, band), :].astype(BF16)
    s = _dot(kb, q1)
    r_hi = t - (kpos_base + start)
    row = lax.broadcasted_iota(I32, s.shape, 0)
    s = jnp.where((row <= r_hi) & (row > r_hi - WINDOW), s, NEG)
    m = jnp.max(s, axis=0, keepdims=True)
    pb = jnp.exp2(s - m).astype(BF16)
    keep = m > NEG / 2
    vt = v_ref[0, pl.ds(start, band), :].T.astype(BF16)
    ones = jnp.ones((VSUM_ROWS, band), BF16)
    for h in range(N_KV_HEADS):
        vth = jnp.concatenate([vt[HEAD_DIM * h:HEAD_DIM * (h + 1), :], ones], axis=0)
        o_h = _dot(vth, pb[:, h * hl:(h + 1) * hl])
        for g in range(GQA):
            hd = h * GQA + g
            gs = slice(g * tq, (g + 1) * tq)
            linv = jnp.where(keep[:, hd * tq:(hd + 1) * tq], 1.0 / jnp.maximum(o_h[HEAD_DIM:HEAD_DIM + 1, gs], TINY), 0.0)
            o_ref[0, HEAD_DIM * hd:HEAD_DIM * (hd + 1), :] = o_h[0:HEAD_DIM, gs] * linv


def _window(qblk, k, v, tl, *, tq, band, kpos_base):
    nq, _, L = qblk.shape
    b, tlen, _ = k.shape
    nqb = nq // b
    shared_t = tl.shape[0] == 1
    return pl.pallas_call(
        functools.partial(_window_kernel, tq=tq, band=band, kpos_base=kpos_base),
        grid=(b, nqb),
        in_specs=[pl.BlockSpec((1, KV_WIDTH, L), lambda i, j: (i * nqb + j, 0, 0)),
                  pl.BlockSpec((1, tlen, KV_WIDTH), lambda i, j: (i, 0, 0)),
                  pl.BlockSpec((1, tlen, KV_WIDTH), lambda i, j: (i, 0, 0)),
                  pl.BlockSpec((1, 1, L), (lambda i, j: (0, 0, 0)) if shared_t else (lambda i, j: (j, 0, 0)))],
        out_specs=pl.BlockSpec((1, N_HEADS * HEAD_DIM, tq), lambda i, j: (i * nqb + j, 0, 0)),
        out_shape=jax.ShapeDtypeStruct((nq, N_HEADS * HEAD_DIM, tq), F32),
        compiler_params=_cparams(("parallel", "parallel"), VMEM_LIMIT), name="window",
    )(qblk, k, v, tl)


def _combine_kernel(x_ref, a_ref, oc_ref, os_ref, ow_ref, gt_ref, e_ref, woa_ref, wob_ref, o_ref, *, tm):
    hi, lo = _split_bf16(gt_ref[...])
    ge = _dot(e_ref[...], hi) + _dot(e_ref[...], lo)
    bw = N_HEADS * HEAD_DIM
    parts = []
    for j in range(tm // LANE):
        ls = slice(j * LANE, (j + 1) * LANE)
        parts.append(ge[0:bw, ls] * oc_ref[j] + ge[bw:2 * bw, ls] * os_ref[j] + ge[2 * bw:3 * bw, ls] * ow_ref[j])
    mixt = parts[0] if len(parts) == 1 else jnp.concatenate(parts, axis=1)
    mix = mixt.T.astype(BF16)
    o_ref[...] = x_ref[...] + _dot(a_ref[...], woa_ref[...]) + _dot(mix, wob_ref[...])


def _combine(x, a, oc, os_, ow, gt, e, woa, wob, *, tm):
    n, dm = x.shape
    bw = N_HEADS * HEAD_DIM
    full = lambda arr: pl.BlockSpec(arr.shape, lambda i: (0,) * arr.ndim)
    ospec = pl.BlockSpec((tm // LANE, bw, LANE), lambda i: (i, 0, 0))
    return pl.pallas_call(
        functools.partial(_combine_kernel, tm=tm),
        grid=(n // tm,),
        in_specs=[pl.BlockSpec((tm, dm), lambda i: (i, 0)), pl.BlockSpec((tm, a.shape[1]), lambda i: (i, 0)),
                  ospec, ospec, ospec, pl.BlockSpec((32, tm), lambda i: (0, i)), full(e), full(woa), full(wob)],
        out_specs=pl.BlockSpec((tm, dm), lambda i: (i, 0)),
        out_shape=jax.ShapeDtypeStruct((n, dm), F32),
        compiler_params=_cparams(("parallel",), VMEM_LIMIT), name="combine",
    )(x, a, oc, os_, ow, gt, e, woa, wob)


def _ffn_dense_kernel(x_ref, g_ref, wg_ref, wu_ref, wd_ref, o_ref):
    xf = x_ref[...]
    h = _rms(xf, g_ref[...]).astype(BF16)
    gate = _dot(h, wg_ref[...])
    up = _dot(h, wu_ref[...])
    act = (gate * _sigmoid(gate) * up).astype(BF16)
    o_ref[...] = xf + _dot(act, wd_ref[...])


def _ffn_dense(x, g, wg, wu, wd, *, tm):
    n, dm = x.shape
    full = lambda arr: pl.BlockSpec(arr.shape, lambda i: (0,) * arr.ndim, pipeline_mode=pl.Buffered(1))
    return pl.pallas_call(
        _ffn_dense_kernel, grid=(n // tm,),
        in_specs=[pl.BlockSpec((tm, dm), lambda i: (i, 0)), full(g), full(wg), full(wu), full(wd)],
        out_specs=pl.BlockSpec((tm, dm), lambda i: (i, 0)),
        out_shape=jax.ShapeDtypeStruct((n, dm), F32),
        compiler_params=_cparams(("parallel",), VMEM_LIMIT), name="ffn_dense",
    )(x, g, wg, wu, wd)


def _router_kernel(x_ref, g_ref, wrh_ref, wrl_ref, tri_ref, h_ref, meta_ref, cnt_ref, *, tm, n_exp):
    h = _rms(x_ref[...], g_ref[...])
    h_ref[...] = h
    hi, lo = _split_bf16(h)
    logits = _dot(hi, wrh_ref[...]) + _dot(lo, wrh_ref[...]) + _dot(hi, wrl_ref[...])
    lane = lax.broadcasted_iota(I32, (tm, LANE), 1)
    lanef = lane.astype(F32)
    logits = jnp.where(lane < n_exp, logits, NEG)
    m1 = jnp.max(logits, axis=1, keepdims=True)
    i1 = jnp.min(jnp.where(logits == m1, lanef, 1e9), axis=1, keepdims=True)
    rest = jnp.where(lanef == i1, -3e38, logits)
    m2 = jnp.max(rest, axis=1, keepdims=True)
    i2 = jnp.min(jnp.where(rest == m2, lanef, 1e9), axis=1, keepdims=True)
    e2 = jnp.exp(m2 - m1)
    w1 = 1.0 / (1.0 + e2)
    w2 = e2 / (1.0 + e2)
    hit1 = lanef == i1
    hit2 = lanef == i2
    msel = jnp.where(hit1 | hit2, 1.0, 0.0)
    ranks = _dot(tri_ref[...], msel.astype(BF16))
    r1 = jnp.sum(jnp.where(hit1, ranks, 0.0), axis=1, keepdims=True)
    r2 = jnp.sum(jnp.where(hit2, ranks, 0.0), axis=1, keepdims=True)
    meta = jnp.where(lane == 0, i1, 0.0)
    for k, val in enumerate((i2, w1, w2, r1, r2)):
        meta = jnp.where(lane == k + 1, val, meta)
    meta_ref[...] = meta
    cnt_ref[0] = jnp.sum(msel, axis=0, keepdims=True)


def _router(x, g, wrh, wrl, tri, *, tm, n_exp):
    n, dm = x.shape
    full = lambda arr: pl.BlockSpec(arr.shape, lambda i: (0,) * arr.ndim)
    return pl.pallas_call(
        functools.partial(_router_kernel, tm=tm, n_exp=n_exp), grid=(n // tm,),
        in_specs=[pl.BlockSpec((tm, dm), lambda i: (i, 0)), full(g), full(wrh), full(wrl), full(tri)],
        out_specs=(pl.BlockSpec((tm, dm), lambda i: (i, 0)), pl.BlockSpec((tm, LANE), lambda i: (i, 0)),
                   pl.BlockSpec((1, 1, LANE), lambda i: (i, 0, 0))),
        out_shape=(jax.ShapeDtypeStruct((n, dm), F32), jax.ShapeDtypeStruct((n, LANE), F32),
                   jax.ShapeDtypeStruct((n // tm, 1, LANE), F32)),
        compiler_params=_cparams(("parallel",), VMEM_LIMIT), name="router",
    )(x, g, wrh, wrl, tri)


def _row_copy(src, src_row, dst, dst_row, sem):
    return pltpu.make_async_copy(src.at[pl.ds(src_row, 1)], dst.at[pl.ds(dst_row, 1)], sem)


def _dispatch_kernel(dest_ref, h_ref, xg_in_ref, xg_ref, sem, *, tm):
    del xg_in_ref

    def issue(r, carry):
        for k in range(TOP_K):
            _row_copy(h_ref, r, xg_ref, dest_ref[TOP_K * r + k], sem).start()
        return carry

    lax.fori_loop(0, tm, issue, 0)

    def drain(r, carry):
        for k in range(TOP_K):
            _row_copy(h_ref, 0, xg_ref, 0, sem).wait()
        return carry

    lax.fori_loop(0, tm, drain, 0)


def _dispatch(dest, h, xg0, *, tm):
    n, dm = h.shape
    return pl.pallas_call(
        functools.partial(_dispatch_kernel, tm=tm), grid=(n // tm,),
        in_specs=[pl.BlockSpec((TOP_K * tm,), lambda i: (i,), memory_space=pltpu.SMEM),
                  pl.BlockSpec((tm, dm), lambda i: (i, 0)),
                  pl.BlockSpec(memory_space=pl.ANY)],
        out_specs=pl.BlockSpec(memory_space=pl.ANY),
        out_shape=jax.ShapeDtypeStruct(xg0.shape, F32),
        scratch_shapes=[pltpu.SemaphoreType.DMA],
        input_output_aliases={2: 0},
        compiler_params=_cparams(("arbitrary",), VMEM_LIMIT), name="moe_dispatch",
    )(dest, h, xg0)


def _expert_ffn_kernel(te_ref, nused_ref, x_ref, wg_ref, wu_ref, wd_ref, o_ref):
    s = pl.program_id(0)

    @pl.when(s < nused_ref[0])
    def _():
        h = x_ref[...].astype(BF16)
        gate = _dot(h, wg_ref[0])
        up = _dot(h, wu_ref[0])
        act = (gate * _sigmoid(gate) * up).astype(BF16)
        o_ref[...] = _dot(act, wd_ref[0])

    @pl.when(s >= nused_ref[0])
    def _():
        o_ref[...] = jnp.zeros(o_ref.shape, F32)


def _expert_ffn(te, nused, xg, wg, wu, wd, *, ts):
    s_rows, dm = xg.shape
    dff = wg.shape[2]
    grid_spec = pltpu.PrefetchScalarGridSpec(
        num_scalar_prefetch=2, grid=(s_rows // ts,),
        in_specs=[pl.BlockSpec((ts, dm), lambda i, te, nu: (i, 0)),
                  pl.BlockSpec((1, dm, dff), lambda i, te, nu: (te[i], 0, 0), pipeline_mode=pl.Buffered(1)),
                  pl.BlockSpec((1, dm, dff), lambda i, te, nu: (te[i], 0, 0), pipeline_mode=pl.Buffered(1)),
                  pl.BlockSpec((1, dff, dm), lambda i, te, nu: (te[i], 0, 0), pipeline_mode=pl.Buffered(1))],
        out_specs=pl.BlockSpec((ts, dm), lambda i, te, nu: (i, 0)))
    return pl.pallas_call(
        _expert_ffn_kernel, grid_spec=grid_spec,
        out_shape=jax.ShapeDtypeStruct((s_rows, dm), F32),
        compiler_params=_cparams(("arbitrary",), VMEM_LIMIT), name="expert_ffn",
    )(te, nused, xg, wg, wu, wd)


def _moe_combine_kernel(dest_ref, x_ref, meta_ref, y_ref, o_ref, buf, sem, *, tm):
    def issue(r, carry):
        for k in range(TOP_K):
            _row_copy(y_ref, dest_ref[TOP_K * r + k], buf.at[k], r, sem).start()
        return carry

    lax.fori_loop(0, tm, issue, 0)

    def drain(r, carry):
        for k in range(TOP_K):
            _row_copy(y_ref, 0, buf.at[k], 0, sem).wait()
        return carry

    lax.fori_loop(0, tm, drain, 0)
    meta = meta_ref[...]
    o_ref[...] = x_ref[...] + meta[:, 2:3] * buf[0] + meta[:, 3:4] * buf[1]


def _moe_combine(dest, x, meta, y, *, tm):
    n, dm = x.shape
    return pl.pallas_call(
        functools.partial(_moe_combine_kernel, tm=tm), grid=(n // tm,),
        in_specs=[pl.BlockSpec((TOP_K * tm,), lambda i: (i,), memory_space=pltpu.SMEM),
                  pl.BlockSpec((tm, dm), lambda i: (i, 0)),
                  pl.BlockSpec((tm, LANE), lambda i: (i, 0)),
                  pl.BlockSpec(memory_space=pl.ANY)],
        out_specs=pl.BlockSpec((tm, dm), lambda i: (i, 0)),
        out_shape=jax.ShapeDtypeStruct((n, dm), F32),
        scratch_shapes=[pltpu.VMEM((TOP_K, tm, dm), F32), pltpu.SemaphoreType.DMA],
        compiler_params=_cparams(("arbitrary",), VMEM_LIMIT), name="moe_combine",
    )(dest, x, meta, y)


def _moe(x, g, wr, wg, wu, wd, *, tm, ts):
    n, dm = x.shape
    n_exp = wr.shape[1]
    wr_pad = jnp.zeros((dm, LANE), F32).at[:, :n_exp].set(wr)
    wrh, wrl = _split_bf16(wr_pad)
    tri = jnp.tril(jnp.ones((tm, tm), F32), -1).astype(BF16)
    h, meta, cnt = _router(x, g, wrh, wrl, tri, tm=tm, n_exp=n_exp)
    cnt = cnt[:, 0, :n_exp].astype(I32)
    tile_off = jnp.cumsum(cnt, axis=0) - cnt
    tot = jnp.sum(cnt, axis=0)
    padded = ((tot + ts - 1) // ts) * ts
    gend = jnp.cumsum(padded)
    gstart = gend - padded
    ei = meta[:, 0:2].astype(I32)
    rk = meta[:, 4:6].astype(I32)
    base = jnp.broadcast_to((gstart[None, :] + tile_off)[:, None, :], (n // tm, tm, n_exp)).reshape(n, 1, n_exp)
    pick = ei[:, :, None] == jnp.arange(n_exp, dtype=I32)[None, None, :]
    dest = (jnp.sum(jnp.where(pick, base, 0), axis=-1) + rk).reshape(-1)
    n_slot_tiles = -(-(TOP_K * n + n_exp * (ts - 1)) // ts)
    n_slots = n_slot_tiles * ts
    nused = (gend[-1] // ts).astype(I32).reshape(1)
    tile_start = jnp.minimum(jnp.arange(n_slot_tiles, dtype=I32), jnp.maximum(nused[0] - 1, 0)) * ts
    te = jnp.minimum(jnp.sum((gend[None, :] <= tile_start[:, None]).astype(I32), axis=1), n_exp - 1)
    xg = _dispatch(dest, h, jnp.zeros((n_slots, dm), F32), tm=tm)
    y = _expert_ffn(te, nused, xg, wg, wu, wd, ts=ts)
    return _moe_combine(dest, x, meta, y, tm=tm)


def _norm_kernel(x_ref, g_ref, o_ref):
    o_ref[...] = _rms(x_ref[...], g_ref[...])


def _final_norm(x, g, *, tm):
    n, dm = x.shape
    return pl.pallas_call(
        _norm_kernel, grid=(n // tm,),
        in_specs=[pl.BlockSpec((tm, dm), lambda i: (i, 0)), pl.BlockSpec(g.shape, lambda i: (0, 0))],
        out_specs=pl.BlockSpec((tm, dm), lambda i: (i, 0)),
        out_shape=jax.ShapeDtypeStruct((n, dm), F32),
        compiler_params=_cparams(("parallel",)), name="final_norm",
    )(x, g)


PAGES_PER_STEP = 16


def _page_gather_kernel(pt_ref, *refs, n_chunked):
    del pt_ref
    g = PAGES_PER_STEP
    n_arr = (len(refs) - 1) // (g + 1)
    pages, outs, scr = refs[:n_arr * g], refs[n_arr * g:n_arr * (g + 1)], refs[-1]
    for c in range(n_arr):
        for i in range(g):
            p = pages[c * g + i][0, 0].T
            npos = p.shape[0]
            if c < n_chunked:
                scr[...] = p
                rows = npos // CMP_STRIDE
                for s in range(CMP_STRIDE):
                    outs[c][0, i * rows:(i + 1) * rows, s * KV_WIDTH:(s + 1) * KV_WIDTH] = scr[pl.ds(s, rows, stride=CMP_STRIDE), :]
            else:
                outs[c][0, i * npos:(i + 1) * npos, :] = p


def _page_gather(page_table, pools_t, layer, *, n_chunked):
    b, n_pages = page_table.shape
    page = pools_t[0].shape[3]
    g = PAGES_PER_STEP
    n_arr = len(pools_t)
    in_specs, operands = [], []
    for p in pools_t:
        for i in range(g):
            in_specs.append(pl.BlockSpec((1, 1, KV_WIDTH, page), lambda bi, j, pt, i=i: (layer, pt[bi, j * g + i], 0, 0)))
            operands.append(p)
    rows = page // CMP_STRIDE
    out_specs, out_shape = [], []
    for c in range(n_arr):
        if c < n_chunked:
            out_specs.append(pl.BlockSpec((1, g * rows, CMP_STRIDE * KV_WIDTH), lambda bi, j, pt: (bi, j, 0)))
            out_shape.append(jax.ShapeDtypeStruct((b, n_pages * rows, CMP_STRIDE * KV_WIDTH), F32))
        else:
            out_specs.append(pl.BlockSpec((1, g * page, KV_WIDTH), lambda bi, j, pt: (bi, j, 0)))
            out_shape.append(jax.ShapeDtypeStruct((b, n_pages * page, KV_WIDTH), F32))
    grid_spec = pltpu.PrefetchScalarGridSpec(
        num_scalar_prefetch=1, grid=(b, n_pages // g), in_specs=in_specs, out_specs=out_specs,
        scratch_shapes=[pltpu.VMEM((page, KV_WIDTH), F32)])
    return pl.pallas_call(
        functools.partial(_page_gather_kernel, n_chunked=n_chunked),
        grid_spec=grid_spec, out_shape=out_shape,
        compiler_params=_cparams(("parallel", "parallel"), VMEM_LIMIT), name="page_gather",
    )(page_table, *operands)


def _rope_tables(pos):
    half = HEAD_DIM // 2
    inv = ROPE_THETA ** (-jnp.arange(half, dtype=F32) / half)
    ang = pos.astype(F32)[:, None] * inv[None, :]
    cos, sin = jnp.cos(ang), jnp.sin(ang)
    cos_std = jnp.tile(cos, (1, 4))
    sin_std = jnp.tile(jnp.concatenate([-sin, sin], axis=1), (1, 2))
    return cos_std, sin_std, cos.T, sin.T


def _cmp_to_sel_t(n_cmp_pad, n_sel_pad):
    i0 = np.arange(n_cmp_pad, dtype=np.int64)[None, :] * CMP_STRIDE
    j0 = np.arange(n_sel_pad, dtype=np.int64)[:, None] * SEL_LEN
    ov = np.clip(np.minimum(i0 + CMP_LEN, j0 + SEL_LEN) - np.maximum(i0, j0), 0, CMP_LEN)
    return jnp.asarray(ov.astype(np.float32) / CMP_LEN, dtype=BF16)


def _block_onehot(n_keys, n_sel_pad):
    blk = np.arange(n_keys)[:, None] // SEL_LEN
    return jnp.asarray((blk == np.arange(n_sel_pad)[None, :]).astype(np.float32), dtype=BF16)


def _lane_positions(pos_tiles, tq):
    return jnp.tile(pos_tiles.astype(I32), (1, N_HEADS))[:, None, :]


def _compress_weights(w1, w2, pe):
    nr = CMP_LEN // CMP_STRIDE
    w1r = w1.reshape(nr, CMP_STRIDE, HEAD_DIM, CMP_HIDDEN)
    eye = jnp.eye(N_KV_HEADS, dtype=F32)
    wbig = jnp.einsum("rsde,gh->sgdrhe", w1r, eye)
    wbig = wbig.reshape(CMP_STRIDE * KV_WIDTH, nr * N_KV_HEADS * CMP_HIDDEN).astype(BF16)
    w2big = jnp.einsum("ed,gh->gehd", w2, eye).reshape(N_KV_HEADS * CMP_HIDDEN, KV_WIDTH).astype(BF16)
    pe8 = jnp.tile(pe.reshape(1, CMP_LEN * HEAD_DIM), (8, 1)).astype(BF16)
    return wbig, pe8, w1.astype(BF16), w2big


def _layer_weights(l, g_mix, w_in, g_sgu, w_sgu, b_sgu, w_cmpk1, w_cmpk2, pe_cmpk, w_cmpv1, w_cmpv2, pe_cmpv, w_o, dec_t):
    aw = A_GROUPS * CHUNK
    bw = N_HEADS * HEAD_DIM
    w = w_in[l]
    wstd = jnp.concatenate([w[:, :2 * aw], w[:, 2 * aw + bw:2 * aw + bw + 6 * KV_WIDTH]], axis=1).astype(BF16)
    n_gate = 3 * N_HEADS
    wt = jnp.concatenate([w[:, 2 * aw:2 * aw + bw], w[:, -n_gate:], jnp.zeros((w.shape[0], 32 - n_gate), F32)], axis=1).T.astype(BF16)
    causal = jnp.tril(jnp.ones((CHUNK, CHUNK), bool))
    wsgu_p = jnp.where(causal[None], w_sgu[l], 0.0).astype(BF16)
    bsgu_p = jnp.broadcast_to(b_sgu[l][:, :, None], (A_GROUPS, CHUNK, CHUNK)).astype(F32)
    reps = CHUNK // dec_t
    wsmall = jnp.where(causal[None, :dec_t, :dec_t], w_sgu[l][:, :dec_t, :dec_t], 0.0)
    wsgu_s = jnp.einsum("ab,gts->gatbs", jnp.eye(reps, dtype=F32), wsmall).reshape(A_GROUPS, CHUNK, CHUNK).astype(BF16)
    bsgu_s = jnp.broadcast_to(jnp.tile(b_sgu[l][:, :dec_t], (1, reps))[:, :, None], (A_GROUPS, CHUNK, CHUNK)).astype(F32)
    ck_w = _compress_weights(w_cmpk1[l], w_cmpk2[l], pe_cmpk[l])
    cv_w = _compress_weights(w_cmpv1[l], w_cmpv2[l], pe_cmpv[l])
    cv_w = cv_w[:3] + (cv_w[3].T,)
    e = np.zeros((3 * bw, 32), np.float32)
    for br in range(3):
        for hd in range(N_HEADS):
            e[br * bw + hd * HEAD_DIM:br * bw + (hd + 1) * HEAD_DIM, br * N_HEADS + hd] = 1.0
    return dict(gmix=g_mix[l][None, :], wstd=wstd, wt=wt, gsgu=g_sgu[l][None, :],
                wsgu_p=wsgu_p, bsgu_p=bsgu_p, wsgu_s=wsgu_s, bsgu_s=bsgu_s, ck_w=ck_w, cv_w=cv_w,
                e=jnp.asarray(e, dtype=BF16), woa=w_o[l][:aw].astype(BF16), wob=w_o[l][aw:].astype(BF16))


def _channel_mixer(l, x, g_ffn, dense_w, w_router, moe_w, *, tm_dense, tm_moe):
    g = g_ffn[l][None, :]
    i = l // 2
    if l % 2 == 0:
        return _ffn_dense(x, g, *(w[i] for w in dense_w), tm=tm_dense)
    return _moe(x, g, w_router[i], *(w[i] for w in moe_w), tm=tm_moe, ts=512)


def kernel(x_prompt, x_sample, cache_cmp_k, cache_cmp_v, cache_sel_k, cache_sel_v, cache_win_k, cache_win_v, page_table,
           g_mix, w_in, g_sgu, w_sgu, b_sgu, w_cmpk1, w_cmpk2, pe_cmpk, w_cmpv1, w_cmpv2, pe_cmpv, w_o, g_ffn,
           w_ff_gate, w_ff_up, w_ff_down, w_router, w_moe_gate, w_moe_up, w_moe_down, g_final):
    bsz, seq, dm = x_prompt.shape
    dec_b, dec_t, _ = x_sample.shape
    depth = g_mix.shape[0]
    page = cache_cmp_k.shape[2]
    past_len = page_table.shape[1] * page
    wbuf = cache_win_k.shape[2]
    n_p, n_s = bsz * seq, dec_b * dec_t
    assert n_s == CHUNK and seq % 512 == 0 and wbuf == WINDOW and past_len % 2048 == 0
    tq_p, tq_s = LANE, LANE // N_HEADS
    nq_b = seq // tq_p
    band = WINDOW + tq_p

    pos_p = jnp.tile(jnp.arange(seq, dtype=I32), bsz)
    pos_s = jnp.tile(past_len + jnp.arange(dec_t, dtype=I32), dec_b)
    rope_p = _rope_tables(pos_p)
    rope_s = _rope_tables(pos_s)
    tl_p = _lane_positions(jnp.arange(seq, dtype=I32).reshape(nq_b, tq_p), tq_p)
    tok_s = past_len + jnp.minimum(jnp.arange(tq_s, dtype=I32), dec_t - 1)
    tl_s = _lane_positions(tok_s[None, :], tq_s)
    nch_p = seq // CMP_STRIDE
    nsel_p = seq // SEL_LEN
    nch_s = past_len // CMP_STRIDE
    nsel_s = -(-(past_len // SEL_LEN + 1) // LANE) * LANE
    mt_p = _cmp_to_sel_t(nch_p, nsel_p)
    mt_s = _cmp_to_sel_t(nch_s, nsel_s)
    oh_p = _block_onehot(seq, nsel_p)
    n_sel_steps = past_len // (SEL_PAGES * page) + 1
    assert nsel_s >= n_sel_steps * BLOCKS_PER_STEP and SEL_PAGES * page == BLOCKS_PER_STEP * SEL_LEN
    ohc_s = _block_onehot(SEL_PAGES * page, LANE).T
    trow_s = jnp.broadcast_to(jnp.tile(tok_s, N_HEADS)[:, None], (N_HEADS * tq_s, LANE)).astype(I32)
    head_of_row = jnp.asarray((np.arange(N_HEADS)[None, :] // GQA == np.arange(N_KV_HEADS)[:, None]).astype(np.float32))

    pools_t = [jnp.transpose(c, (0, 1, 3, 4, 2)).reshape(depth, c.shape[1], KV_WIDTH, page)
               for c in (cache_cmp_k, cache_cmp_v, cache_sel_k, cache_sel_v)]
    flat = lambda a: a.reshape(a.shape[0], a.shape[1], KV_WIDTH)

    dense_w = tuple(w.astype(BF16) for w in (w_ff_gate, w_ff_up, w_ff_down))
    moe_w = tuple(w.astype(BF16) for w in (w_moe_gate, w_moe_up, w_moe_down))
    xp = x_prompt.reshape(n_p, dm)
    xs = x_sample.reshape(n_s, dm)
    outs = [[] for _ in range(13)]
    for l in range(depth):
        lw = _layer_weights(l, g_mix, w_in, g_sgu, w_sgu, b_sgu, w_cmpk1, w_cmpk2, pe_cmpk, w_cmpv1, w_cmpv2, pe_cmpv, w_o, dec_t)
        a, _, kc, vc, ks, vs, kw, vw, qblk, gt = _inproj(
            xp, lw["gmix"], lw["wstd"], lw["wt"], lw["gsgu"], lw["wsgu_p"], lw["bsgu_p"], *rope_p, tm=512, emit_qblk=True)
        ck = _compress(kc.reshape(bsz, nch_p, CMP_STRIDE * KV_WIDTH), *lw["ck_w"], transpose_out=False)
        cvt = _compress(vc.reshape(bsz, nch_p, CMP_STRIDE * KV_WIDTH), *lw["cv_w"], transpose_out=True)
        oc, qaug = _cmp_select(qblk, ck, cvt, mt_p, tl_p, tq=tq_p, nq_per_b=nq_b)
        kaug, vst = _sel_prep(ks, vs, oh_p, tk=SEL_TK)
        osel = _sel_prompt(qaug, kaug, vst, tl_p, tq=tq_p, tk=SEL_TK, bsz=bsz)
        ow = _window(qblk, kw.reshape(bsz, seq, KV_WIDTH), vw.reshape(bsz, seq, KV_WIDTH), tl_p, tq=tq_p, band=band, kpos_base=0)
        xp = _combine(xp, a, oc, osel, ow, gt, lw["e"], lw["woa"], lw["wob"], tm=512)
        kv4 = lambda t: t.reshape(bsz, seq, N_KV_HEADS, HEAD_DIM)
        for idx, t in enumerate((kc, vc, ks, vs)):
            outs[idx].append(kv4(t))
        nwin_p = min(WINDOW, seq)
        outs[4].append(kv4(kw)[:, seq - nwin_p:])
        outs[5].append(kv4(vw)[:, seq - nwin_p:])

        a, v_s, kc, vc, ks, vs, kw, vw, qt, gt = _inproj(
            xs, lw["gmix"], lw["wstd"], lw["wt"], lw["gsgu"], lw["wsgu_s"], lw["bsgu_s"], *rope_s, tm=CHUNK, emit_qblk=False)
        q4 = qt.reshape(N_HEADS, HEAD_DIM, dec_b, dec_t)
        q4 = jnp.pad(q4, ((0, 0), (0, 0), (0, 0), (0, tq_s - dec_t)))
        qb = jnp.einsum("hdbt,gh->bgdht", q4.astype(F32), head_of_row).astype(BF16)
        qblk_s = qb.reshape(dec_b, KV_WIDTH, N_HEADS * tq_s)
        past_ck, past_cv = _page_gather(page_table, pools_t[:2], l, n_chunked=2)
        ck = _compress(past_ck, *lw["ck_w"], transpose_out=False)
        cvt = _compress(past_cv, *lw["cv_w"], transpose_out=True)
        oc, _, sbt = _cmp_select(qblk_s, ck, cvt, mt_s, tl_s, tq=tq_s, nq_per_b=1, n_bias_steps=n_sel_steps)
        new_t = lambda t: jnp.pad(t.reshape(dec_b, dec_t, KV_WIDTH).transpose(0, 2, 1), ((0, 0), (0, 0), (0, page - dec_t)))
        osel_rows = _sel_paged(page_table, jnp.swapaxes(qblk_s, 1, 2), sbt, ohc_s, trow_s, new_t(ks), new_t(vs),
                               pools_t[2], pools_t[3], l, past_len=past_len)
        osel = jnp.einsum("bhtgd,gh->bhdt", osel_rows.reshape(dec_b, N_HEADS, tq_s, N_KV_HEADS, HEAD_DIM),
                          head_of_row).reshape(dec_b, N_HEADS * HEAD_DIM, tq_s)
        kw_all = jnp.concatenate([flat(cache_win_k[l]), kw.reshape(dec_b, dec_t, KV_WIDTH)], axis=1)
        vw_all = jnp.concatenate([flat(cache_win_v[l]), vw.reshape(dec_b, dec_t, KV_WIDTH)], axis=1)
        wpad = ((0, 0), (0, band - wbuf - dec_t), (0, 0))
        ow = _window(qblk_s, jnp.pad(kw_all, wpad), jnp.pad(vw_all, wpad), tl_s, tq=tq_s, band=band, kpos_base=past_len - wbuf)
        untile = lambda o: o[:, :, :dec_t].transpose(1, 0, 2).reshape(1, N_HEADS * HEAD_DIM, n_s)
        xs = _combine(xs, a, untile(oc), untile(osel), untile(ow), gt, lw["e"], lw["woa"], lw["wob"], tm=CHUNK)
        kv4s = lambda t: t.reshape(dec_b, dec_t, N_KV_HEADS, HEAD_DIM)
        for idx, t in enumerate((kc, vc, ks, vs)):
            outs[6 + idx].append(kv4s(t))
        nwin_s = min(WINDOW, wbuf + dec_t)
        outs[10].append(kw_all[:, wbuf + dec_t - nwin_s:].reshape(dec_b, nwin_s, N_KV_HEADS, HEAD_DIM))
        outs[11].append(vw_all[:, wbuf + dec_t - nwin_s:].reshape(dec_b, nwin_s, N_KV_HEADS, HEAD_DIM))
        outs[12].append(v_s.reshape(dec_b, dec_t, A_GROUPS * CHUNK))

        xp = _channel_mixer(l, xp, g_ffn, dense_w, w_router, moe_w, tm_dense=512, tm_moe=512)
        xs = _channel_mixer(l, xs, g_ffn, dense_w, w_router, moe_w, tm_dense=CHUNK, tm_moe=CHUNK)

    gf = g_final[None, :]
    y_prompt = _final_norm(xp, gf, tm=512).reshape(bsz, seq, dm)
    y_sample = _final_norm(xs, gf, tm=CHUNK).reshape(dec_b, dec_t, dm)
    return (y_prompt, y_sample) + tuple(jnp.stack(o, axis=0) for o in outs)
```

```python
import functools
import math

import numpy as np
import jax
import jax.numpy as jnp
from jax import lax
from jax.experimental import pallas as pl
from jax.experimental.pallas import tpu as pltpu

F32 = jnp.float32
BF16 = jnp.bfloat16
I32 = jnp.int32

A_GROUPS = 4
N_HEADS = 8
N_KV_HEADS = 2
HEAD_DIM = 64
CMP_LEN = 32
CMP_STRIDE = 16
CMP_HIDDEN = 128
SEL_LEN = 64
N_SEL = 16
WINDOW = 512
CHUNK = 128
ROPE_THETA = 10000.0
TOP_K = 2
EPS = 1e-6
NEG = -1e30
BIG = 1e30
TINY = 1e-30
PICKED = -3e38
N_FORCED = 3
SCALE = HEAD_DIM ** -0.5
QSCALE = SCALE * math.log2(math.e)

LANE = 128
KV_WIDTH = N_KV_HEADS * HEAD_DIM
GQA = N_HEADS // N_KV_HEADS
VMEM_LIMIT = 56 * 1024 * 1024


def _cparams(sem, vmem=None):
    return pltpu.CompilerParams(dimension_semantics=sem, vmem_limit_bytes=vmem)


def _rms(xf, g):
    return xf * lax.rsqrt(jnp.mean(xf * xf, axis=-1, keepdims=True) + EPS) * g


def _gelu(x):
    c = math.sqrt(2.0 / math.pi)
    return x * (0.5 * (1.0 + jnp.tanh(c * (x + 0.044715 * (x * x * x)))))


def _sigmoid(x):
    return 1.0 / (1.0 + jnp.exp(-x))


def _dot(a, b):
    return jnp.dot(a, b, preferred_element_type=F32)


def _dot_nt(a, b):
    return lax.dot_general(a, b, (((1,), (1,)), ((), ())), preferred_element_type=F32)


def _split_bf16(x):
    hi = x.astype(BF16)
    lo = (x - hi.astype(F32)).astype(BF16)
    return hi, lo


def _inproj_kernel(x_ref, gmix_ref, wstd_ref, wt_ref, gsgu_ref, wsgu_ref, bsgu_ref,
                   cos_ref, sin_ref, cost_ref, sint_ref,
                   a_ref, v_ref, kc_ref, vc_ref, ks_ref, vs_ref, kw_ref, vw_ref, q_ref, gt_ref,
                   *, tm, emit_qblk):
    xf = x_ref[...]
    h = _rms(xf, gmix_ref[...]).astype(BF16)
    z = _dot(h, wstd_ref[...])
    zt = _dot_nt(wt_ref[...], h)
    aw = A_GROUPS * CHUNK
    u = _gelu(z[:, 0:aw])
    vv = _gelu(z[:, aw:2 * aw])
    mu = jnp.mean(vv, axis=-1, keepdims=True)
    d = vv - mu
    var = jnp.mean(d * d, axis=-1, keepdims=True)
    v = d * lax.rsqrt(var + EPS) * gsgu_ref[...]
    v_ref[...] = v
    vb = v.astype(BF16)
    nc = tm // CHUNK
    for g in range(A_GROUPS):
        gs = slice(g * CHUNK, (g + 1) * CHUNK)
        parts = [vb[c * CHUNK:(c + 1) * CHUNK, gs] for c in range(nc)]
        xg = parts[0] if nc == 1 else jnp.concatenate(parts, axis=1)
        yg = _dot(wsgu_ref[g], xg)
        for c in range(nc):
            cs = slice(c * CHUNK, (c + 1) * CHUNK)
            mixed = yg[:, cs] + bsgu_ref[g]
            a_ref[cs, gs] = (u[cs, gs] * mixed).astype(BF16)

    cosr = cos_ref[...]
    sinr = sin_ref[...]
    lane = lax.broadcasted_iota(I32, (tm, KV_WIDTH), 1)
    first = (lane % HEAD_DIM) < (HEAD_DIM // 2)

    def rope(x):
        rot = jnp.where(first, pltpu.roll(x, KV_WIDTH - HEAD_DIM // 2, 1), pltpu.roll(x, HEAD_DIM // 2, 1))
        return x * cosr + rot * sinr

    o = 2 * aw
    kc_ref[...] = rope(z[:, o:o + 128])
    vc_ref[...] = z[:, o + 128:o + 256]
    ks_ref[...] = rope(z[:, o + 256:o + 384])
    vs_ref[...] = z[:, o + 384:o + 512]
    kw_ref[...] = rope(z[:, o + 512:o + 640])
    vw_ref[...] = z[:, o + 640:o + 768]

    ct = cost_ref[...]
    st = sint_ref[...]
    half = HEAD_DIM // 2
    for hd in range(N_HEADS):
        x1 = zt[HEAD_DIM * hd:HEAD_DIM * hd + half]
        x2 = zt[HEAD_DIM * hd + half:HEAD_DIM * (hd + 1)]
        qh = jnp.concatenate([(x1 * ct - x2 * st) * QSCALE, (x2 * ct + x1 * st) * QSCALE], axis=0).astype(BF16)
        if emit_qblk:
            kvh = hd // GQA
            zero = jnp.zeros((HEAD_DIM, LANE), BF16)
            for j in range(tm // LANE):
                ls = slice(hd * LANE, (hd + 1) * LANE)
                q_ref[j, HEAD_DIM * kvh:HEAD_DIM * (kvh + 1), ls] = qh[:, j * LANE:(j + 1) * LANE]
                q_ref[j, HEAD_DIM * (1 - kvh):HEAD_DIM * (2 - kvh), ls] = zero
        else:
            q_ref[HEAD_DIM * hd:HEAD_DIM * (hd + 1), :] = qh
    nq = N_HEADS * HEAD_DIM
    gt_ref[...] = _sigmoid(zt[nq:nq + 32])


def _inproj(x, gmix, wstd, wt, gsgu, wsgu, bsgu, cos, sin, cost, sint, *, tm, emit_qblk):
    n, dm = x.shape
    nt = n // tm
    row = lambda w: pl.BlockSpec((tm, w), lambda i: (i, 0))
    full = lambda a: pl.BlockSpec(a.shape, lambda i: (0,) * a.ndim)
    if emit_qblk:
        q_shape = jax.ShapeDtypeStruct((n // LANE, KV_WIDTH, N_HEADS * LANE), BF16)
        q_spec = pl.BlockSpec((tm // LANE, KV_WIDTH, N_HEADS * LANE), lambda i: (i, 0, 0))
    else:
        q_shape = jax.ShapeDtypeStruct((N_HEADS * HEAD_DIM, n), BF16)
        q_spec = pl.BlockSpec((N_HEADS * HEAD_DIM, tm), lambda i: (0, i))
    kv = jax.ShapeDtypeStruct((n, KV_WIDTH), F32)
    out_shape = (jax.ShapeDtypeStruct((n, 512), BF16), jax.ShapeDtypeStruct((n, 512), F32),
                 kv, kv, kv, kv, kv, kv, q_shape, jax.ShapeDtypeStruct((32, n), F32))
    out_specs = (row(512), row(512), row(128), row(128), row(128), row(128), row(128), row(128), q_spec,
                 pl.BlockSpec((32, tm), lambda i: (0, i)))
    in_specs = [row(dm), full(gmix), full(wstd), full(wt), full(gsgu), full(wsgu), full(bsgu),
                row(128), row(128), pl.BlockSpec((32, tm), lambda i: (0, i)), pl.BlockSpec((32, tm), lambda i: (0, i))]
    return pl.pallas_call(
        functools.partial(_inproj_kernel, tm=tm, emit_qblk=emit_qblk),
        grid=(nt,), in_specs=in_specs, out_specs=out_specs, out_shape=out_shape,
        compiler_params=_cparams(("parallel",), VMEM_LIMIT), name="inproj",
    )(x, gmix, wstd, wt, gsgu, wsgu, bsgu, cos, sin, cost, sint)


def _compress_kernel(c_ref, wbig_ref, pe_ref, w1_ref, w2_ref, o_ref, *, transpose_out):
    c = c_ref[0].astype(BF16)
    ab = _dot(c, wbig_ref[...])
    nch = ab.shape[0]
    hw = N_KV_HEADS * CMP_HIDDEN
    peb = _dot(pe_ref[...], w1_ref[...])
    bias = jnp.concatenate([peb[0:1]] * N_KV_HEADS, axis=1)
    hh = ab[:, :hw] + pltpu.roll(ab[:, hw:], nch - 1, 0) + bias
    g = _gelu(hh).astype(BF16)
    if transpose_out:
        o_ref[0] = _dot_nt(w2_ref[...], g).astype(BF16)
    else:
        o_ref[0] = _dot(g, w2_ref[...]).astype(BF16)


def _compress(c, wbig, pe8, w1, w2, *, transpose_out):
    b, nch, cw = c.shape
    if transpose_out:
        out_shape = jax.ShapeDtypeStruct((b, KV_WIDTH, nch), BF16)
        out_spec = pl.BlockSpec((1, KV_WIDTH, nch), lambda i: (i, 0, 0))
    else:
        out_shape = jax.ShapeDtypeStruct((b, nch, KV_WIDTH), BF16)
        out_spec = pl.BlockSpec((1, nch, KV_WIDTH), lambda i: (i, 0, 0))
    full = lambda a: pl.BlockSpec(a.shape, lambda i: (0,) * a.ndim)
    return pl.pallas_call(
        functools.partial(_compress_kernel, transpose_out=transpose_out),
        grid=(b,), in_specs=[pl.BlockSpec((1, nch, cw), lambda i: (i, 0, 0)), full(wbig), full(pe8), full(w1), full(w2)],
        out_specs=out_spec, out_shape=out_shape,
        compiler_params=_cparams(("parallel",), VMEM_LIMIT), name="compress",
    )(c, wbig, pe8, w1, w2)


CMP_PAGES = 64


def _compress_paged_kernel(pt_ref, wbig_ref, pe_ref, w1_ref, w2_ref, *refs, transpose_out):
    del pt_ref
    g = len(refs) - 3
    pages, nxt, o_ref, scr = refs[:g], refs[g], refs[g + 1], refs[g + 2]
    page = scr.shape[0] // (g + 1)
    for i in range(g):
        scr[i * page:(i + 1) * page, :] = pages[i][0, 0].T
    scr[g * page:(g + 1) * page, :] = nxt[0, 0].T
    nchs = g * page // CMP_STRIDE
    rows = nchs + 8
    ab = None
    half = CMP_STRIDE // 2
    for s in range(half):
        a = jnp.concatenate([scr[pl.ds(s, rows, stride=CMP_STRIDE), :], scr[pl.ds(s + half, rows, stride=CMP_STRIDE), :]],
                            axis=1).astype(BF16)
        d = _dot(a, wbig_ref[s])
        ab = d if ab is None else ab + d
    hw = N_KV_HEADS * CMP_HIDDEN
    peb = _dot(pe_ref[...], w1_ref[...])
    bias = jnp.concatenate([peb[0:1]] * N_KV_HEADS, axis=1)
    hh = ab[0:nchs, :hw] + pltpu.roll(ab[:, hw:], rows - 1, 0)[0:nchs] + bias
    gg = _gelu(hh).astype(BF16)
    if transpose_out:
        o_ref[0] = _dot_nt(w2_ref[...], gg).astype(BF16)
    else:
        o_ref[0] = _dot(gg, w2_ref[...]).astype(BF16)


def _compress_paged(page_table, pool_t, layer, wbig, pe8, w1, w2, *, transpose_out):
    b, n_pages = page_table.shape
    page = pool_t.shape[3]
    g = min(CMP_PAGES, n_pages)
    n_steps = n_pages // g
    nchs = g * page // CMP_STRIDE
    wbig3 = wbig.reshape(CMP_STRIDE, KV_WIDTH, wbig.shape[1])
    wbig3 = jnp.concatenate([wbig3[:CMP_STRIDE // 2], wbig3[CMP_STRIDE // 2:]], axis=1)
    pspec = lambda i: pl.BlockSpec((1, 1, KV_WIDTH, page), lambda bi, j, pt, i=i: (layer, pt[bi, j * g + i], 0, 0))
    nspec = pl.BlockSpec((1, 1, KV_WIDTH, page),
                         lambda bi, j, pt: (layer, pt[bi, jnp.minimum(j * g + g, n_pages - 1)], 0, 0))
    full = lambda a: pl.BlockSpec(a.shape, lambda bi, j, pt: (0,) * a.ndim)
    if transpose_out:
        out_shape = jax.ShapeDtypeStruct((b, KV_WIDTH, n_steps * nchs), BF16)
        out_spec = pl.BlockSpec((1, KV_WIDTH, nchs), lambda bi, j, pt: (bi, 0, j))
    else:
        out_shape = jax.ShapeDtypeStruct((b, n_steps * nchs, KV_WIDTH), BF16)
        out_spec = pl.BlockSpec((1, nchs, KV_WIDTH), lambda bi, j, pt: (bi, j, 0))
    grid_spec = pltpu.PrefetchScalarGridSpec(
        num_scalar_prefetch=1, grid=(b, n_steps),
        in_specs=[full(wbig3), full(pe8), full(w1), full(w2)] + [pspec(i) for i in range(g)] + [nspec],
        out_specs=out_spec, scratch_shapes=[pltpu.VMEM(((g + 1) * page, KV_WIDTH), F32)])
    return pl.pallas_call(
        functools.partial(_compress_paged_kernel, transpose_out=transpose_out),
        grid_spec=grid_spec, out_shape=out_shape,
        compiler_params=_cparams(("parallel", "parallel"), VMEM_LIMIT), name="compress_paged",
    )(page_table, wbig3, pe8, w1, w2, *([pool_t] * (g + 1)))


def _cmp_select_kernel(q_ref, ck_ref, cvt_ref, mt_ref, t_ref, oc_ref, qaug_ref, sbt_ref, *, tq):
    L = N_HEADS * tq
    hl = L // N_KV_HEADS
    q1 = q_ref[0]
    t = t_ref[0]
    s = _dot(ck_ref[0], q1)
    nch = s.shape[0]
    n_last = (t - (CMP_LEN - 1)) // CMP_STRIDE
    s = jnp.where(lax.broadcasted_iota(I32, (nch, L), 0) <= n_last, s, NEG)
    m = jnp.max(s, axis=0, keepdims=True)
    p = jnp.exp2(s - m)
    inv = jnp.where(m > NEG / 2, 1.0 / jnp.maximum(jnp.sum(p, axis=0, keepdims=True), TINY), 0.0)
    p = p * inv
    pb = p.astype(BF16)
    mt = mt_ref[...]
    nsel = mt.shape[0]
    blk = lax.broadcasted_iota(I32, (nsel, tq), 0).astype(F32)
    cur = (t[:, 0:tq] // SEL_LEN).astype(F32)
    biases = []
    for h in range(N_KV_HEADS):
        o_h = _dot(cvt_ref[0, HEAD_DIM * h:HEAD_DIM * (h + 1), :], pb[:, h * hl:(h + 1) * hl])
        psum = p[:, h * hl:h * hl + tq]
        for g in range(GQA):
            hd = h * GQA + g
            oc_ref[0, HEAD_DIM * hd:HEAD_DIM * (hd + 1), :] = o_h[:, g * tq:(g + 1) * tq]
            if g > 0:
                psum = psum + p[:, hd * tq:(hd + 1) * tq]
        hi, lo = _split_bf16(psum)
        imp = _dot(mt, hi) + _dot(mt, lo)
        forced = (blk == 0.0) | (blk == cur) | (blk == cur - 1.0)
        past = blk <= cur
        sc = jnp.where(past & ~forced, imp, NEG)
        for _ in range(N_SEL - N_FORCED):
            mx = jnp.max(sc, axis=0, keepdims=True)
            idx = jnp.min(jnp.where(sc == mx, blk, 1e9), axis=0, keepdims=True)
            sc = jnp.where(blk == idx, PICKED, sc)
        sel = past & (forced | (sc < PICKED / 2) | (cur < float(N_SEL)))
        bias_h = jnp.where(sel, 0.0, NEG).astype(BF16)
        biases.extend([bias_h] * GQA)
    bias = jnp.concatenate(biases, axis=1)
    qaug_ref[0, 0:KV_WIDTH, :] = q1
    qaug_ref[0, KV_WIDTH:, :] = bias
    if sbt_ref is not None:
        assert L == LANE
        pad = jnp.zeros((LANE - BLOCKS_PER_STEP, L), F32)
        for js in range(sbt_ref.shape[1]):
            grp = bias[js * BLOCKS_PER_STEP:(js + 1) * BLOCKS_PER_STEP, :].astype(F32)
            sbt_ref[0, js] = jnp.concatenate([grp, pad], axis=0).T.astype(BF16)


def _cmp_select(qblk, ck, cvt, mt, tl, *, tq, nq_per_b, n_bias_steps=0):
    nq, _, L = qblk.shape
    b, nch, _ = ck.shape
    nsel = mt.shape[0]
    shared_t = tl.shape[0] == 1
    out_specs = [pl.BlockSpec((1, N_HEADS * HEAD_DIM, tq), lambda i, j: (i * nq_per_b + j, 0, 0)),
                 pl.BlockSpec((1, KV_WIDTH + nsel, L), lambda i, j: (i * nq_per_b + j, 0, 0))]
    out_shape = [jax.ShapeDtypeStruct((nq, N_HEADS * HEAD_DIM, tq), F32),
                 jax.ShapeDtypeStruct((nq, KV_WIDTH + nsel, L), BF16)]
    if n_bias_steps:
        out_specs.append(pl.BlockSpec((1, n_bias_steps, L, LANE), lambda i, j: (i * nq_per_b + j, 0, 0, 0)))
        out_shape.append(jax.ShapeDtypeStruct((nq, n_bias_steps, L, LANE), BF16))
        body = functools.partial(_cmp_select_kernel, tq=tq)
    else:
        body = lambda *refs: _cmp_select_kernel(*refs, None, tq=tq)
    return pl.pallas_call(
        body,
        grid=(b, nq_per_b),
        in_specs=[pl.BlockSpec((1, KV_WIDTH, L), lambda i, j: (i * nq_per_b + j, 0, 0)),
                  pl.BlockSpec((1, nch, KV_WIDTH), lambda i, j: (i, 0, 0)),
                  pl.BlockSpec((1, KV_WIDTH, nch), lambda i, j: (i, 0, 0)),
                  pl.BlockSpec(mt.shape, lambda i, j: (0, 0)),
                  pl.BlockSpec((1, 1, L), (lambda i, j: (0, 0, 0)) if shared_t else (lambda i, j: (j, 0, 0)))],
        out_specs=out_specs, out_shape=out_shape,
        compiler_params=_cparams(("parallel", "parallel"), VMEM_LIMIT), name="cmp_select",
    )(qblk, ck, cvt, mt, tl)


VSUM_ROWS = 16
VT_ROWS = HEAD_DIM + VSUM_ROWS
SEL_TK = 512


def _sel_prompt_kernel(qaug_ref, kaug_ref, vt_ref, t_ref, o_ref, m_sc, acc_sc, sa_sc, sb_sc, *, tq, tk):
    qi = pl.program_id(1)
    L = N_HEADS * tq
    hl = L // N_KV_HEADS
    m_sc[...] = jnp.full(m_sc.shape, NEG, F32)
    acc_sc[...] = jnp.zeros(acc_sc.shape, F32)
    last = (qi * tq + tq - 1) // tk

    def scores(ki, buf):
        buf[...] = _dot(kaug_ref[0, pl.ds(pl.multiple_of(ki * tk, tk), tk), :], qaug_ref[0])

    def consume(ki, buf, causal):
        s = buf[...]
        if causal:
            kpos = ki * tk + lax.broadcasted_iota(I32, s.shape, 0)
            s = jnp.where(kpos <= t_ref[0], s, NEG)
        m_old = m_sc[...]
        m_new = jnp.maximum(m_old, jnp.max(s, axis=0, keepdims=True))
        m_sc[...] = m_new
        alpha = jnp.exp2(m_old - m_new)
        pb = jnp.exp2(s - m_new).astype(BF16)
        for h in range(N_KV_HEADS):
            ls = slice(h * hl, (h + 1) * hl)
            acc_sc[h] = acc_sc[h] * alpha[:, ls] + _dot(vt_ref[ki, h * VT_ROWS:(h + 1) * VT_ROWS, :], pb[:, ls])

    scores(0, sa_sc)

    def body(j, carry):
        scores(2 * j + 1, sb_sc)
        consume(2 * j, sa_sc, False)
        scores(2 * j + 2, sa_sc)
        consume(2 * j + 1, sb_sc, False)
        return carry

    lax.fori_loop(0, last // 2, body, 0)

    @pl.when(last % 2 == 1)
    def _():
        scores(last, sb_sc)
        consume(last - 1, sa_sc, False)
        consume(last, sb_sc, True)

    @pl.when(last % 2 == 0)
    def _():
        consume(last, sa_sc, True)

    for hd in range(N_HEADS):
        h, g = divmod(hd, GQA)
        gs = slice(g * tq, (g + 1) * tq)
        linv = 1.0 / jnp.maximum(acc_sc[h, HEAD_DIM:HEAD_DIM + 1, gs], TINY)
        o_ref[0, HEAD_DIM * hd:HEAD_DIM * (hd + 1), :] = acc_sc[h, 0:HEAD_DIM, gs] * linv


def _sel_prompt(qaug, kaug, vt, tl, *, tq, tk, bsz):
    nq, r, L = qaug.shape
    nqb = nq // bsz
    tlen = kaug.shape[0] // bsz
    return pl.pallas_call(
        functools.partial(_sel_prompt_kernel, tq=tq, tk=tk),
        grid=(bsz, nqb),
        in_specs=[pl.BlockSpec((1, r, L), lambda i, j: (i * nqb + j, 0, 0)),
                  pl.BlockSpec((1, tlen, r), lambda i, j: (i, 0, 0)),
                  pl.BlockSpec((tlen // tk, N_KV_HEADS * VT_ROWS, tk), lambda i, j: (i, 0, 0)),
                  pl.BlockSpec((1, 1, L), lambda i, j: (j, 0, 0))],
        out_specs=pl.BlockSpec((1, N_HEADS * HEAD_DIM, tq), lambda i, j: (i * nqb + j, 0, 0)),
        out_shape=jax.ShapeDtypeStruct((nq, N_HEADS * HEAD_DIM, tq), F32),
        scratch_shapes=[pltpu.VMEM((1, L), F32), pltpu.VMEM((N_KV_HEADS, VT_ROWS, L // N_KV_HEADS), F32),
                        pltpu.VMEM((tk, L), F32), pltpu.VMEM((tk, L), F32)],
        compiler_params=_cparams(("parallel", "parallel"), VMEM_LIMIT), name="sel_prompt",
    )(qaug, kaug.reshape(bsz, tlen, r), vt, tl)


def _sel_prep_kernel(k_ref, v_ref, oh_ref, kaug_ref, vt_ref):
    kaug_ref[:, 0:KV_WIDTH] = k_ref[...].astype(BF16)
    kaug_ref[:, KV_WIDTH:] = oh_ref[...]
    vt = v_ref[...].T.astype(BF16)
    ones = jnp.ones((VSUM_ROWS, vt.shape[1]), BF16)
    for h in range(N_KV_HEADS):
        vt_ref[0, h * VT_ROWS:h * VT_ROWS + HEAD_DIM, :] = vt[h * HEAD_DIM:(h + 1) * HEAD_DIM, :]
        vt_ref[0, h * VT_ROWS + HEAD_DIM:(h + 1) * VT_ROWS, :] = ones


def _sel_prep(k, v, oh, *, tk):
    n = k.shape[0]
    tiles_per_seq = oh.shape[0] // tk
    nsel = oh.shape[1]
    return pl.pallas_call(
        _sel_prep_kernel, grid=(n // tk,),
        in_specs=[pl.BlockSpec((tk, KV_WIDTH), lambda i: (i, 0)), pl.BlockSpec((tk, KV_WIDTH), lambda i: (i, 0)),
                  pl.BlockSpec((tk, nsel), lambda i: (i % tiles_per_seq, 0))],
        out_specs=(pl.BlockSpec((tk, KV_WIDTH + nsel), lambda i: (i, 0)),
                   pl.BlockSpec((1, N_KV_HEADS * VT_ROWS, tk), lambda i: (i, 0, 0))),
        out_shape=(jax.ShapeDtypeStruct((n, KV_WIDTH + nsel), BF16),
                   jax.ShapeDtypeStruct((n // tk, N_KV_HEADS * VT_ROWS, tk), BF16)),
        compiler_params=_cparams(("parallel",)), name="sel_prep",
    )(k, v, oh)


SEL_PAGES = 32
BLOCKS_PER_STEP = 64


def _sel_paged_kernel(pt_ref, q_ref, sbt_ref, sbt_tail_ref, ohc_ref, trow_ref, knew_ref, vnew_ref, *refs, past_len):
    del pt_ref
    g = SEL_PAGES
    kpages, vpages, o_ref, m_sc, l_sc, acc_sc, kcat_sc, vcat_sc = refs[:g], refs[g:2 * g], refs[2 * g], *refs[2 * g + 1:]
    js = pl.program_id(1)
    q = q_ref[0]

    @pl.when(js == 0)
    def _():
        m_sc[...] = jnp.full(m_sc.shape, NEG, F32)
        l_sc[...] = jnp.zeros(l_sc.shape, F32)
        acc_sc[...] = jnp.zeros(acc_sc.shape, F32)

    def update(s, vt):
        m_old = m_sc[...]
        m_new = jnp.maximum(m_old, jnp.max(s, axis=1, keepdims=True))
        alpha = jnp.exp2(m_old - m_new)
        p = jnp.exp2(s - m_new)
        l_sc[...] = alpha * l_sc[...] + jnp.sum(p, axis=1, keepdims=True)
        m_sc[...] = m_new
        acc_sc[...] = acc_sc[...] * alpha + _dot_nt(p.astype(BF16), vt)

    for i in range(g):
        kcat_sc[:, i * LANE:(i + 1) * LANE] = kpages[i][0, 0].astype(BF16)
        vcat_sc[:, i * LANE:(i + 1) * LANE] = vpages[i][0, 0].astype(BF16)
    ohc = ohc_ref[...]
    s = _dot(q, kcat_sc[...]) + _dot(sbt_ref[0, 0], ohc)
    update(s, vcat_sc[...])

    @pl.when(js == pl.num_programs(1) - 1)
    def _():
        sn = _dot(q, knew_ref[0].astype(BF16)) + _dot(sbt_tail_ref[0, 0], ohc[:, 0:LANE])
        kpos = past_len + lax.broadcasted_iota(I32, sn.shape, 1)
        sn = jnp.where(kpos <= trow_ref[...], sn, NEG)
        update(sn, vnew_ref[0].astype(BF16))
        o_ref[0] = acc_sc[...] * (1.0 / jnp.maximum(l_sc[...], TINY))


def _sel_paged(page_table, qstd, sbt, ohc, trow, knew_t, vnew_t, pool_k, pool_v, layer, *, past_len):
    b, rows, _ = qstd.shape
    g = SEL_PAGES
    page = pool_k.shape[3]
    n_steps = page_table.shape[1] // g
    pspec = lambda i: pl.BlockSpec((1, 1, KV_WIDTH, page), lambda bi, j, pt, i=i: (layer, pt[bi, j * g + i], 0, 0))
    per_b = lambda a: pl.BlockSpec((1,) + a.shape[1:], lambda bi, j, pt: (bi,) + (0,) * (a.ndim - 1))
    full = lambda a: pl.BlockSpec(a.shape, lambda bi, j, pt: (0,) * a.ndim)
    grid_spec = pltpu.PrefetchScalarGridSpec(
        num_scalar_prefetch=1, grid=(b, n_steps),
        in_specs=[per_b(qstd),
                  pl.BlockSpec((1, 1) + sbt.shape[2:], lambda bi, j, pt: (bi, j, 0, 0)),
                  pl.BlockSpec((1, 1) + sbt.shape[2:], lambda bi, j, pt: (bi, n_steps, 0, 0)),
                  full(ohc), full(trow), per_b(knew_t), per_b(vnew_t)]
                 + [pspec(i) for i in range(g)] + [pspec(i) for i in range(g)],
        out_specs=pl.BlockSpec((1, rows, KV_WIDTH), lambda bi, j, pt: (bi, 0, 0)),
        scratch_shapes=[pltpu.VMEM((rows, 1), F32), pltpu.VMEM((rows, 1), F32), pltpu.VMEM((rows, KV_WIDTH), F32),
                        pltpu.VMEM((KV_WIDTH, g * page), BF16), pltpu.VMEM((KV_WIDTH, g * page), BF16)])
    return pl.pallas_call(
        functools.partial(_sel_paged_kernel, past_len=past_len),
        grid_spec=grid_spec, out_shape=jax.ShapeDtypeStruct((b, rows, KV_WIDTH), F32),
        compiler_params=_cparams(("parallel", "arbitrary"), VMEM_LIMIT), name="sel_paged",
    )(page_table, qstd, sbt, sbt, ohc, trow, knew_t, vnew_t, *([pool_k] * g), *([pool_v] * g))


def _window_kernel(q_ref, k_ref, v_ref, t_ref, o_ref, *, tq, band, kpos_base):
    qi = pl.program_id(1)
    L = N_HEADS * tq
    hl = L // N_KV_HEADS
    q1 = q_ref[0]
    t = t_ref[0]
    start = pl.multiple_of(jnp.maximum(qi * tq + tq - band, 0), LANE)
    kb = k_ref[0, pl.ds(start, band), :].astype(BF16)
    s = _dot(kb, q1)
    r_hi = t - (kpos_base + start)
    row = lax.broadcasted_iota(I32, s.shape, 0)
    s = jnp.where((row <= r_hi) & (row > r_hi - WINDOW), s, NEG)
    m = jnp.max(s, axis=0, keepdims=True)
    pb = jnp.exp2(s - m).astype(BF16)
    keep = m > NEG / 2
    vt = v_ref[0, pl.ds(start, band), :].T.astype(BF16)
    ones = jnp.ones((VSUM_ROWS, band), BF16)
    for h in range(N_KV_HEADS):
        vth = jnp.concatenate([vt[HEAD_DIM * h:HEAD_DIM * (h + 1), :], ones], axis=0)
        o_h = _dot(vth, pb[:, h * hl:(h + 1) * hl])
        for g in range(GQA):
            hd = h * GQA + g
            gs = slice(g * tq, (g + 1) * tq)
            linv = jnp.where(keep[:, hd * tq:(hd + 1) * tq], 1.0 / jnp.maximum(o_h[HEAD_DIM:HEAD_DIM + 1, gs], TINY), 0.0)
            o_ref[0, HEAD_DIM * hd:HEAD_DIM * (hd + 1), :] = o_h[0:HEAD_DIM, gs] * linv


def _window(qblk, k, v, tl, *, tq, band, kpos_base):
    nq, _, L = qblk.shape
    b, tlen, _ = k.shape
    nqb = nq // b
    shared_t = tl.shape[0] == 1
    return pl.pallas_call(
        functools.partial(_window_kernel, tq=tq, band=band, kpos_base=kpos_base),
        grid=(b, nqb),
        in_specs=[pl.BlockSpec((1, KV_WIDTH, L), lambda i, j: (i * nqb + j, 0, 0)),
                  pl.BlockSpec((1, tlen, KV_WIDTH), lambda i, j: (i, 0, 0)),
                  pl.BlockSpec((1, tlen, KV_WIDTH), lambda i, j: (i, 0, 0)),
                  pl.BlockSpec((1, 1, L), (lambda i, j: (0, 0, 0)) if shared_t else (lambda i, j: (j, 0, 0)))],
        out_specs=pl.BlockSpec((1, N_HEADS * HEAD_DIM, tq), lambda i, j: (i * nqb + j, 0, 0)),
        out_shape=jax.ShapeDtypeStruct((nq, N_HEADS * HEAD_DIM, tq), F32),
        compiler_params=_cparams(("parallel", "parallel"), VMEM_LIMIT), name="window",
    )(qblk, k, v, tl)


def _combine_kernel(x_ref, a_ref, oc_ref, os_ref, ow_ref, gt_ref, e_ref, woa_ref, wob_ref, o_ref, *, tm):
    hi, lo = _split_bf16(gt_ref[...])
    ge = _dot(e_ref[...], hi) + _dot(e_ref[...], lo)
    bw = N_HEADS * HEAD_DIM
    parts = []
    for j in range(tm // LANE):
        ls = slice(j * LANE, (j + 1) * LANE)
        parts.append(ge[0:bw, ls] * oc_ref[j] + ge[bw:2 * bw, ls] * os_ref[j] + ge[2 * bw:3 * bw, ls] * ow_ref[j])
    mixt = parts[0] if len(parts) == 1 else jnp.concatenate(parts, axis=1)
    mix = mixt.T.astype(BF16)
    o_ref[...] = x_ref[...] + _dot(a_ref[...], woa_ref[...]) + _dot(mix, wob_ref[...])


def _combine(x, a, oc, os_, ow, gt, e, woa, wob, *, tm):
    n, dm = x.shape
    bw = N_HEADS * HEAD_DIM
    full = lambda arr: pl.BlockSpec(arr.shape, lambda i: (0,) * arr.ndim)
    ospec = pl.BlockSpec((tm // LANE, bw, LANE), lambda i: (i, 0, 0))
    return pl.pallas_call(
        functools.partial(_combine_kernel, tm=tm),
        grid=(n // tm,),
        in_specs=[pl.BlockSpec((tm, dm), lambda i: (i, 0)), pl.BlockSpec((tm, a.shape[1]), lambda i: (i, 0)),
                  ospec, ospec, ospec, pl.BlockSpec((32, tm), lambda i: (0, i)), full(e), full(woa), full(wob)],
        out_specs=pl.BlockSpec((tm, dm), lambda i: (i, 0)),
        out_shape=jax.ShapeDtypeStruct((n, dm), F32),
        compiler_params=_cparams(("parallel",), VMEM_LIMIT), name="combine",
    )(x, a, oc, os_, ow, gt, e, woa, wob)


def _ffn_dense_kernel(x_ref, g_ref, wg_ref, wu_ref, wd_ref, o_ref):
    xf = x_ref[...]
    h = _rms(xf, g_ref[...]).astype(BF16)
    gate = _dot(h, wg_ref[...])
    up = _dot(h, wu_ref[...])
    act = (gate * _sigmoid(gate) * up).astype(BF16)
    o_ref[...] = xf + _dot(act, wd_ref[...])


def _ffn_dense(x, g, wg, wu, wd, *, tm):
    n, dm = x.shape
    full = lambda arr: pl.BlockSpec(arr.shape, lambda i: (0,) * arr.ndim, pipeline_mode=pl.Buffered(1))
    return pl.pallas_call(
        _ffn_dense_kernel, grid=(n // tm,),
        in_specs=[pl.BlockSpec((tm, dm), lambda i: (i, 0)), full(g), full(wg), full(wu), full(wd)],
        out_specs=pl.BlockSpec((tm, dm), lambda i: (i, 0)),
        out_shape=jax.ShapeDtypeStruct((n, dm), F32),
        compiler_params=_cparams(("parallel",), VMEM_LIMIT), name="ffn_dense",
    )(x, g, wg, wu, wd)


def _router_kernel(x_ref, g_ref, wrh_ref, wrl_ref, tri_ref, h_ref, meta_ref, cnt_ref, *, tm, n_exp):
    h = _rms(x_ref[...], g_ref[...])
    h_ref[...] = h
    hi, lo = _split_bf16(h)
    logits = _dot(hi, wrh_ref[...]) + _dot(lo, wrh_ref[...]) + _dot(hi, wrl_ref[...])
    lane = lax.broadcasted_iota(I32, (tm, LANE), 1)
    lanef = lane.astype(F32)
    logits = jnp.where(lane < n_exp, logits, NEG)
    m1 = jnp.max(logits, axis=1, keepdims=True)
    i1 = jnp.min(jnp.where(logits == m1, lanef, 1e9), axis=1, keepdims=True)
    rest = jnp.where(lanef == i1, -3e38, logits)
    m2 = jnp.max(rest, axis=1, keepdims=True)
    i2 = jnp.min(jnp.where(rest == m2, lanef, 1e9), axis=1, keepdims=True)
    e2 = jnp.exp(m2 - m1)
    w1 = 1.0 / (1.0 + e2)
    w2 = e2 / (1.0 + e2)
    hit1 = lanef == i1
    hit2 = lanef == i2
    msel = jnp.where(hit1 | hit2, 1.0, 0.0)
    ranks = _dot(tri_ref[...], msel.astype(BF16))
    r1 = jnp.sum(jnp.where(hit1, ranks, 0.0), axis=1, keepdims=True)
    r2 = jnp.sum(jnp.where(hit2, ranks, 0.0), axis=1, keepdims=True)
    meta = jnp.where(lane == 0, i1, 0.0)
    for k, val in enumerate((i2, w1, w2, r1, r2)):
        meta = jnp.where(lane == k + 1, val, meta)
    meta_ref[...] = meta
    cnt_ref[0] = jnp.sum(msel, axis=0, keepdims=True)


def _router(x, g, wrh, wrl, tri, *, tm, n_exp):
    n, dm = x.shape
    full = lambda arr: pl.BlockSpec(arr.shape, lambda i: (0,) * arr.ndim)
    return pl.pallas_call(
        functools.partial(_router_kernel, tm=tm, n_exp=n_exp), grid=(n // tm,),
        in_specs=[pl.BlockSpec((tm, dm), lambda i: (i, 0)), full(g), full(wrh), full(wrl), full(tri)],
        out_specs=(pl.BlockSpec((tm, dm), lambda i: (i, 0)), pl.BlockSpec((tm, LANE), lambda i: (i, 0)),
                   pl.BlockSpec((1, 1, LANE), lambda i: (i, 0, 0))),
        out_shape=(jax.ShapeDtypeStruct((n, dm), F32), jax.ShapeDtypeStruct((n, LANE), F32),
                   jax.ShapeDtypeStruct((n // tm, 1, LANE), F32)),
        compiler_params=_cparams(("parallel",), VMEM_LIMIT), name="router",
    )(x, g, wrh, wrl, tri)


def _row_copy(src, src_row, dst, dst_row, sem):
    return pltpu.make_async_copy(src.at[pl.ds(src_row, 1)], dst.at[pl.ds(dst_row, 1)], sem)


def _dispatch_kernel(dest_ref, h_ref, xg_in_ref, xg_ref, sem, *, tm):
    del xg_in_ref

    def issue(r, carry):
        for k in range(TOP_K):
            _row_copy(h_ref, r, xg_ref, dest_ref[TOP_K * r + k], sem).start()
        return carry

    lax.fori_loop(0, tm, issue, 0)

    def drain(r, carry):
        for k in range(TOP_K):
            _row_copy(h_ref, 0, xg_ref, 0, sem).wait()
        return carry

    lax.fori_loop(0, tm, drain, 0)


def _dispatch(dest, h, xg0, *, tm):
    n, dm = h.shape
    return pl.pallas_call(
        functools.partial(_dispatch_kernel, tm=tm), grid=(n // tm,),
        in_specs=[pl.BlockSpec((TOP_K * tm,), lambda i: (i,), memory_space=pltpu.SMEM),
                  pl.BlockSpec((tm, dm), lambda i: (i, 0)),
                  pl.BlockSpec(memory_space=pl.ANY)],
        out_specs=pl.BlockSpec(memory_space=pl.ANY),
        out_shape=jax.ShapeDtypeStruct(xg0.shape, F32),
        scratch_shapes=[pltpu.SemaphoreType.DMA],
        input_output_aliases={2: 0},
        compiler_params=_cparams(("arbitrary",), VMEM_LIMIT), name="moe_dispatch",
    )(dest, h, xg0)


def _expert_ffn_kernel(te_ref, nused_ref, x_ref, wg_ref, wu_ref, wd_ref, o_ref):
    s = pl.program_id(0)

    @pl.when(s < nused_ref[0])
    def _():
        h = x_ref[...].astype(BF16)
        gate = _dot(h, wg_ref[0])
        up = _dot(h, wu_ref[0])
        act = (gate * _sigmoid(gate) * up).astype(BF16)
        o_ref[...] = _dot(act, wd_ref[0])

    @pl.when(s >= nused_ref[0])
    def _():
        o_ref[...] = jnp.zeros(o_ref.shape, F32)


def _expert_ffn(te, nused, xg, wg, wu, wd, *, ts):
    s_rows, dm = xg.shape
    dff = wg.shape[2]
    grid_spec = pltpu.PrefetchScalarGridSpec(
        num_scalar_prefetch=2, grid=(s_rows // ts,),
        in_specs=[pl.BlockSpec((ts, dm), lambda i, te, nu: (i, 0)),
                  pl.BlockSpec((1, dm, dff), lambda i, te, nu: (te[i], 0, 0), pipeline_mode=pl.Buffered(1)),
                  pl.BlockSpec((1, dm, dff), lambda i, te, nu: (te[i], 0, 0), pipeline_mode=pl.Buffered(1)),
                  pl.BlockSpec((1, dff, dm), lambda i, te, nu: (te[i], 0, 0), pipeline_mode=pl.Buffered(1))],
        out_specs=pl.BlockSpec((ts, dm), lambda i, te, nu: (i, 0)))
    return pl.pallas_call(
        _expert_ffn_kernel, grid_spec=grid_spec,
        out_shape=jax.ShapeDtypeStruct((s_rows, dm), F32),
        compiler_params=_cparams(("arbitrary",), VMEM_LIMIT), name="expert_ffn",
    )(te, nused, xg, wg, wu, wd)


def _moe_combine_kernel(dest_ref, x_ref, meta_ref, y_ref, o_ref, buf, sem, *, tm):
    def issue(r, carry):
        for k in range(TOP_K):
            _row_copy(y_ref, dest_ref[TOP_K * r + k], buf.at[k], r, sem).start()
        return carry

    lax.fori_loop(0, tm, issue, 0)

    def drain(r, carry):
        for k in range(TOP_K):
            _row_copy(y_ref, 0, buf.at[k], 0, sem).wait()
        return carry

    lax.fori_loop(0, tm, drain, 0)
    meta = meta_ref[...]
    o_ref[...] = x_ref[...] + meta[:, 2:3] * buf[0] + meta[:, 3:4] * buf[1]


def _moe_combine(dest, x, meta, y, *, tm):
    n, dm = x.shape
    return pl.pallas_call(
        functools.partial(_moe_combine_kernel, tm=tm), grid=(n // tm,),
        in_specs=[pl.BlockSpec((TOP_K * tm,), lambda i: (i,), memory_space=pltpu.SMEM),
                  pl.BlockSpec((tm, dm), lambda i: (i, 0)),
                  pl.BlockSpec((tm, LANE), lambda i: (i, 0)),
                  pl.BlockSpec(memory_space=pl.ANY)],
        out_specs=pl.BlockSpec((tm, dm), lambda i: (i, 0)),
        out_shape=jax.ShapeDtypeStruct((n, dm), F32),
        scratch_shapes=[pltpu.VMEM((TOP_K, tm, dm), F32), pltpu.SemaphoreType.DMA],
        compiler_params=_cparams(("arbitrary",), VMEM_LIMIT), name="moe_combine",
    )(dest, x, meta, y)


def _moe(x, g, wr, wg, wu, wd, *, tm, ts):
    n, dm = x.shape
    n_exp = wr.shape[1]
    wr_pad = jnp.zeros((dm, LANE), F32).at[:, :n_exp].set(wr)
    wrh, wrl = _split_bf16(wr_pad)
    tri = jnp.tril(jnp.ones((tm, tm), F32), -1).astype(BF16)
    h, meta, cnt = _router(x, g, wrh, wrl, tri, tm=tm, n_exp=n_exp)
    cnt = cnt[:, 0, :n_exp].astype(I32)
    tile_off = jnp.cumsum(cnt, axis=0) - cnt
    tot = jnp.sum(cnt, axis=0)
    padded = ((tot + ts - 1) // ts) * ts
    gend = jnp.cumsum(padded)
    gstart = gend - padded
    ei = meta[:, 0:2].astype(I32)
    rk = meta[:, 4:6].astype(I32)
    base = jnp.broadcast_to((gstart[None, :] + tile_off)[:, None, :], (n // tm, tm, n_exp)).reshape(n, 1, n_exp)
    pick = ei[:, :, None] == jnp.arange(n_exp, dtype=I32)[None, None, :]
    dest = (jnp.sum(jnp.where(pick, base, 0), axis=-1) + rk).reshape(-1)
    n_slot_tiles = -(-(TOP_K * n + n_exp * (ts - 1)) // ts)
    n_slots = n_slot_tiles * ts
    nused = (gend[-1] // ts).astype(I32).reshape(1)
    tile_start = jnp.minimum(jnp.arange(n_slot_tiles, dtype=I32), jnp.maximum(nused[0] - 1, 0)) * ts
    te = jnp.minimum(jnp.sum((gend[None, :] <= tile_start[:, None]).astype(I32), axis=1), n_exp - 1)
    xg = _dispatch(dest, h, jnp.zeros((n_slots, dm), F32), tm=tm)
    y = _expert_ffn(te, nused, xg, wg, wu, wd, ts=ts)
    return _moe_combine(dest, x, meta, y, tm=tm)


def _norm_kernel(x_ref, g_ref, o_ref):
    o_ref[...] = _rms(x_ref[...], g_ref[...])


def _final_norm(x, g, *, tm):
    n, dm = x.shape
    return pl.pallas_call(
        _norm_kernel, grid=(n // tm,),
        in_specs=[pl.BlockSpec((tm, dm), lambda i: (i, 0)), pl.BlockSpec(g.shape, lambda i: (0, 0))],
        out_specs=pl.BlockSpec((tm, dm), lambda i: (i, 0)),
        out_shape=jax.ShapeDtypeStruct((n, dm), F32),
        compiler_params=_cparams(("parallel",)), name="final_norm",
    )(x, g)


def _rope_tables(pos):
    half = HEAD_DIM // 2
    inv = ROPE_THETA ** (-jnp.arange(half, dtype=F32) / half)
    ang = pos.astype(F32)[:, None] * inv[None, :]
    cos, sin = jnp.cos(ang), jnp.sin(ang)
    cos_std = jnp.tile(cos, (1, 4))
    sin_std = jnp.tile(jnp.concatenate([-sin, sin], axis=1), (1, 2))
    return cos_std, sin_std, cos.T, sin.T


def _cmp_to_sel_t(n_cmp_pad, n_sel_pad):
    i0 = np.arange(n_cmp_pad, dtype=np.int64)[None, :] * CMP_STRIDE
    j0 = np.arange(n_sel_pad, dtype=np.int64)[:, None] * SEL_LEN
    ov = np.clip(np.minimum(i0 + CMP_LEN, j0 + SEL_LEN) - np.maximum(i0, j0), 0, CMP_LEN)
    return jnp.asarray(ov.astype(np.float32) / CMP_LEN, dtype=BF16)


def _block_onehot(n_keys, n_sel_pad):
    blk = np.arange(n_keys)[:, None] // SEL_LEN
    return jnp.asarray((blk == np.arange(n_sel_pad)[None, :]).astype(np.float32), dtype=BF16)


def _lane_positions(pos_tiles, tq):
    return jnp.tile(pos_tiles.astype(I32), (1, N_HEADS))[:, None, :]


def _compress_weights(w1, w2, pe):
    nr = CMP_LEN // CMP_STRIDE
    w1r = w1.reshape(nr, CMP_STRIDE, HEAD_DIM, CMP_HIDDEN)
    eye = jnp.eye(N_KV_HEADS, dtype=F32)
    wbig = jnp.einsum("rsde,gh->sgdrhe", w1r, eye)
    wbig = wbig.reshape(CMP_STRIDE * KV_WIDTH, nr * N_KV_HEADS * CMP_HIDDEN).astype(BF16)
    w2big = jnp.einsum("ed,gh->gehd", w2, eye).reshape(N_KV_HEADS * CMP_HIDDEN, KV_WIDTH).astype(BF16)
    pe8 = jnp.tile(pe.reshape(1, CMP_LEN * HEAD_DIM), (8, 1)).astype(BF16)
    return wbig, pe8, w1.astype(BF16), w2big


def _layer_weights(l, g_mix, w_in, g_sgu, w_sgu, b_sgu, w_cmpk1, w_cmpk2, pe_cmpk, w_cmpv1, w_cmpv2, pe_cmpv, w_o, dec_t):
    aw = A_GROUPS * CHUNK
    bw = N_HEADS * HEAD_DIM
    w = w_in[l]
    wstd = jnp.concatenate([w[:, :2 * aw], w[:, 2 * aw + bw:2 * aw + bw + 6 * KV_WIDTH]], axis=1).astype(BF16)
    n_gate = 3 * N_HEADS
    wt = jnp.concatenate([w[:, 2 * aw:2 * aw + bw], w[:, -n_gate:], jnp.zeros((w.shape[0], 32 - n_gate), F32)], axis=1).T.astype(BF16)
    causal = jnp.tril(jnp.ones((CHUNK, CHUNK), bool))
    wsgu_p = jnp.where(causal[None], w_sgu[l], 0.0).astype(BF16)
    bsgu_p = jnp.broadcast_to(b_sgu[l][:, :, None], (A_GROUPS, CHUNK, CHUNK)).astype(F32)
    reps = CHUNK // dec_t
    wsmall = jnp.where(causal[None, :dec_t, :dec_t], w_sgu[l][:, :dec_t, :dec_t], 0.0)
    wsgu_s = jnp.einsum("ab,gts->gatbs", jnp.eye(reps, dtype=F32), wsmall).reshape(A_GROUPS, CHUNK, CHUNK).astype(BF16)
    bsgu_s = jnp.broadcast_to(jnp.tile(b_sgu[l][:, :dec_t], (1, reps))[:, :, None], (A_GROUPS, CHUNK, CHUNK)).astype(F32)
    ck_w = _compress_weights(w_cmpk1[l], w_cmpk2[l], pe_cmpk[l])
    cv_w = _compress_weights(w_cmpv1[l], w_cmpv2[l], pe_cmpv[l])
    cv_w = cv_w[:3] + (cv_w[3].T,)
    e = np.zeros((3 * bw, 32), np.float32)
    for br in range(3):
        for hd in range(N_HEADS):
            e[br * bw + hd * HEAD_DIM:br * bw + (hd + 1) * HEAD_DIM, br * N_HEADS + hd] = 1.0
    return dict(gmix=g_mix[l][None, :], wstd=wstd, wt=wt, gsgu=g_sgu[l][None, :],
                wsgu_p=wsgu_p, bsgu_p=bsgu_p, wsgu_s=wsgu_s, bsgu_s=bsgu_s, ck_w=ck_w, cv_w=cv_w,
                e=jnp.asarray(e, dtype=BF16), woa=w_o[l][:aw].astype(BF16), wob=w_o[l][aw:].astype(BF16))


def _channel_mixer(l, x, g_ffn, dense_w, w_router, moe_w, *, tm_dense, tm_moe):
    g = g_ffn[l][None, :]
    i = l // 2
    if l % 2 == 0:
        return _ffn_dense(x, g, *(w[i] for w in dense_w), tm=tm_dense)
    return _moe(x, g, w_router[i], *(w[i] for w in moe_w), tm=tm_moe, ts=512)


def kernel(x_prompt, x_sample, cache_cmp_k, cache_cmp_v, cache_sel_k, cache_sel_v, cache_win_k, cache_win_v, page_table,
           g_mix, w_in, g_sgu, w_sgu, b_sgu, w_cmpk1, w_cmpk2, pe_cmpk, w_cmpv1, w_cmpv2, pe_cmpv, w_o, g_ffn,
           w_ff_gate, w_ff_up, w_ff_down, w_router, w_moe_gate, w_moe_up, w_moe_down, g_final):
    bsz, seq, dm = x_prompt.shape
    dec_b, dec_t, _ = x_sample.shape
    depth = g_mix.shape[0]
    page = cache_cmp_k.shape[2]
    past_len = page_table.shape[1] * page
    wbuf = cache_win_k.shape[2]
    n_p, n_s = bsz * seq, dec_b * dec_t
    assert n_s == CHUNK and seq % 512 == 0 and wbuf == WINDOW and past_len % 2048 == 0
    tq_p, tq_s = LANE, LANE // N_HEADS
    nq_b = seq // tq_p
    band = WINDOW + tq_p

    pos_p = jnp.tile(jnp.arange(seq, dtype=I32), bsz)
    pos_s = jnp.tile(past_len + jnp.arange(dec_t, dtype=I32), dec_b)
    rope_p = _rope_tables(pos_p)
    rope_s = _rope_tables(pos_s)
    tl_p = _lane_positions(jnp.arange(seq, dtype=I32).reshape(nq_b, tq_p), tq_p)
    tok_s = past_len + jnp.minimum(jnp.arange(tq_s, dtype=I32), dec_t - 1)
    tl_s = _lane_positions(tok_s[None, :], tq_s)
    nch_p = seq // CMP_STRIDE
    nsel_p = seq // SEL_LEN
    nch_s = past_len // CMP_STRIDE
    nsel_s = -(-(past_len // SEL_LEN + 1) // LANE) * LANE
    mt_p = _cmp_to_sel_t(nch_p, nsel_p)
    mt_s = _cmp_to_sel_t(nch_s, nsel_s)
    oh_p = _block_onehot(seq, nsel_p)
    n_sel_steps = past_len // (SEL_PAGES * page) + 1
    assert nsel_s >= n_sel_steps * BLOCKS_PER_STEP and SEL_PAGES * page == BLOCKS_PER_STEP * SEL_LEN
    ohc_s = _block_onehot(SEL_PAGES * page, LANE).T
    trow_s = jnp.broadcast_to(jnp.tile(tok_s, N_HEADS)[:, None], (N_HEADS * tq_s, LANE)).astype(I32)
    head_of_row = jnp.asarray((np.arange(N_HEADS)[None, :] // GQA == np.arange(N_KV_HEADS)[:, None]).astype(np.float32))

    pools_t = [jnp.transpose(c, (0, 1, 3, 4, 2)).reshape(depth, c.shape[1], KV_WIDTH, page)
               for c in (cache_cmp_k, cache_cmp_v, cache_sel_k, cache_sel_v)]
    flat = lambda a: a.reshape(a.shape[0], a.shape[1], KV_WIDTH)

    dense_w = tuple(w.astype(BF16) for w in (w_ff_gate, w_ff_up, w_ff_down))
    moe_w = tuple(w.astype(BF16) for w in (w_moe_gate, w_moe_up, w_moe_down))
    xp = x_prompt.reshape(n_p, dm)
    xs = x_sample.reshape(n_s, dm)
    outs = [[] for _ in range(13)]
    for l in range(depth):
        lw = _layer_weights(l, g_mix, w_in, g_sgu, w_sgu, b_sgu, w_cmpk1, w_cmpk2, pe_cmpk, w_cmpv1, w_cmpv2, pe_cmpv, w_o, dec_t)
        a, _, kc, vc, ks, vs, kw, vw, qblk, gt = _inproj(
            xp, lw["gmix"], lw["wstd"], lw["wt"], lw["gsgu"], lw["wsgu_p"], lw["bsgu_p"], *rope_p, tm=512, emit_qblk=True)
        ck = _compress(kc.reshape(bsz, nch_p, CMP_STRIDE * KV_WIDTH), *lw["ck_w"], transpose_out=False)
        cvt = _compress(vc.reshape(bsz, nch_p, CMP_STRIDE * KV_WIDTH), *lw["cv_w"], transpose_out=True)
        oc, qaug = _cmp_select(qblk, ck, cvt, mt_p, tl_p, tq=tq_p, nq_per_b=nq_b)
        kaug, vst = _sel_prep(ks, vs, oh_p, tk=SEL_TK)
        osel = _sel_prompt(qaug, kaug, vst, tl_p, tq=tq_p, tk=SEL_TK, bsz=bsz)
        ow = _window(qblk, kw.reshape(bsz, seq, KV_WIDTH), vw.reshape(bsz, seq, KV_WIDTH), tl_p, tq=tq_p, band=band, kpos_base=0)
        xp = _combine(xp, a, oc, osel, ow, gt, lw["e"], lw["woa"], lw["wob"], tm=512)
        kv4 = lambda t: t.reshape(bsz, seq, N_KV_HEADS, HEAD_DIM)
        for idx, t in enumerate((kc, vc, ks, vs)):
            outs[idx].append(kv4(t))
        nwin_p = min(WINDOW, seq)
        outs[4].append(kv4(kw)[:, seq - nwin_p:])
        outs[5].append(kv4(vw)[:, seq - nwin_p:])

        a, v_s, kc, vc, ks, vs, kw, vw, qt, gt = _inproj(
            xs, lw["gmix"], lw["wstd"], lw["wt"], lw["gsgu"], lw["wsgu_s"], lw["bsgu_s"], *rope_s, tm=CHUNK, emit_qblk=False)
        q4 = qt.reshape(N_HEADS, HEAD_DIM, dec_b, dec_t)
        q4 = jnp.pad(q4, ((0, 0), (0, 0), (0, 0), (0, tq_s - dec_t)))
        qb = jnp.einsum("hdbt,gh->bgdht", q4.astype(F32), head_of_row).astype(BF16)
        qblk_s = qb.reshape(dec_b, KV_WIDTH, N_HEADS * tq_s)
        ck = _compress_paged(page_table, pools_t[0], l, *lw["ck_w"], transpose_out=False)
        cvt = _compress_paged(page_table, pools_t[1], l, *lw["cv_w"], transpose_out=True)
        oc, _, sbt = _cmp_select(qblk_s, ck, cvt, mt_s, tl_s, tq=tq_s, nq_per_b=1, n_bias_steps=n_sel_steps)
        new_t = lambda t: jnp.pad(t.reshape(dec_b, dec_t, KV_WIDTH).transpose(0, 2, 1), ((0, 0), (0, 0), (0, page - dec_t)))
        osel_rows = _sel_paged(page_table, jnp.swapaxes(qblk_s, 1, 2), sbt, ohc_s, trow_s, new_t(ks), new_t(vs),
                               pools_t[2], pools_t[3], l, past_len=past_len)
        osel = jnp.einsum("bhtgd,gh->bhdt", osel_rows.reshape(dec_b, N_HEADS, tq_s, N_KV_HEADS, HEAD_DIM),
                          head_of_row).reshape(dec_b, N_HEADS * HEAD_DIM, tq_s)
        kw_all = jnp.concatenate([flat(cache_win_k[l]), kw.reshape(dec_b, dec_t, KV_WIDTH)], axis=1)
        vw_all = jnp.concatenate([flat(cache_win_v[l]), vw.reshape(dec_b, dec_t, KV_WIDTH)], axis=1)
        wpad = ((0, 0), (0, band - wbuf - dec_t), (0, 0))
        ow = _window(qblk_s, jnp.pad(kw_all, wpad), jnp.pad(vw_all, wpad), tl_s, tq=tq_s, band=band, kpos_base=past_len - wbuf)
        untile = lambda o: o[:, :, :dec_t].transpose(1, 0, 2).reshape(1, N_HEADS * HEAD_DIM, n_s)
        xs = _combine(xs, a, untile(oc), untile(osel), untile(ow), gt, lw["e"], lw["woa"], lw["wob"], tm=CHUNK)
        kv4s = lambda t: t.reshape(dec_b, dec_t, N_KV_HEADS, HEAD_DIM)
        for idx, t in enumerate((kc, vc, ks, vs)):
            outs[6 + idx].append(kv4s(t))
        nwin_s = min(WINDOW, wbuf + dec_t)
        outs[10].append(kw_all[:, wbuf + dec_t - nwin_s:].reshape(dec_b, nwin_s, N_KV_HEADS, HEAD_DIM))
        outs[11].append(vw_all[:, wbuf + dec_t - nwin_s:].reshape(dec_b, nwin_s, N_KV_HEADS, HEAD_DIM))
        outs[12].append(v_s.reshape(dec_b, dec_t, A_GROUPS * CHUNK))

        xp = _channel_mixer(l, xp, g_ffn, dense_w, w_router, moe_w, tm_dense=512, tm_moe=512)
        xs = _channel_mixer(l, xs, g_ffn, dense_w, w_router, moe_w, tm_dense=CHUNK, tm_moe=CHUNK)

    gf = g_final[None, :]
    y_prompt = _final_norm(xp, gf, tm=512).reshape(bsz, seq, dm)
    y_sample = _final_norm(xs, gf, tm=CHUNK).reshape(dec_b, dec_t, dm)
    return (y_prompt, y_sample) + tuple(jnp.stack(o, axis=0) for o in outs)
```

```python
import functools
import math

import numpy as np
import jax
import jax.numpy as jnp
from jax import lax
from jax.experimental import pallas as pl
from jax.experimental.pallas import tpu as pltpu

F32 = jnp.float32
BF16 = jnp.bfloat16
I32 = jnp.int32

A_GROUPS = 4
N_HEADS = 8
N_KV_HEADS = 2
HEAD_DIM = 64
CMP_LEN = 32
CMP_STRIDE = 16
CMP_HIDDEN = 128
SEL_LEN = 64
N_SEL = 16
WINDOW = 512
CHUNK = 128
ROPE_THETA = 10000.0
TOP_K = 2
EPS = 1e-6
NEG = -1e30
BIG = 1e30
TINY = 1e-30
PICKED = -3e38
N_FORCED = 3
SCALE = HEAD_DIM ** -0.5
QSCALE = SCALE * math.log2(math.e)

LANE = 128
KV_WIDTH = N_KV_HEADS * HEAD_DIM
GQA = N_HEADS // N_KV_HEADS
VMEM_LIMIT = 56 * 1024 * 1024


def _cparams(sem, vmem=None):
    return pltpu.CompilerParams(dimension_semantics=sem, vmem_limit_bytes=vmem)


def _rms(xf, g):
    return xf * lax.rsqrt(jnp.mean(xf * xf, axis=-1, keepdims=True) + EPS) * g


def _gelu(x):
    c = math.sqrt(2.0 / math.pi)
    return x * (0.5 * (1.0 + jnp.tanh(c * (x + 0.044715 * (x * x * x)))))


def _sigmoid(x):
    return 1.0 / (1.0 + jnp.exp(-x))


def _dot(a, b):
    return jnp.dot(a, b, preferred_element_type=F32)


def _dot_nt(a, b):
    return lax.dot_general(a, b, (((1,), (1,)), ((), ())), preferred_element_type=F32)


def _split_bf16(x):
    hi = x.astype(BF16)
    lo = (x - hi.astype(F32)).astype(BF16)
    return hi, lo


def _inproj_kernel(x_ref, gmix_ref, wstd_ref, wt_ref, gsgu_ref, wsgu_ref, bsgu_ref,
                   cos_ref, sin_ref, cost_ref, sint_ref,
                   a_ref, v_ref, kc_ref, vc_ref, ks_ref, vs_ref, kw_ref, vw_ref, q_ref, gt_ref,
                   *, tm, emit_qblk):
    xf = x_ref[...]
    h = _rms(xf, gmix_ref[...]).astype(BF16)
    z = _dot(h, wstd_ref[...])
    zt = _dot_nt(wt_ref[...], h)
    aw = A_GROUPS * CHUNK
    u = _gelu(z[:, 0:aw])
    vv = _gelu(z[:, aw:2 * aw])
    mu = jnp.mean(vv, axis=-1, keepdims=True)
    d = vv - mu
    var = jnp.mean(d * d, axis=-1, keepdims=True)
    v = d * lax.rsqrt(var + EPS) * gsgu_ref[...]
    v_ref[...] = v
    vb = v.astype(BF16)
    nc = tm // CHUNK
    for g in range(A_GROUPS):
        gs = slice(g * CHUNK, (g + 1) * CHUNK)
        parts = [vb[c * CHUNK:(c + 1) * CHUNK, gs] for c in range(nc)]
        xg = parts[0] if nc == 1 else jnp.concatenate(parts, axis=1)
        yg = _dot(wsgu_ref[g], xg)
        for c in range(nc):
            cs = slice(c * CHUNK, (c + 1) * CHUNK)
            mixed = yg[:, cs] + bsgu_ref[g]
            a_ref[cs, gs] = (u[cs, gs] * mixed).astype(BF16)

    cosr = cos_ref[...]
    sinr = sin_ref[...]
    lane = lax.broadcasted_iota(I32, (tm, KV_WIDTH), 1)
    first = (lane % HEAD_DIM) < (HEAD_DIM // 2)

    def rope(x):
        rot = jnp.where(first, pltpu.roll(x, KV_WIDTH - HEAD_DIM // 2, 1), pltpu.roll(x, HEAD_DIM // 2, 1))
        return x * cosr + rot * sinr

    o = 2 * aw
    kc_ref[...] = rope(z[:, o:o + 128])
    vc_ref[...] = z[:, o + 128:o + 256]
    ks_ref[...] = rope(z[:, o + 256:o + 384])
    vs_ref[...] = z[:, o + 384:o + 512]
    kw_ref[...] = rope(z[:, o + 512:o + 640])
    vw_ref[...] = z[:, o + 640:o + 768]

    ct = cost_ref[...]
    st = sint_ref[...]
    half = HEAD_DIM // 2
    for hd in range(N_HEADS):
        x1 = zt[HEAD_DIM * hd:HEAD_DIM * hd + half]
        x2 = zt[HEAD_DIM * hd + half:HEAD_DIM * (hd + 1)]
        qh = jnp.concatenate([(x1 * ct - x2 * st) * QSCALE, (x2 * ct + x1 * st) * QSCALE], axis=0).astype(BF16)
        if emit_qblk:
            kvh = hd // GQA
            zero = jnp.zeros((HEAD_DIM, LANE), BF16)
            for j in range(tm // LANE):
                ls = slice(hd * LANE, (hd + 1) * LANE)
                q_ref[j, HEAD_DIM * kvh:HEAD_DIM * (kvh + 1), ls] = qh[:, j * LANE:(j + 1) * LANE]
                q_ref[j, HEAD_DIM * (1 - kvh):HEAD_DIM * (2 - kvh), ls] = zero
        else:
            q_ref[HEAD_DIM * hd:HEAD_DIM * (hd + 1), :] = qh
    nq = N_HEADS * HEAD_DIM
    gt_ref[...] = _sigmoid(zt[nq:nq + 32])


def _inproj(x, gmix, wstd, wt, gsgu, wsgu, bsgu, cos, sin, cost, sint, *, tm, emit_qblk):
    n, dm = x.shape
    nt = n // tm
    row = lambda w: pl.BlockSpec((tm, w), lambda i: (i, 0))
    full = lambda a: pl.BlockSpec(a.shape, lambda i: (0,) * a.ndim)
    if emit_qblk:
        q_shape = jax.ShapeDtypeStruct((n // LANE, KV_WIDTH, N_HEADS * LANE), BF16)
        q_spec = pl.BlockSpec((tm // LANE, KV_WIDTH, N_HEADS * LANE), lambda i: (i, 0, 0))
    else:
        q_shape = jax.ShapeDtypeStruct((N_HEADS * HEAD_DIM, n), BF16)
        q_spec = pl.BlockSpec((N_HEADS * HEAD_DIM, tm), lambda i: (0, i))
    kv = jax.ShapeDtypeStruct((n, KV_WIDTH), F32)
    out_shape = (jax.ShapeDtypeStruct((n, 512), BF16), jax.ShapeDtypeStruct((n, 512), F32),
                 kv, kv, kv, kv, kv, kv, q_shape, jax.ShapeDtypeStruct((32, n), F32))
    out_specs = (row(512), row(512), row(128), row(128), row(128), row(128), row(128), row(128), q_spec,
                 pl.BlockSpec((32, tm), lambda i: (0, i)))
    in_specs = [row(dm), full(gmix), full(wstd), full(wt), full(gsgu), full(wsgu), full(bsgu),
                row(128), row(128), pl.BlockSpec((32, tm), lambda i: (0, i)), pl.BlockSpec((32, tm), lambda i: (0, i))]
    return pl.pallas_call(
        functools.partial(_inproj_kernel, tm=tm, emit_qblk=emit_qblk),
        grid=(nt,), in_specs=in_specs, out_specs=out_specs, out_shape=out_shape,
        compiler_params=_cparams(("parallel",), VMEM_LIMIT), name="inproj",
    )(x, gmix, wstd, wt, gsgu, wsgu, bsgu, cos, sin, cost, sint)


def _compress_kernel(c_ref, wbig_ref, pe_ref, w1_ref, w2_ref, o_ref, *, transpose_out):
    c = c_ref[0].astype(BF16)
    ab = _dot(c, wbig_ref[...])
    nch = ab.shape[0]
    hw = N_KV_HEADS * CMP_HIDDEN
    peb = _dot(pe_ref[...], w1_ref[...])
    bias = jnp.concatenate([peb[0:1]] * N_KV_HEADS, axis=1)
    hh = ab[:, :hw] + pltpu.roll(ab[:, hw:], nch - 1, 0) + bias
    g = _gelu(hh).astype(BF16)
    if transpose_out:
        o_ref[0] = _dot_nt(w2_ref[...], g).astype(BF16)
    else:
        o_ref[0] = _dot(g, w2_ref[...]).astype(BF16)


def _compress(c, wbig, pe8, w1, w2, *, transpose_out):
    b, nch, cw = c.shape
    if transpose_out:
        out_shape = jax.ShapeDtypeStruct((b, KV_WIDTH, nch), BF16)
        out_spec = pl.BlockSpec((1, KV_WIDTH, nch), lambda i: (i, 0, 0))
    else:
        out_shape = jax.ShapeDtypeStruct((b, nch, KV_WIDTH), BF16)
        out_spec = pl.BlockSpec((1, nch, KV_WIDTH), lambda i: (i, 0, 0))
    full = lambda a: pl.BlockSpec(a.shape, lambda i: (0,) * a.ndim)
    return pl.pallas_call(
        functools.partial(_compress_kernel, transpose_out=transpose_out),
        grid=(b,), in_specs=[pl.BlockSpec((1, nch, cw), lambda i: (i, 0, 0)), full(wbig), full(pe8), full(w1), full(w2)],
        out_specs=out_spec, out_shape=out_shape,
        compiler_params=_cparams(("parallel",), VMEM_LIMIT), name="compress",
    )(c, wbig, pe8, w1, w2)


CMP_PAGES = 64


def _compress_paged_kernel(pt_ref, wbig_ref, pe_ref, w1_ref, w2_ref, *refs, transpose_out):
    del pt_ref
    g = len(refs) - 3
    pages, nxt, o_ref, scr = refs[:g], refs[g], refs[g + 1], refs[g + 2]
    page = scr.shape[0] // (g + 1)
    for i in range(g):
        scr[i * page:(i + 1) * page, :] = pages[i][0, 0].T
    scr[g * page:(g + 1) * page, :] = nxt[0, 0].T
    nchs = g * page // CMP_STRIDE
    rows = nchs + 8
    ab = None
    half = CMP_STRIDE // 2
    for s in range(half):
        a = jnp.concatenate([scr[pl.ds(s, rows, stride=CMP_STRIDE), :], scr[pl.ds(s + half, rows, stride=CMP_STRIDE), :]],
                            axis=1).astype(BF16)
        d = _dot(a, wbig_ref[s])
        ab = d if ab is None else ab + d
    hw = N_KV_HEADS * CMP_HIDDEN
    peb = _dot(pe_ref[...], w1_ref[...])
    bias = jnp.concatenate([peb[0:1]] * N_KV_HEADS, axis=1)
    hh = ab[0:nchs, :hw] + pltpu.roll(ab[:, hw:], rows - 1, 0)[0:nchs] + bias
    gg = _gelu(hh).astype(BF16)
    if transpose_out:
        o_ref[0] = _dot_nt(w2_ref[...], gg).astype(BF16)
    else:
        o_ref[0] = _dot(gg, w2_ref[...]).astype(BF16)


def _compress_paged(page_table, pool_t, layer, wbig, pe8, w1, w2, *, transpose_out):
    b, n_pages = page_table.shape
    page = pool_t.shape[3]
    g = min(CMP_PAGES, n_pages)
    n_steps = n_pages // g
    nchs = g * page // CMP_STRIDE
    wbig3 = wbig.reshape(CMP_STRIDE, KV_WIDTH, wbig.shape[1])
    wbig3 = jnp.concatenate([wbig3[:CMP_STRIDE // 2], wbig3[CMP_STRIDE // 2:]], axis=1)
    pspec = lambda i: pl.BlockSpec((1, 1, KV_WIDTH, page), lambda bi, j, pt, i=i: (layer, pt[bi, j * g + i], 0, 0))
    nspec = pl.BlockSpec((1, 1, KV_WIDTH, page),
                         lambda bi, j, pt: (layer, pt[bi, jnp.minimum(j * g + g, n_pages - 1)], 0, 0))
    full = lambda a: pl.BlockSpec(a.shape, lambda bi, j, pt: (0,) * a.ndim)
    if transpose_out:
        out_shape = jax.ShapeDtypeStruct((b, KV_WIDTH, n_steps * nchs), BF16)
        out_spec = pl.BlockSpec((1, KV_WIDTH, nchs), lambda bi, j, pt: (bi, 0, j))
    else:
        out_shape = jax.ShapeDtypeStruct((b, n_steps * nchs, KV_WIDTH), BF16)
        out_spec = pl.BlockSpec((1, nchs, KV_WIDTH), lambda bi, j, pt: (bi, j, 0))
    grid_spec = pltpu.PrefetchScalarGridSpec(
        num_scalar_prefetch=1, grid=(b, n_steps),
        in_specs=[full(wbig3), full(pe8), full(w1), full(w2)] + [pspec(i) for i in range(g)] + [nspec],
        out_specs=out_spec, scratch_shapes=[pltpu.VMEM(((g + 1) * page, KV_WIDTH), F32)])
    return pl.pallas_call(
        functools.partial(_compress_paged_kernel, transpose_out=transpose_out),
        grid_spec=grid_spec, out_shape=out_shape,
        compiler_params=_cparams(("parallel", "parallel"), VMEM_LIMIT), name="compress_paged",
    )(page_table, wbig3, pe8, w1, w2, *([pool_t] * (g + 1)))


def _cmp_select_kernel(q_ref, ck_ref, cvt_ref, mt_ref, t_ref, oc_ref, qaug_ref, sbt_ref, *, tq):
    L = N_HEADS * tq
    hl = L // N_KV_HEADS
    q1 = q_ref[0]
    t = t_ref[0]
    s = _dot(ck_ref[0], q1)
    nch = s.shape[0]
    n_last = (t - (CMP_LEN - 1)) // CMP_STRIDE
    s = jnp.where(lax.broadcasted_iota(I32, (nch, L), 0) <= n_last, s, NEG)
    m = jnp.max(s, axis=0, keepdims=True)
    p = jnp.exp2(s - m)
    inv = jnp.where(m > NEG / 2, 1.0 / jnp.maximum(jnp.sum(p, axis=0, keepdims=True), TINY), 0.0)
    p = p * inv
    pb = p.astype(BF16)
    mt = mt_ref[...]
    nsel = mt.shape[0]
    blk = lax.broadcasted_iota(I32, (nsel, tq), 0).astype(F32)
    cur = (t[:, 0:tq] // SEL_LEN).astype(F32)
    biases = []
    for h in range(N_KV_HEADS):
        o_h = _dot(cvt_ref[0, HEAD_DIM * h:HEAD_DIM * (h + 1), :], pb[:, h * hl:(h + 1) * hl])
        psum = p[:, h * hl:h * hl + tq]
        for g in range(GQA):
            hd = h * GQA + g
            oc_ref[0, HEAD_DIM * hd:HEAD_DIM * (hd + 1), :] = o_h[:, g * tq:(g + 1) * tq]
            if g > 0:
                psum = psum + p[:, hd * tq:(hd + 1) * tq]
        hi, lo = _split_bf16(psum)
        imp = _dot(mt, hi) + _dot(mt, lo)
        forced = (blk == 0.0) | (blk == cur) | (blk == cur - 1.0)
        past = blk <= cur
        sc = jnp.where(past & ~forced, imp, NEG)
        for _ in range(N_SEL - N_FORCED):
            mx = jnp.max(sc, axis=0, keepdims=True)
            idx = jnp.min(jnp.where(sc == mx, blk, 1e9), axis=0, keepdims=True)
            sc = jnp.where(blk == idx, PICKED, sc)
        sel = past & (forced | (sc < PICKED / 2) | (cur < float(N_SEL)))
        bias_h = jnp.where(sel, 0.0, NEG).astype(BF16)
        biases.extend([bias_h] * GQA)
    bias = jnp.concatenate(biases, axis=1)
    qaug_ref[0, 0:KV_WIDTH, :] = q1
    qaug_ref[0, KV_WIDTH:, :] = bias
    if sbt_ref is not None:
        assert L == LANE
        pad = jnp.zeros((LANE - BLOCKS_PER_STEP, L), F32)
        for js in range(sbt_ref.shape[1]):
            grp = bias[js * BLOCKS_PER_STEP:(js + 1) * BLOCKS_PER_STEP, :].astype(F32)
            sbt_ref[0, js] = jnp.concatenate([grp, pad], axis=0).T.astype(BF16)


def _cmp_select(qblk, ck, cvt, mt, tl, *, tq, nq_per_b, n_bias_steps=0):
    nq, _, L = qblk.shape
    b, nch, _ = ck.shape
    nsel = mt.shape[0]
    shared_t = tl.shape[0] == 1
    out_specs = [pl.BlockSpec((1, N_HEADS * HEAD_DIM, tq), lambda i, j: (i * nq_per_b + j, 0, 0)),
                 pl.BlockSpec((1, KV_WIDTH + nsel, L), lambda i, j: (i * nq_per_b + j, 0, 0))]
    out_shape = [jax.ShapeDtypeStruct((nq, N_HEADS * HEAD_DIM, tq), F32),
                 jax.ShapeDtypeStruct((nq, KV_WIDTH + nsel, L), BF16)]
    if n_bias_steps:
        out_specs.append(pl.BlockSpec((1, n_bias_steps, L, LANE), lambda i, j: (i * nq_per_b + j, 0, 0, 0)))
        out_shape.append(jax.ShapeDtypeStruct((nq, n_bias_steps, L, LANE), BF16))
        body = functools.partial(_cmp_select_kernel, tq=tq)
    else:
        body = lambda *refs: _cmp_select_kernel(*refs, None, tq=tq)
    return pl.pallas_call(
        body,
        grid=(b, nq_per_b),
        in_specs=[pl.BlockSpec((1, KV_WIDTH, L), lambda i, j: (i * nq_per_b + j, 0, 0)),
                  pl.BlockSpec((1, nch, KV_WIDTH), lambda i, j: (i, 0, 0)),
                  pl.BlockSpec((1, KV_WIDTH, nch), lambda i, j: (i, 0, 0)),
                  pl.BlockSpec(mt.shape, lambda i, j: (0, 0)),
                  pl.BlockSpec((1, 1, L), (lambda i, j: (0, 0, 0)) if shared_t else (lambda i, j: (j, 0, 0)))],
        out_specs=out_specs, out_shape=out_shape,
        compiler_params=_cparams(("parallel", "parallel"), VMEM_LIMIT), name="cmp_select",
    )(qblk, ck, cvt, mt, tl)


VSUM_ROWS = 16
VT_ROWS = HEAD_DIM + VSUM_ROWS
SEL_TK = 512


def _sel_prompt_kernel(qaug_ref, kaug_ref, vt_ref, t_ref, o_ref, m_sc, acc_sc, sa_sc, sb_sc, *, tq, tk):
    qi = pl.program_id(1)
    L = N_HEADS * tq
    hl = L // N_KV_HEADS
    m_sc[...] = jnp.full(m_sc.shape, NEG, F32)
    acc_sc[...] = jnp.zeros(acc_sc.shape, F32)
    last = (qi * tq + tq - 1) // tk

    def scores(ki, buf):
        buf[...] = _dot(kaug_ref[0, pl.ds(pl.multiple_of(ki * tk, tk), tk), :], qaug_ref[0])

    def consume(ki, buf, causal):
        s = buf[...]
        if causal:
            kpos = ki * tk + lax.broadcasted_iota(I32, s.shape, 0)
            s = jnp.where(kpos <= t_ref[0], s, NEG)
        m_old = m_sc[...]
        m_new = jnp.maximum(m_old, jnp.max(s, axis=0, keepdims=True))
        m_sc[...] = m_new
        alpha = jnp.exp2(m_old - m_new)
        pb = jnp.exp2(s - m_new).astype(BF16)
        for h in range(N_KV_HEADS):
            ls = slice(h * hl, (h + 1) * hl)
            acc_sc[h] = acc_sc[h] * alpha[:, ls] + _dot(vt_ref[ki, h * VT_ROWS:(h + 1) * VT_ROWS, :], pb[:, ls])

    scores(0, sa_sc)

    def body(j, carry):
        scores(2 * j + 1, sb_sc)
        consume(2 * j, sa_sc, False)
        scores(2 * j + 2, sa_sc)
        consume(2 * j + 1, sb_sc, False)
        return carry

    lax.fori_loop(0, last // 2, body, 0)

    @pl.when(last % 2 == 1)
    def _():
        scores(last, sb_sc)
        consume(last - 1, sa_sc, False)
        consume(last, sb_sc, True)

    @pl.when(last % 2 == 0)
    def _():
        consume(last, sa_sc, True)

    for hd in range(N_HEADS):
        h, g = divmod(hd, GQA)
        gs = slice(g * tq, (g + 1) * tq)
        linv = 1.0 / jnp.maximum(acc_sc[h, HEAD_DIM:HEAD_DIM + 1, gs], TINY)
        o_ref[0, HEAD_DIM * hd:HEAD_DIM * (hd + 1), :] = acc_sc[h, 0:HEAD_DIM, gs] * linv


def _sel_prompt(qaug, kaug, vt, tl, *, tq, tk, bsz):
    nq, r, L = qaug.shape
    nqb = nq // bsz
    tlen = kaug.shape[0] // bsz
    return pl.pallas_call(
        functools.partial(_sel_prompt_kernel, tq=tq, tk=tk),
        grid=(bsz, nqb),
        in_specs=[pl.BlockSpec((1, r, L), lambda i, j: (i * nqb + j, 0, 0)),
                  pl.BlockSpec((1, tlen, r), lambda i, j: (i, 0, 0)),
                  pl.BlockSpec((tlen // tk, N_KV_HEADS * VT_ROWS, tk), lambda i, j: (i, 0, 0)),
                  pl.BlockSpec((1, 1, L), lambda i, j: (j, 0, 0))],
        out_specs=pl.BlockSpec((1, N_HEADS * HEAD_DIM, tq), lambda i, j: (i * nqb + j, 0, 0)),
        out_shape=jax.ShapeDtypeStruct((nq, N_HEADS * HEAD_DIM, tq), F32),
        scratch_shapes=[pltpu.VMEM((1, L), F32), pltpu.VMEM((N_KV_HEADS, VT_ROWS, L // N_KV_HEADS), F32),
                        pltpu.VMEM((tk, L), F32), pltpu.VMEM((tk, L), F32)],
        compiler_params=_cparams(("parallel", "parallel"), VMEM_LIMIT), name="sel_prompt",
    )(qaug, kaug.reshape(bsz, tlen, r), vt, tl)


def _sel_prep_kernel(k_ref, v_ref, oh_ref, kaug_ref, vt_ref):
    kaug_ref[:, 0:KV_WIDTH] = k_ref[...].astype(BF16)
    kaug_ref[:, KV_WIDTH:] = oh_ref[...]
    vt = v_ref[...].T.astype(BF16)
    ones = jnp.ones((VSUM_ROWS, vt.shape[1]), BF16)
    for h in range(N_KV_HEADS):
        vt_ref[0, h * VT_ROWS:h * VT_ROWS + HEAD_DIM, :] = vt[h * HEAD_DIM:(h + 1) * HEAD_DIM, :]
        vt_ref[0, h * VT_ROWS + HEAD_DIM:(h + 1) * VT_ROWS, :] = ones


def _sel_prep(k, v, oh, *, tk):
    n = k.shape[0]
    tiles_per_seq = oh.shape[0] // tk
    nsel = oh.shape[1]
    return pl.pallas_call(
        _sel_prep_kernel, grid=(n // tk,),
        in_specs=[pl.BlockSpec((tk, KV_WIDTH), lambda i: (i, 0)), pl.BlockSpec((tk, KV_WIDTH), lambda i: (i, 0)),
                  pl.BlockSpec((tk, nsel), lambda i: (i % tiles_per_seq, 0))],
        out_specs=(pl.BlockSpec((tk, KV_WIDTH + nsel), lambda i: (i, 0)),
                   pl.BlockSpec((1, N_KV_HEADS * VT_ROWS, tk), lambda i: (i, 0, 0))),
        out_shape=(jax.ShapeDtypeStruct((n, KV_WIDTH + nsel), BF16),
                   jax.ShapeDtypeStruct((n // tk, N_KV_HEADS * VT_ROWS, tk), BF16)),
        compiler_params=_cparams(("parallel",)), name="sel_prep",
    )(k, v, oh)


SEL_PAGES = 32
BLOCKS_PER_STEP = 64


def _sel_paged_kernel(pt_ref, q_ref, sbt_ref, sbt_tail_ref, ohc_ref, trow_ref, knew_ref, vnew_ref, *refs, past_len):
    del pt_ref
    g = SEL_PAGES
    kpages, vpages, o_ref, m_sc, l_sc, acc_sc, kcat_sc, vcat_sc = refs[:g], refs[g:2 * g], refs[2 * g], *refs[2 * g + 1:]
    js = pl.program_id(1)
    q = q_ref[0]

    @pl.when(js == 0)
    def _():
        m_sc[...] = jnp.full(m_sc.shape, NEG, F32)
        l_sc[...] = jnp.zeros(l_sc.shape, F32)
        acc_sc[...] = jnp.zeros(acc_sc.shape, F32)

    def update(s, vt):
        m_old = m_sc[...]
        m_new = jnp.maximum(m_old, jnp.max(s, axis=1, keepdims=True))
        alpha = jnp.exp2(m_old - m_new)
        p = jnp.exp2(s - m_new)
        l_sc[...] = alpha * l_sc[...] + jnp.sum(p, axis=1, keepdims=True)
        m_sc[...] = m_new
        acc_sc[...] = acc_sc[...] * alpha + _dot_nt(p.astype(BF16), vt)

    for i in range(g):
        kcat_sc[:, i * LANE:(i + 1) * LANE] = kpages[i][0, 0].astype(BF16)
        vcat_sc[:, i * LANE:(i + 1) * LANE] = vpages[i][0, 0].astype(BF16)
    ohc = ohc_ref[...]
    s = _dot(q, kcat_sc[...]) + _dot(sbt_ref[0, 0], ohc)
    update(s, vcat_sc[...])

    @pl.when(js == pl.num_programs(1) - 1)
    def _():
        sn = _dot(q, knew_ref[0].astype(BF16)) + _dot(sbt_tail_ref[0, 0], ohc[:, 0:LANE])
        kpos = past_len + lax.broadcasted_iota(I32, sn.shape, 1)
        sn = jnp.where(kpos <= trow_ref[...], sn, NEG)
        update(sn, vnew_ref[0].astype(BF16))
        o_ref[0] = acc_sc[...] * (1.0 / jnp.maximum(l_sc[...], TINY))


def _sel_paged(page_table, qstd, sbt, ohc, trow, knew_t, vnew_t, pool_k, pool_v, layer, *, past_len):
    b, rows, _ = qstd.shape
    g = SEL_PAGES
    page = pool_k.shape[3]
    n_steps = page_table.shape[1] // g
    pspec = lambda i: pl.BlockSpec((1, 1, KV_WIDTH, page), lambda bi, j, pt, i=i: (layer, pt[bi, j * g + i], 0, 0))
    per_b = lambda a: pl.BlockSpec((1,) + a.shape[1:], lambda bi, j, pt: (bi,) + (0,) * (a.ndim - 1))
    full = lambda a: pl.BlockSpec(a.shape, lambda bi, j, pt: (0,) * a.ndim)
    grid_spec = pltpu.PrefetchScalarGridSpec(
        num_scalar_prefetch=1, grid=(b, n_steps),
        in_specs=[per_b(qstd),
                  pl.BlockSpec((1, 1) + sbt.shape[2:], lambda bi, j, pt: (bi, j, 0, 0)),
                  pl.BlockSpec((1, 1) + sbt.shape[2:], lambda bi, j, pt: (bi, n_steps, 0, 0)),
                  full(ohc), full(trow), per_b(knew_t), per_b(vnew_t)]
                 + [pspec(i) for i in range(g)] + [pspec(i) for i in range(g)],
        out_specs=pl.BlockSpec((1, rows, KV_WIDTH), lambda bi, j, pt: (bi, 0, 0)),
        scratch_shapes=[pltpu.VMEM((rows, 1), F32), pltpu.VMEM((rows, 1), F32), pltpu.VMEM((rows, KV_WIDTH), F32),
                        pltpu.VMEM((KV_WIDTH, g * page), BF16), pltpu.VMEM((KV_WIDTH, g * page), BF16)])
    return pl.pallas_call(
        functools.partial(_sel_paged_kernel, past_len=past_len),
        grid_spec=grid_spec, out_shape=jax.ShapeDtypeStruct((b, rows, KV_WIDTH), F32),
        compiler_params=_cparams(("parallel", "arbitrary"), VMEM_LIMIT), name="sel_paged",
    )(page_table, qstd, sbt, sbt, ohc, trow, knew_t, vnew_t, *([pool_k] * g), *([pool_v] * g))


def _window_kernel(q_ref, k_ref, v_ref, t_ref, o_ref, *, tq, band, kpos_base):
    qi = pl.program_id(1)
    L = N_HEADS * tq
    hl = L // N_KV_HEADS
    q1 = q_ref[0]
    t = t_ref[0]
    start = pl.multiple_of(jnp.maximum(qi * tq + tq - band, 0), LANE)
    kb = k_ref[0, pl.ds(start, band), :].astype(BF16)
    s = _dot(kb, q1)
    r_hi = t - (kpos_base + start)
    row = lax.broadcasted_iota(I32, s.shape, 0)
    s = jnp.where((row <= r_hi) & (row > r_hi - WINDOW), s, NEG)
    m = jnp.max(s, axis=0, keepdims=True)
    pb = jnp.exp2(s - m).astype(BF16)
    keep = m > NEG / 2
    vt = v_ref[0, pl.ds(start, band), :].T.astype(BF16)
    ones = jnp.ones((VSUM_ROWS, band), BF16)
    for h in range(N_KV_HEADS):
        vth = jnp.concatenate([vt[HEAD_DIM * h:HEAD_DIM * (h + 1), :], ones], axis=0)
        o_h = _dot(vth, pb[:, h * hl:(h + 1) * hl])
        for g in range(GQA):
            hd = h * GQA + g
            gs = slice(g * tq, (g + 1) * tq)
            linv = jnp.where(keep[:, hd * tq:(hd + 1) * tq], 1.0 / jnp.maximum(o_h[HEAD_DIM:HEAD_DIM + 1, gs], TINY), 0.0)
            o_ref[0, HEAD_DIM * hd:HEAD_DIM * (hd + 1), :] = o_h[0:HEAD_DIM, gs] * linv


def _window(qblk, k, v, tl, *, tq, band, kpos_base):
    nq, _, L = qblk.shape
    b, tlen, _ = k.shape
    nqb = nq // b
    shared_t = tl.shape[0] == 1
    return pl.pallas_call(
        functools.partial(_window_kernel, tq=tq, band=band, kpos_base=kpos_base),
        grid=(b, nqb),
        in_specs=[pl.BlockSpec((1, KV_WIDTH, L), lambda i, j: (i * nqb + j, 0, 0)),
                  pl.BlockSpec((1, tlen, KV_WIDTH), lambda i, j: (i, 0, 0)),
                  pl.BlockSpec((1, tlen, KV_WIDTH), lambda i, j: (i, 0, 0)),
                  pl.BlockSpec((1, 1, L), (lambda i, j: (0, 0, 0)) if shared_t else (lambda i, j: (j, 0, 0)))],
        out_specs=pl.BlockSpec((1, N_HEADS * HEAD_DIM, tq), lambda i, j: (i * nqb + j, 0, 0)),
        out_shape=jax.ShapeDtypeStruct((nq, N_HEADS * HEAD_DIM, tq), F32),
        compiler_params=_cparams(("parallel", "parallel"), VMEM_LIMIT), name="window",
    )(qblk, k, v, tl)


def _combine_kernel(x_ref, a_ref, oc_ref, os_ref, ow_ref, gt_ref, e_ref, woa_ref, wob_ref, o_ref, *, tm):
    hi, lo = _split_bf16(gt_ref[...])
    ge = _dot(e_ref[...], hi) + _dot(e_ref[...], lo)
    bw = N_HEADS * HEAD_DIM
    parts = []
    for j in range(tm // LANE):
        ls = slice(j * LANE, (j + 1) * LANE)
        parts.append(ge[0:bw, ls] * oc_ref[j] + ge[bw:2 * bw, ls] * os_ref[j] + ge[2 * bw:3 * bw, ls] * ow_ref[j])
    mixt = parts[0] if len(parts) == 1 else jnp.concatenate(parts, axis=1)
    mix = mixt.T.astype(BF16)
    o_ref[...] = x_ref[...] + _dot(a_ref[...], woa_ref[...]) + _dot(mix, wob_ref[...])


def _combine(x, a, oc, os_, ow, gt, e, woa, wob, *, tm):
    n, dm = x.shape
    bw = N_HEADS * HEAD_DIM
    full = lambda arr: pl.BlockSpec(arr.shape, lambda i: (0,) * arr.ndim)
    ospec = pl.BlockSpec((tm // LANE, bw, LANE), lambda i: (i, 0, 0))
    return pl.pallas_call(
        functools.partial(_combine_kernel, tm=tm),
        grid=(n // tm,),
        in_specs=[pl.BlockSpec((tm, dm), lambda i: (i, 0)), pl.BlockSpec((tm, a.shape[1]), lambda i: (i, 0)),
                  ospec, ospec, ospec, pl.BlockSpec((32, tm), lambda i: (0, i)), full(e), full(woa), full(wob)],
        out_specs=pl.BlockSpec((tm, dm), lambda i: (i, 0)),
        out_shape=jax.ShapeDtypeStruct((n, dm), F32),
        compiler_params=_cparams(("parallel",), VMEM_LIMIT), name="combine",
    )(x, a, oc, os_, ow, gt, e, woa, wob)


def _ffn_dense_kernel(x_ref, g_ref, wg_ref, wu_ref, wd_ref, o_ref):
    xf = x_ref[...]
    h = _rms(xf, g_ref[...]).astype(BF16)
    gate = _dot(h, wg_ref[...])
    up = _dot(h, wu_ref[...])
    act = (gate * _sigmoid(gate) * up).astype(BF16)
    o_ref[...] = xf + _dot(act, wd_ref[...])


def _ffn_dense(x, g, wg, wu, wd, *, tm):
    n, dm = x.shape
    full = lambda arr: pl.BlockSpec(arr.shape, lambda i: (0,) * arr.ndim, pipeline_mode=pl.Buffered(1))
    return pl.pallas_call(
        _ffn_dense_kernel, grid=(n // tm,),
        in_specs=[pl.BlockSpec((tm, dm), lambda i: (i, 0)), full(g), full(wg), full(wu), full(wd)],
        out_specs=pl.BlockSpec((tm, dm), lambda i: (i, 0)),
        out_shape=jax.ShapeDtypeStruct((n, dm), F32),
        compiler_params=_cparams(("parallel",), VMEM_LIMIT), name="ffn_dense",
    )(x, g, wg, wu, wd)


def _router_kernel(x_ref, g_ref, wrh_ref, wrl_ref, tri_ref, h_ref, meta_ref, cnt_ref, *, tm, n_exp):
    h = _rms(x_ref[...], g_ref[...])
    h_ref[...] = h
    hi, lo = _split_bf16(h)
    logits = _dot(hi, wrh_ref[...]) + _dot(lo, wrh_ref[...]) + _dot(hi, wrl_ref[...])
    lane = lax.broadcasted_iota(I32, (tm, LANE), 1)
    lanef = lane.astype(F32)
    logits = jnp.where(lane < n_exp, logits, NEG)
    m1 = jnp.max(logits, axis=1, keepdims=True)
    i1 = jnp.min(jnp.where(logits == m1, lanef, 1e9), axis=1, keepdims=True)
    rest = jnp.where(lanef == i1, -3e38, logits)
    m2 = jnp.max(rest, axis=1, keepdims=True)
    i2 = jnp.min(jnp.where(rest == m2, lanef, 1e9), axis=1, keepdims=True)
    e2 = jnp.exp(m2 - m1)
    w1 = 1.0 / (1.0 + e2)
    w2 = e2 / (1.0 + e2)
    hit1 = lanef == i1
    hit2 = lanef == i2
    msel = jnp.where(hit1 | hit2, 1.0, 0.0)
    ranks = _dot(tri_ref[...], msel.astype(BF16))
    r1 = jnp.sum(jnp.where(hit1, ranks, 0.0), axis=1, keepdims=True)
    r2 = jnp.sum(jnp.where(hit2, ranks, 0.0), axis=1, keepdims=True)
    meta = jnp.where(lane == 0, i1, 0.0)
    for k, val in enumerate((i2, w1, w2, r1, r2)):
        meta = jnp.where(lane == k + 1, val, meta)
    meta_ref[...] = meta
    cnt_ref[0] = jnp.sum(msel, axis=0, keepdims=True)


def _router(x, g, wrh, wrl, tri, *, tm, n_exp):
    n, dm = x.shape
    full = lambda arr: pl.BlockSpec(arr.shape, lambda i: (0,) * arr.ndim)
    return pl.pallas_call(
        functools.partial(_router_kernel, tm=tm, n_exp=n_exp), grid=(n // tm,),
        in_specs=[pl.BlockSpec((tm, dm), lambda i: (i, 0)), full(g), full(wrh), full(wrl), full(tri)],
        out_specs=(pl.BlockSpec((tm, dm), lambda i: (i, 0)), pl.BlockSpec((tm, LANE), lambda i: (i, 0)),
                   pl.BlockSpec((1, 1, LANE), lambda i: (i, 0, 0))),
        out_shape=(jax.ShapeDtypeStruct((n, dm), F32), jax.ShapeDtypeStruct((n, LANE), F32),
                   jax.ShapeDtypeStruct((n // tm, 1, LANE), F32)),
        compiler_params=_cparams(("parallel",), VMEM_LIMIT), name="router",
    )(x, g, wrh, wrl, tri)


def _row_copy(src, src_row, dst, dst_row, sem):
    return pltpu.make_async_copy(src.at[pl.ds(src_row, 1)], dst.at[pl.ds(dst_row, 1)], sem)


def _dispatch_kernel(dest_ref, h_ref, xg_in_ref, xg_ref, sem, *, tm):
    del xg_in_ref

    def issue(r, carry):
        for k in range(TOP_K):
            _row_copy(h_ref, r, xg_ref, dest_ref[TOP_K * r + k], sem).start()
        return carry

    lax.fori_loop(0, tm, issue, 0)

    def drain(r, carry):
        for k in range(TOP_K):
            _row_copy(h_ref, 0, xg_ref, 0, sem).wait()
        return carry

    lax.fori_loop(0, tm, drain, 0)


def _dispatch(dest, h, xg0, *, tm):
    n, dm = h.shape
    return pl.pallas_call(
        functools.partial(_dispatch_kernel, tm=tm), grid=(n // tm,),
        in_specs=[pl.BlockSpec((TOP_K * tm,), lambda i: (i,), memory_space=pltpu.SMEM),
                  pl.BlockSpec((tm, dm), lambda i: (i, 0)),
                  pl.BlockSpec(memory_space=pl.ANY)],
        out_specs=pl.BlockSpec(memory_space=pl.ANY),
        out_shape=jax.ShapeDtypeStruct(xg0.shape, F32),
        scratch_shapes=[pltpu.SemaphoreType.DMA],
        input_output_aliases={2: 0},
        compiler_params=_cparams(("arbitrary",), VMEM_LIMIT), name="moe_dispatch",
    )(dest, h, xg0)


def _expert_ffn_kernel(te_ref, nused_ref, x_ref, wg_ref, wu_ref, wd_ref, o_ref):
    s = pl.program_id(0)

    @pl.when(s < nused_ref[0])
    def _():
        h = x_ref[...].astype(BF16)
        gate = _dot(h, wg_ref[0])
        up = _dot(h, wu_ref[0])
        act = (gate * _sigmoid(gate) * up).astype(BF16)
        o_ref[...] = _dot(act, wd_ref[0])

    @pl.when(s >= nused_ref[0])
    def _():
        o_ref[...] = jnp.zeros(o_ref.shape, F32)


def _expert_ffn(te, nused, xg, wg, wu, wd, *, ts):
    s_rows, dm = xg.shape
    dff = wg.shape[2]
    grid_spec = pltpu.PrefetchScalarGridSpec(
        num_scalar_prefetch=2, grid=(s_rows // ts,),
        in_specs=[pl.BlockSpec((ts, dm), lambda i, te, nu: (i, 0)),
                  pl.BlockSpec((1, dm, dff), lambda i, te, nu: (te[i], 0, 0), pipeline_mode=pl.Buffered(1)),
                  pl.BlockSpec((1, dm, dff), lambda i, te, nu: (te[i], 0, 0), pipeline_mode=pl.Buffered(1)),
                  pl.BlockSpec((1, dff, dm), lambda i, te, nu: (te[i], 0, 0), pipeline_mode=pl.Buffered(1))],
        out_specs=pl.BlockSpec((ts, dm), lambda i, te, nu: (i, 0)))
    return pl.pallas_call(
        _expert_ffn_kernel, grid_spec=grid_spec,
        out_shape=jax.ShapeDtypeStruct((s_rows, dm), F32),
        compiler_params=_cparams(("arbitrary",), VMEM_LIMIT), name="expert_ffn",
    )(te, nused, xg, wg, wu, wd)


def _moe_combine_kernel(dest_ref, x_ref, meta_ref, y_ref, *rest, tm):
    gf_ref = rest[0] if len(rest) == 4 else None
    o_ref, buf, sem = rest[-3:]

    def issue(r, carry):
        for k in range(TOP_K):
            _row_copy(y_ref, dest_ref[TOP_K * r + k], buf.at[k], r, sem).start()
        return carry

    lax.fori_loop(0, tm, issue, 0)

    def drain(r, carry):
        for k in range(TOP_K):
            _row_copy(y_ref, 0, buf.at[k], 0, sem).wait()
        return carry

    lax.fori_loop(0, tm, drain, 0)
    meta = meta_ref[...]
    out = x_ref[...] + meta[:, 2:3] * buf[0] + meta[:, 3:4] * buf[1]
    o_ref[...] = out if gf_ref is None else _rms(out, gf_ref[...])


def _moe_combine(dest, x, meta, y, gf, *, tm):
    n, dm = x.shape
    extra = [] if gf is None else [gf]
    return pl.pallas_call(
        functools.partial(_moe_combine_kernel, tm=tm), grid=(n // tm,),
        in_specs=[pl.BlockSpec((TOP_K * tm,), lambda i: (i,), memory_space=pltpu.SMEM),
                  pl.BlockSpec((tm, dm), lambda i: (i, 0)),
                  pl.BlockSpec((tm, LANE), lambda i: (i, 0)),
                  pl.BlockSpec(memory_space=pl.ANY)] + [pl.BlockSpec(a.shape, lambda i: (0, 0)) for a in extra],
        out_specs=pl.BlockSpec((tm, dm), lambda i: (i, 0)),
        out_shape=jax.ShapeDtypeStruct((n, dm), F32),
        scratch_shapes=[pltpu.VMEM((TOP_K, tm, dm), F32), pltpu.SemaphoreType.DMA],
        compiler_params=_cparams(("arbitrary",), VMEM_LIMIT), name="moe_combine",
    )(dest, x, meta, y, *extra)


def _moe(x, g, wr, wg, wu, wd, gf, *, tm, ts):
    n, dm = x.shape
    n_exp = wr.shape[1]
    wr_pad = jnp.zeros((dm, LANE), F32).at[:, :n_exp].set(wr)
    wrh, wrl = _split_bf16(wr_pad)
    tri = jnp.tril(jnp.ones((tm, tm), F32), -1).astype(BF16)
    h, meta, cnt = _router(x, g, wrh, wrl, tri, tm=tm, n_exp=n_exp)
    cnt = cnt[:, 0, :n_exp].astype(I32)
    tile_off = jnp.cumsum(cnt, axis=0) - cnt
    tot = jnp.sum(cnt, axis=0)
    padded = ((tot + ts - 1) // ts) * ts
    gend = jnp.cumsum(padded)
    gstart = gend - padded
    ei = meta[:, 0:2].astype(I32)
    rk = meta[:, 4:6].astype(I32)
    base = jnp.broadcast_to((gstart[None, :] + tile_off)[:, None, :], (n // tm, tm, n_exp)).reshape(n, 1, n_exp)
    pick = ei[:, :, None] == jnp.arange(n_exp, dtype=I32)[None, None, :]
    dest = (jnp.sum(jnp.where(pick, base, 0), axis=-1) + rk).reshape(-1)
    n_slot_tiles = -(-(TOP_K * n + n_exp * (ts - 1)) // ts)
    n_slots = n_slot_tiles * ts
    nused = (gend[-1] // ts).astype(I32).reshape(1)
    tile_start = jnp.minimum(jnp.arange(n_slot_tiles, dtype=I32), jnp.maximum(nused[0] - 1, 0)) * ts
    te = jnp.minimum(jnp.sum((gend[None, :] <= tile_start[:, None]).astype(I32), axis=1), n_exp - 1)
    xg = _dispatch(dest, h, jnp.zeros((n_slots, dm), F32), tm=tm)
    y = _expert_ffn(te, nused, xg, wg, wu, wd, ts=ts)
    return _moe_combine(dest, x, meta, y, gf, tm=tm)


def _norm_kernel(x_ref, g_ref, o_ref):
    o_ref[...] = _rms(x_ref[...], g_ref[...])


def _final_norm(x, g, *, tm):
    n, dm = x.shape
    return pl.pallas_call(
        _norm_kernel, grid=(n // tm,),
        in_specs=[pl.BlockSpec((tm, dm), lambda i: (i, 0)), pl.BlockSpec(g.shape, lambda i: (0, 0))],
        out_specs=pl.BlockSpec((tm, dm), lambda i: (i, 0)),
        out_shape=jax.ShapeDtypeStruct((n, dm), F32),
        compiler_params=_cparams(("parallel",)), name="final_norm",
    )(x, g)


def _rope_tables(pos):
    half = HEAD_DIM // 2
    inv = ROPE_THETA ** (-jnp.arange(half, dtype=F32) / half)
    ang = pos.astype(F32)[:, None] * inv[None, :]
    cos, sin = jnp.cos(ang), jnp.sin(ang)
    cos_std = jnp.tile(cos, (1, 4))
    sin_std = jnp.tile(jnp.concatenate([-sin, sin], axis=1), (1, 2))
    return cos_std, sin_std, cos.T, sin.T


def _cmp_to_sel_t(n_cmp_pad, n_sel_pad):
    i0 = np.arange(n_cmp_pad, dtype=np.int64)[None, :] * CMP_STRIDE
    j0 = np.arange(n_sel_pad, dtype=np.int64)[:, None] * SEL_LEN
    ov = np.clip(np.minimum(i0 + CMP_LEN, j0 + SEL_LEN) - np.maximum(i0, j0), 0, CMP_LEN)
    return jnp.asarray(ov.astype(np.float32) / CMP_LEN, dtype=BF16)


def _block_onehot(n_keys, n_sel_pad):
    blk = np.arange(n_keys)[:, None] // SEL_LEN
    return jnp.asarray((blk == np.arange(n_sel_pad)[None, :]).astype(np.float32), dtype=BF16)


def _lane_positions(pos_tiles, tq):
    return jnp.tile(pos_tiles.astype(I32), (1, N_HEADS))[:, None, :]


def _compress_weights(w1, w2, pe):
    nr = CMP_LEN // CMP_STRIDE
    w1r = w1.reshape(nr, CMP_STRIDE, HEAD_DIM, CMP_HIDDEN)
    eye = jnp.eye(N_KV_HEADS, dtype=F32)
    wbig = jnp.einsum("rsde,gh->sgdrhe", w1r, eye)
    wbig = wbig.reshape(CMP_STRIDE * KV_WIDTH, nr * N_KV_HEADS * CMP_HIDDEN).astype(BF16)
    w2big = jnp.einsum("ed,gh->gehd", w2, eye).reshape(N_KV_HEADS * CMP_HIDDEN, KV_WIDTH).astype(BF16)
    pe8 = jnp.tile(pe.reshape(1, CMP_LEN * HEAD_DIM), (8, 1)).astype(BF16)
    return wbig, pe8, w1.astype(BF16), w2big


def _layer_weights(l, g_mix, w_in, g_sgu, w_sgu, b_sgu, w_cmpk1, w_cmpk2, pe_cmpk, w_cmpv1, w_cmpv2, pe_cmpv, w_o, dec_t):
    aw = A_GROUPS * CHUNK
    bw = N_HEADS * HEAD_DIM
    w = w_in[l]
    wstd = jnp.concatenate([w[:, :2 * aw], w[:, 2 * aw + bw:2 * aw + bw + 6 * KV_WIDTH]], axis=1).astype(BF16)
    n_gate = 3 * N_HEADS
    wt = jnp.concatenate([w[:, 2 * aw:2 * aw + bw], w[:, -n_gate:], jnp.zeros((w.shape[0], 32 - n_gate), F32)], axis=1).T.astype(BF16)
    causal = jnp.tril(jnp.ones((CHUNK, CHUNK), bool))
    wsgu_p = jnp.where(causal[None], w_sgu[l], 0.0).astype(BF16)
    bsgu_p = jnp.broadcast_to(b_sgu[l][:, :, None], (A_GROUPS, CHUNK, CHUNK)).astype(F32)
    reps = CHUNK // dec_t
    wsmall = jnp.where(causal[None, :dec_t, :dec_t], w_sgu[l][:, :dec_t, :dec_t], 0.0)
    wsgu_s = jnp.einsum("ab,gts->gatbs", jnp.eye(reps, dtype=F32), wsmall).reshape(A_GROUPS, CHUNK, CHUNK).astype(BF16)
    bsgu_s = jnp.broadcast_to(jnp.tile(b_sgu[l][:, :dec_t], (1, reps))[:, :, None], (A_GROUPS, CHUNK, CHUNK)).astype(F32)
    ck_w = _compress_weights(w_cmpk1[l], w_cmpk2[l], pe_cmpk[l])
    cv_w = _compress_weights(w_cmpv1[l], w_cmpv2[l], pe_cmpv[l])
    cv_w = cv_w[:3] + (cv_w[3].T,)
    e = np.zeros((3 * bw, 32), np.float32)
    for br in range(3):
        for hd in range(N_HEADS):
            e[br * bw + hd * HEAD_DIM:br * bw + (hd + 1) * HEAD_DIM, br * N_HEADS + hd] = 1.0
    return dict(gmix=g_mix[l][None, :], wstd=wstd, wt=wt, gsgu=g_sgu[l][None, :],
                wsgu_p=wsgu_p, bsgu_p=bsgu_p, wsgu_s=wsgu_s, bsgu_s=bsgu_s, ck_w=ck_w, cv_w=cv_w,
                e=jnp.asarray(e, dtype=BF16), woa=w_o[l][:aw].astype(BF16), wob=w_o[l][aw:].astype(BF16))


def _channel_mixer(l, x, g_ffn, dense_w, w_router, moe_w, gf, *, tm_dense, tm_moe):
    g = g_ffn[l][None, :]
    i = l // 2
    if l % 2 == 0:
        y = _ffn_dense(x, g, *(w[i] for w in dense_w), tm=tm_dense)
        return y if gf is None else _final_norm(y, gf, tm=tm_dense)
    return _moe(x, g, w_router[i], *(w[i] for w in moe_w), gf, tm=tm_moe, ts=512)


def kernel(x_prompt, x_sample, cache_cmp_k, cache_cmp_v, cache_sel_k, cache_sel_v, cache_win_k, cache_win_v, page_table,
           g_mix, w_in, g_sgu, w_sgu, b_sgu, w_cmpk1, w_cmpk2, pe_cmpk, w_cmpv1, w_cmpv2, pe_cmpv, w_o, g_ffn,
           w_ff_gate, w_ff_up, w_ff_down, w_router, w_moe_gate, w_moe_up, w_moe_down, g_final):
    bsz, seq, dm = x_prompt.shape
    dec_b, dec_t, _ = x_sample.shape
    depth = g_mix.shape[0]
    page = cache_cmp_k.shape[2]
    past_len = page_table.shape[1] * page
    wbuf = cache_win_k.shape[2]
    n_p, n_s = bsz * seq, dec_b * dec_t
    assert n_s == CHUNK and seq % 512 == 0 and wbuf == WINDOW and past_len % 2048 == 0
    tq_p, tq_s = LANE, LANE // N_HEADS
    nq_b = seq // tq_p
    band = WINDOW + tq_p

    pos_p = jnp.tile(jnp.arange(seq, dtype=I32), bsz)
    pos_s = jnp.tile(past_len + jnp.arange(dec_t, dtype=I32), dec_b)
    rope_p = _rope_tables(pos_p)
    rope_s = _rope_tables(pos_s)
    tl_p = _lane_positions(jnp.arange(seq, dtype=I32).reshape(nq_b, tq_p), tq_p)
    tok_s = past_len + jnp.minimum(jnp.arange(tq_s, dtype=I32), dec_t - 1)
    tl_s = _lane_positions(tok_s[None, :], tq_s)
    nch_p = seq // CMP_STRIDE
    nsel_p = seq // SEL_LEN
    nch_s = past_len // CMP_STRIDE
    nsel_s = -(-(past_len // SEL_LEN + 1) // LANE) * LANE
    mt_p = _cmp_to_sel_t(nch_p, nsel_p)
    mt_s = _cmp_to_sel_t(nch_s, nsel_s)
    oh_p = _block_onehot(seq, nsel_p)
    n_sel_steps = past_len // (SEL_PAGES * page) + 1
    assert nsel_s >= n_sel_steps * BLOCKS_PER_STEP and SEL_PAGES * page == BLOCKS_PER_STEP * SEL_LEN
    ohc_s = _block_onehot(SEL_PAGES * page, LANE).T
    trow_s = jnp.broadcast_to(jnp.tile(tok_s, N_HEADS)[:, None], (N_HEADS * tq_s, LANE)).astype(I32)
    head_of_row = jnp.asarray((np.arange(N_HEADS)[None, :] // GQA == np.arange(N_KV_HEADS)[:, None]).astype(np.float32))

    pools_t = [jnp.transpose(c, (0, 1, 3, 4, 2)).reshape(depth, c.shape[1], KV_WIDTH, page)
               for c in (cache_cmp_k, cache_cmp_v, cache_sel_k, cache_sel_v)]
    flat = lambda a: a.reshape(a.shape[0], a.shape[1], KV_WIDTH)

    dense_w = tuple(w.astype(BF16) for w in (w_ff_gate, w_ff_up, w_ff_down))
    moe_w = tuple(w.astype(BF16) for w in (w_moe_gate, w_moe_up, w_moe_down))
    xp = x_prompt.reshape(n_p, dm)
    xs = x_sample.reshape(n_s, dm)
    outs = [[] for _ in range(13)]
    for l in range(depth):
        lw = _layer_weights(l, g_mix, w_in, g_sgu, w_sgu, b_sgu, w_cmpk1, w_cmpk2, pe_cmpk, w_cmpv1, w_cmpv2, pe_cmpv, w_o, dec_t)
        a, _, kc, vc, ks, vs, kw, vw, qblk, gt = _inproj(
            xp, lw["gmix"], lw["wstd"], lw["wt"], lw["gsgu"], lw["wsgu_p"], lw["bsgu_p"], *rope_p, tm=512, emit_qblk=True)
        ck = _compress(kc.reshape(bsz, nch_p, CMP_STRIDE * KV_WIDTH), *lw["ck_w"], transpose_out=False)
        cvt = _compress(vc.reshape(bsz, nch_p, CMP_STRIDE * KV_WIDTH), *lw["cv_w"], transpose_out=True)
        oc, qaug = _cmp_select(qblk, ck, cvt, mt_p, tl_p, tq=tq_p, nq_per_b=nq_b)
        kaug, vst = _sel_prep(ks, vs, oh_p, tk=SEL_TK)
        osel = _sel_prompt(qaug, kaug, vst, tl_p, tq=tq_p, tk=SEL_TK, bsz=bsz)
        ow = _window(qblk, kw.reshape(bsz, seq, KV_WIDTH), vw.reshape(bsz, seq, KV_WIDTH), tl_p, tq=tq_p, band=band, kpos_base=0)
        xp = _combine(xp, a, oc, osel, ow, gt, lw["e"], lw["woa"], lw["wob"], tm=512)
        kv4 = lambda t: t.reshape(bsz, seq, N_KV_HEADS, HEAD_DIM)
        for idx, t in enumerate((kc, vc, ks, vs)):
            outs[idx].append(kv4(t))
        nwin_p = min(WINDOW, seq)
        outs[4].append(kv4(kw)[:, seq - nwin_p:])
        outs[5].append(kv4(vw)[:, seq - nwin_p:])

        a, v_s, kc, vc, ks, vs, kw, vw, qt, gt = _inproj(
            xs, lw["gmix"], lw["wstd"], lw["wt"], lw["gsgu"], lw["wsgu_s"], lw["bsgu_s"], *rope_s, tm=CHUNK, emit_qblk=False)
        q4 = qt.reshape(N_HEADS, HEAD_DIM, dec_b, dec_t)
        q4 = jnp.pad(q4, ((0, 0), (0, 0), (0, 0), (0, tq_s - dec_t)))
        qb = jnp.einsum("hdbt,gh->bgdht", q4.astype(F32), head_of_row).astype(BF16)
        qblk_s = qb.reshape(dec_b, KV_WIDTH, N_HEADS * tq_s)
        ck = _compress_paged(page_table, pools_t[0], l, *lw["ck_w"], transpose_out=False)
        cvt = _compress_paged(page_table, pools_t[1], l, *lw["cv_w"], transpose_out=True)
        oc, _, sbt = _cmp_select(qblk_s, ck, cvt, mt_s, tl_s, tq=tq_s, nq_per_b=1, n_bias_steps=n_sel_steps)
        new_t = lambda t: jnp.pad(t.reshape(dec_b, dec_t, KV_WIDTH).transpose(0, 2, 1), ((0, 0), (0, 0), (0, page - dec_t)))
        osel_rows = _sel_paged(page_table, jnp.swapaxes(qblk_s, 1, 2), sbt, ohc_s, trow_s, new_t(ks), new_t(vs),
                               pools_t[2], pools_t[3], l, past_len=past_len)
        osel = jnp.einsum("bhtgd,gh->bhdt", osel_rows.reshape(dec_b, N_HEADS, tq_s, N_KV_HEADS, HEAD_DIM),
                          head_of_row).reshape(dec_b, N_HEADS * HEAD_DIM, tq_s)
        kw_all = jnp.concatenate([flat(cache_win_k[l]), kw.reshape(dec_b, dec_t, KV_WIDTH)], axis=1)
        vw_all = jnp.concatenate([flat(cache_win_v[l]), vw.reshape(dec_b, dec_t, KV_WIDTH)], axis=1)
        wpad = ((0, 0), (0, band - wbuf - dec_t), (0, 0))
        ow = _window(qblk_s, jnp.pad(kw_all, wpad), jnp.pad(vw_all, wpad), tl_s, tq=tq_s, band=band, kpos_base=past_len - wbuf)
        untile = lambda o: o[:, :, :dec_t].transpose(1, 0, 2).reshape(1, N_HEADS * HEAD_DIM, n_s)
        xs = _combine(xs, a, untile(oc), untile(osel), untile(ow), gt, lw["e"], lw["woa"], lw["wob"], tm=CHUNK)
        kv4s = lambda t: t.reshape(dec_b, dec_t, N_KV_HEADS, HEAD_DIM)
        for idx, t in enumerate((kc, vc, ks, vs)):
            outs[6 + idx].append(kv4s(t))
        nwin_s = min(WINDOW, wbuf + dec_t)
        outs[10].append(kw_all[:, wbuf + dec_t - nwin_s:].reshape(dec_b, nwin_s, N_KV_HEADS, HEAD_DIM))
        outs[11].append(vw_all[:, wbuf + dec_t - nwin_s:].reshape(dec_b, nwin_s, N_KV_HEADS, HEAD_DIM))
        outs[12].append(v_s.reshape(dec_b, dec_t, A_GROUPS * CHUNK))

        gf = g_final[None, :] if l == depth - 1 else None
        xp = _channel_mixer(l, xp, g_ffn, dense_w, w_router, moe_w, gf, tm_dense=512, tm_moe=512)
        xs = _channel_mixer(l, xs, g_ffn, dense_w, w_router, moe_w, gf, tm_dense=CHUNK, tm_moe=CHUNK)

    y_prompt = xp.reshape(bsz, seq, dm)
    y_sample = xs.reshape(dec_b, dec_t, dm)
    return (y_prompt, y_sample) + tuple(jnp.stack(o, axis=0) for o in outs)
```

```python
import functools
import math

import numpy as np
import jax
import jax.numpy as jnp
from jax import lax
from jax.experimental import pallas as pl
from jax.experimental.pallas import tpu as pltpu

F32 = jnp.float32
BF16 = jnp.bfloat16
I32 = jnp.int32

A_GROUPS = 4
N_HEADS = 8
N_KV_HEADS = 2
HEAD_DIM = 64
CMP_LEN = 32
CMP_STRIDE = 16
CMP_HIDDEN = 128
SEL_LEN = 64
N_SEL = 16
WINDOW = 512
CHUNK = 128
ROPE_THETA = 10000.0
TOP_K = 2
EPS = 1e-6
NEG = -1e30
BIG = 1e30
TINY = 1e-30
PICKED = -3e38
N_FORCED = 3
SCALE = HEAD_DIM ** -0.5
QSCALE = SCALE * math.log2(math.e)

LANE = 128
KV_WIDTH = N_KV_HEADS * HEAD_DIM
GQA = N_HEADS // N_KV_HEADS
VMEM_LIMIT = 56 * 1024 * 1024


def _cparams(sem, vmem=None):
    return pltpu.CompilerParams(dimension_semantics=sem, vmem_limit_bytes=vmem)


def _rms(xf, g):
    return xf * lax.rsqrt(jnp.mean(xf * xf, axis=-1, keepdims=True) + EPS) * g


def _gelu(x):
    c = math.sqrt(2.0 / math.pi)
    return x * (0.5 * (1.0 + jnp.tanh(c * (x + 0.044715 * (x * x * x)))))


def _sigmoid(x):
    return 1.0 / (1.0 + jnp.exp(-x))


def _dot(a, b):
    return jnp.dot(a, b, preferred_element_type=F32)


def _dot_nt(a, b):
    return lax.dot_general(a, b, (((1,), (1,)), ((), ())), preferred_element_type=F32)


def _split_bf16(x):
    hi = x.astype(BF16)
    lo = (x - hi.astype(F32)).astype(BF16)
    return hi, lo


def _inproj_kernel(x_ref, gmix_ref, wstd_ref, wt_ref, gsgu_ref, wsgu_ref, bsgu_ref,
                   cos_ref, sin_ref, cost_ref, sint_ref,
                   a_ref, v_ref, kc_ref, vc_ref, ks_ref, vs_ref, kw_ref, vw_ref, q_ref, gt_ref,
                   *, tm, emit_qblk):
    xf = x_ref[...]
    h = _rms(xf, gmix_ref[...]).astype(BF16)
    z = _dot(h, wstd_ref[...])
    zt = _dot_nt(wt_ref[...], h)
    aw = A_GROUPS * CHUNK
    u = _gelu(z[:, 0:aw])
    vv = _gelu(z[:, aw:2 * aw])
    mu = jnp.mean(vv, axis=-1, keepdims=True)
    d = vv - mu
    var = jnp.mean(d * d, axis=-1, keepdims=True)
    v = d * lax.rsqrt(var + EPS) * gsgu_ref[...]
    v_ref[...] = v
    vb = v.astype(BF16)
    nc = tm // CHUNK
    for g in range(A_GROUPS):
        gs = slice(g * CHUNK, (g + 1) * CHUNK)
        parts = [vb[c * CHUNK:(c + 1) * CHUNK, gs] for c in range(nc)]
        xg = parts[0] if nc == 1 else jnp.concatenate(parts, axis=1)
        yg = _dot(wsgu_ref[g], xg)
        for c in range(nc):
            cs = slice(c * CHUNK, (c + 1) * CHUNK)
            mixed = yg[:, cs] + bsgu_ref[g]
            a_ref[cs, gs] = (u[cs, gs] * mixed).astype(BF16)

    cosr = cos_ref[...]
    sinr = sin_ref[...]
    lane = lax.broadcasted_iota(I32, (tm, KV_WIDTH), 1)
    first = (lane % HEAD_DIM) < (HEAD_DIM // 2)

    def rope(x):
        rot = jnp.where(first, pltpu.roll(x, KV_WIDTH - HEAD_DIM // 2, 1), pltpu.roll(x, HEAD_DIM // 2, 1))
        return x * cosr + rot * sinr

    o = 2 * aw
    kc_ref[...] = rope(z[:, o:o + 128])
    vc_ref[...] = z[:, o + 128:o + 256]
    ks_ref[...] = rope(z[:, o + 256:o + 384])
    vs_ref[...] = z[:, o + 384:o + 512]
    kw_ref[...] = rope(z[:, o + 512:o + 640])
    vw_ref[...] = z[:, o + 640:o + 768]

    ct = cost_ref[...]
    st = sint_ref[...]
    half = HEAD_DIM // 2
    for hd in range(N_HEADS):
        x1 = zt[HEAD_DIM * hd:HEAD_DIM * hd + half]
        x2 = zt[HEAD_DIM * hd + half:HEAD_DIM * (hd + 1)]
        qh = jnp.concatenate([(x1 * ct - x2 * st) * QSCALE, (x2 * ct + x1 * st) * QSCALE], axis=0).astype(BF16)
        if emit_qblk:
            kvh = hd // GQA
            zero = jnp.zeros((HEAD_DIM, LANE), BF16)
            for j in range(tm // LANE):
                ls = slice(hd * LANE, (hd + 1) * LANE)
                q_ref[j, HEAD_DIM * kvh:HEAD_DIM * (kvh + 1), ls] = qh[:, j * LANE:(j + 1) * LANE]
                q_ref[j, HEAD_DIM * (1 - kvh):HEAD_DIM * (2 - kvh), ls] = zero
        else:
            q_ref[HEAD_DIM * hd:HEAD_DIM * (hd + 1), :] = qh
    nq = N_HEADS * HEAD_DIM
    gt_ref[...] = _sigmoid(zt[nq:nq + 32])


def _inproj(x, gmix, wstd, wt, gsgu, wsgu, bsgu, cos, sin, cost, sint, *, tm, emit_qblk):
    n, dm = x.shape
    nt = n // tm
    row = lambda w: pl.BlockSpec((tm, w), lambda i: (i, 0))
    full = lambda a: pl.BlockSpec(a.shape, lambda i: (0,) * a.ndim)
    if emit_qblk:
        q_shape = jax.ShapeDtypeStruct((n // LANE, KV_WIDTH, N_HEADS * LANE), BF16)
        q_spec = pl.BlockSpec((tm // LANE, KV_WIDTH, N_HEADS * LANE), lambda i: (i, 0, 0))
    else:
        q_shape = jax.ShapeDtypeStruct((N_HEADS * HEAD_DIM, n), BF16)
        q_spec = pl.BlockSpec((N_HEADS * HEAD_DIM, tm), lambda i: (0, i))
    kv = jax.ShapeDtypeStruct((n, KV_WIDTH), F32)
    out_shape = (jax.ShapeDtypeStruct((n, 512), BF16), jax.ShapeDtypeStruct((n, 512), F32),
                 kv, kv, kv, kv, kv, kv, q_shape, jax.ShapeDtypeStruct((32, n), F32))
    out_specs = (row(512), row(512), row(128), row(128), row(128), row(128), row(128), row(128), q_spec,
                 pl.BlockSpec((32, tm), lambda i: (0, i)))
    in_specs = [row(dm), full(gmix), full(wstd), full(wt), full(gsgu), full(wsgu), full(bsgu),
                row(128), row(128), pl.BlockSpec((32, tm), lambda i: (0, i)), pl.BlockSpec((32, tm), lambda i: (0, i))]
    return pl.pallas_call(
        functools.partial(_inproj_kernel, tm=tm, emit_qblk=emit_qblk),
        grid=(nt,), in_specs=in_specs, out_specs=out_specs, out_shape=out_shape,
        compiler_params=_cparams(("parallel",), VMEM_LIMIT), name="inproj",
    )(x, gmix, wstd, wt, gsgu, wsgu, bsgu, cos, sin, cost, sint)


def _compress_kernel(c_ref, wbig_ref, pe_ref, w1_ref, w2_ref, o_ref, *, transpose_out):
    c = c_ref[0].astype(BF16)
    ab = _dot(c, wbig_ref[...])
    nch = ab.shape[0]
    hw = N_KV_HEADS * CMP_HIDDEN
    peb = _dot(pe_ref[...], w1_ref[...])
    bias = jnp.concatenate([peb[0:1]] * N_KV_HEADS, axis=1)
    hh = ab[:, :hw] + pltpu.roll(ab[:, hw:], nch - 1, 0) + bias
    g = _gelu(hh).astype(BF16)
    if transpose_out:
        o_ref[0] = _dot_nt(w2_ref[...], g).astype(BF16)
    else:
        o_ref[0] = _dot(g, w2_ref[...]).astype(BF16)


def _compress(c, wbig, pe8, w1, w2, *, transpose_out):
    b, nch, cw = c.shape
    if transpose_out:
        out_shape = jax.ShapeDtypeStruct((b, KV_WIDTH, nch), BF16)
        out_spec = pl.BlockSpec((1, KV_WIDTH, nch), lambda i: (i, 0, 0))
    else:
        out_shape = jax.ShapeDtypeStruct((b, nch, KV_WIDTH), BF16)
        out_spec = pl.BlockSpec((1, nch, KV_WIDTH), lambda i: (i, 0, 0))
    full = lambda a: pl.BlockSpec(a.shape, lambda i: (0,) * a.ndim)
    return pl.pallas_call(
        functools.partial(_compress_kernel, transpose_out=transpose_out),
        grid=(b,), in_specs=[pl.BlockSpec((1, nch, cw), lambda i: (i, 0, 0)), full(wbig), full(pe8), full(w1), full(w2)],
        out_specs=out_spec, out_shape=out_shape,
        compiler_params=_cparams(("parallel",), VMEM_LIMIT), name="compress",
    )(c, wbig, pe8, w1, w2)


CMP_PAGES = 64


def _compress_paged_kernel(pt_ref, wbig_ref, pe_ref, w1_ref, w2_ref, *refs, transpose_out):
    del pt_ref
    g = len(refs) - 3
    pages, nxt, o_ref, scr = refs[:g], refs[g], refs[g + 1], refs[g + 2]
    page = scr.shape[0] // (g + 1)
    for i in range(g):
        scr[i * page:(i + 1) * page, :] = pages[i][0, 0].T
    scr[g * page:(g + 1) * page, :] = nxt[0, 0].T
    nchs = g * page // CMP_STRIDE
    rows = nchs + 8
    ab = None
    half = CMP_STRIDE // 2
    for s in range(half):
        a = jnp.concatenate([scr[pl.ds(s, rows, stride=CMP_STRIDE), :], scr[pl.ds(s + half, rows, stride=CMP_STRIDE), :]],
                            axis=1).astype(BF16)
        d = _dot(a, wbig_ref[s])
        ab = d if ab is None else ab + d
    hw = N_KV_HEADS * CMP_HIDDEN
    peb = _dot(pe_ref[...], w1_ref[...])
    bias = jnp.concatenate([peb[0:1]] * N_KV_HEADS, axis=1)
    hh = ab[0:nchs, :hw] + pltpu.roll(ab[:, hw:], rows - 1, 0)[0:nchs] + bias
    gg = _gelu(hh).astype(BF16)
    if transpose_out:
        o_ref[0] = _dot_nt(w2_ref[...], gg).astype(BF16)
    else:
        o_ref[0] = _dot(gg, w2_ref[...]).astype(BF16)


def _compress_paged(page_table, pool_t, layer, wbig, pe8, w1, w2, *, transpose_out):
    b, n_pages = page_table.shape
    page = pool_t.shape[3]
    g = min(CMP_PAGES, n_pages)
    n_steps = n_pages // g
    nchs = g * page // CMP_STRIDE
    wbig3 = wbig.reshape(CMP_STRIDE, KV_WIDTH, wbig.shape[1])
    wbig3 = jnp.concatenate([wbig3[:CMP_STRIDE // 2], wbig3[CMP_STRIDE // 2:]], axis=1)
    pspec = lambda i: pl.BlockSpec((1, 1, KV_WIDTH, page), lambda bi, j, pt, i=i: (layer, pt[bi, j * g + i], 0, 0))
    nspec = pl.BlockSpec((1, 1, KV_WIDTH, page),
                         lambda bi, j, pt: (layer, pt[bi, jnp.minimum(j * g + g, n_pages - 1)], 0, 0))
    full = lambda a: pl.BlockSpec(a.shape, lambda bi, j, pt: (0,) * a.ndim)
    if transpose_out:
        out_shape = jax.ShapeDtypeStruct((b, KV_WIDTH, n_steps * nchs), BF16)
        out_spec = pl.BlockSpec((1, KV_WIDTH, nchs), lambda bi, j, pt: (bi, 0, j))
    else:
        out_shape = jax.ShapeDtypeStruct((b, n_steps * nchs, KV_WIDTH), BF16)
        out_spec = pl.BlockSpec((1, nchs, KV_WIDTH), lambda bi, j, pt: (bi, j, 0))
    grid_spec = pltpu.PrefetchScalarGridSpec(
        num_scalar_prefetch=1, grid=(b, n_steps),
        in_specs=[full(wbig3), full(pe8), full(w1), full(w2)] + [pspec(i) for i in range(g)] + [nspec],
        out_specs=out_spec, scratch_shapes=[pltpu.VMEM(((g + 1) * page, KV_WIDTH), F32)])
    return pl.pallas_call(
        functools.partial(_compress_paged_kernel, transpose_out=transpose_out),
        grid_spec=grid_spec, out_shape=out_shape,
        compiler_params=_cparams(("parallel", "parallel"), VMEM_LIMIT), name="compress_paged",
    )(page_table, wbig3, pe8, w1, w2, *([pool_t] * (g + 1)))


def _cmp_select_kernel(q_ref, ck_ref, cvt_ref, mt_ref, t_ref, oc_ref, qaug_ref, sbt_ref, *, tq):
    L = N_HEADS * tq
    hl = L // N_KV_HEADS
    q1 = q_ref[0]
    t = t_ref[0]
    s = _dot(ck_ref[0], q1)
    nch = s.shape[0]
    n_last = (t - (CMP_LEN - 1)) // CMP_STRIDE
    s = jnp.where(lax.broadcasted_iota(I32, (nch, L), 0) <= n_last, s, NEG)
    m = jnp.max(s, axis=0, keepdims=True)
    p = jnp.exp2(s - m)
    inv = jnp.where(m > NEG / 2, 1.0 / jnp.maximum(jnp.sum(p, axis=0, keepdims=True), TINY), 0.0)
    p = p * inv
    pb = p.astype(BF16)
    mt = mt_ref[...]
    nsel = mt.shape[0]
    blk = lax.broadcasted_iota(I32, (nsel, tq), 0).astype(F32)
    cur = (t[:, 0:tq] // SEL_LEN).astype(F32)
    biases = []
    for h in range(N_KV_HEADS):
        o_h = _dot(cvt_ref[0, HEAD_DIM * h:HEAD_DIM * (h + 1), :], pb[:, h * hl:(h + 1) * hl])
        psum = p[:, h * hl:h * hl + tq]
        for g in range(GQA):
            hd = h * GQA + g
            oc_ref[0, HEAD_DIM * hd:HEAD_DIM * (hd + 1), :] = o_h[:, g * tq:(g + 1) * tq]
            if g > 0:
                psum = psum + p[:, hd * tq:(hd + 1) * tq]
        hi, lo = _split_bf16(psum)
        imp = _dot(mt, hi) + _dot(mt, lo)
        forced = (blk == 0.0) | (blk == cur) | (blk == cur - 1.0)
        past = blk <= cur
        sc = jnp.where(past & ~forced, imp, NEG)
        for _ in range(N_SEL - N_FORCED):
            mx = jnp.max(sc, axis=0, keepdims=True)
            idx = jnp.min(jnp.where(sc == mx, blk, 1e9), axis=0, keepdims=True)
            sc = jnp.where(blk == idx, PICKED, sc)
        sel = past & (forced | (sc < PICKED / 2) | (cur < float(N_SEL)))
        bias_h = jnp.where(sel, 0.0, NEG).astype(BF16)
        biases.extend([bias_h] * GQA)
    bias = jnp.concatenate(biases, axis=1)
    qaug_ref[0, 0:KV_WIDTH, :] = q1
    qaug_ref[0, KV_WIDTH:, :] = bias
    if sbt_ref is not None:
        assert L == LANE
        pad = jnp.zeros((LANE - BLOCKS_PER_STEP, L), F32)
        for js in range(sbt_ref.shape[1]):
            grp = bias[js * BLOCKS_PER_STEP:(js + 1) * BLOCKS_PER_STEP, :].astype(F32)
            sbt_ref[0, js] = jnp.concatenate([grp, pad], axis=0).T.astype(BF16)


def _cmp_select(qblk, ck, cvt, mt, tl, *, tq, nq_per_b, n_bias_steps=0):
    nq, _, L = qblk.shape
    b, nch, _ = ck.shape
    nsel = mt.shape[0]
    shared_t = tl.shape[0] == 1
    out_specs = [pl.BlockSpec((1, N_HEADS * HEAD_DIM, tq), lambda i, j: (i * nq_per_b + j, 0, 0)),
                 pl.BlockSpec((1, KV_WIDTH + nsel, L), lambda i, j: (i * nq_per_b + j, 0, 0))]
    out_shape = [jax.ShapeDtypeStruct((nq, N_HEADS * HEAD_DIM, tq), F32),
                 jax.ShapeDtypeStruct((nq, KV_WIDTH + nsel, L), BF16)]
    if n_bias_steps:
        out_specs.append(pl.BlockSpec((1, n_bias_steps, L, LANE), lambda i, j: (i * nq_per_b + j, 0, 0, 0)))
        out_shape.append(jax.ShapeDtypeStruct((nq, n_bias_steps, L, LANE), BF16))
        body = functools.partial(_cmp_select_kernel, tq=tq)
    else:
        body = lambda *refs: _cmp_select_kernel(*refs, None, tq=tq)
    return pl.pallas_call(
        body,
        grid=(b, nq_per_b),
        in_specs=[pl.BlockSpec((1, KV_WIDTH, L), lambda i, j: (i * nq_per_b + j, 0, 0)),
                  pl.BlockSpec((1, nch, KV_WIDTH), lambda i, j: (i, 0, 0)),
                  pl.BlockSpec((1, KV_WIDTH, nch), lambda i, j: (i, 0, 0)),
                  pl.BlockSpec(mt.shape, lambda i, j: (0, 0)),
                  pl.BlockSpec((1, 1, L), (lambda i, j: (0, 0, 0)) if shared_t else (lambda i, j: (j, 0, 0)))],
        out_specs=out_specs, out_shape=out_shape,
        compiler_params=_cparams(("parallel", "parallel"), VMEM_LIMIT), name="cmp_select",
    )(qblk, ck, cvt, mt, tl)


VSUM_ROWS = 16
VT_ROWS = HEAD_DIM + VSUM_ROWS
SEL_TK = 512


def _sel_prompt_kernel(qaug_ref, kaug_ref, vt_ref, t_ref, o_ref, m_sc, acc_sc, sa_sc, sb_sc, *, tq, tk):
    qi = pl.program_id(1)
    L = N_HEADS * tq
    hl = L // N_KV_HEADS
    m_sc[...] = jnp.full(m_sc.shape, NEG, F32)
    acc_sc[...] = jnp.zeros(acc_sc.shape, F32)
    last = (qi * tq + tq - 1) // tk

    def scores(ki, buf):
        buf[...] = _dot(kaug_ref[0, pl.ds(pl.multiple_of(ki * tk, tk), tk), :], qaug_ref[0])

    def consume(ki, buf, causal):
        s = buf[...]
        if causal:
            kpos = ki * tk + lax.broadcasted_iota(I32, s.shape, 0)
            s = jnp.where(kpos <= t_ref[0], s, NEG)
        m_old = m_sc[...]
        m_new = jnp.maximum(m_old, jnp.max(s, axis=0, keepdims=True))
        m_sc[...] = m_new
        alpha = jnp.exp2(m_old - m_new)
        pb = jnp.exp2(s - m_new).astype(BF16)
        for h in range(N_KV_HEADS):
            ls = slice(h * hl, (h + 1) * hl)
            acc_sc[h] = acc_sc[h] * alpha[:, ls] + _dot(vt_ref[ki, h * VT_ROWS:(h + 1) * VT_ROWS, :], pb[:, ls])

    scores(0, sa_sc)

    def body(j, carry):
        scores(2 * j + 1, sb_sc)
        consume(2 * j, sa_sc, False)
        scores(2 * j + 2, sa_sc)
        consume(2 * j + 1, sb_sc, False)
        return carry

    lax.fori_loop(0, last // 2, body, 0)

    @pl.when(last % 2 == 1)
    def _():
        scores(last, sb_sc)
        consume(last - 1, sa_sc, False)
        consume(last, sb_sc, True)

    @pl.when(last % 2 == 0)
    def _():
        consume(last, sa_sc, True)

    for hd in range(N_HEADS):
        h, g = divmod(hd, GQA)
        gs = slice(g * tq, (g + 1) * tq)
        linv = 1.0 / jnp.maximum(acc_sc[h, HEAD_DIM:HEAD_DIM + 1, gs], TINY)
        o_ref[0, HEAD_DIM * hd:HEAD_DIM * (hd + 1), :] = acc_sc[h, 0:HEAD_DIM, gs] * linv


def _sel_prompt(qaug, kaug, vt, tl, *, tq, tk, bsz):
    nq, r, L = qaug.shape
    nqb = nq // bsz
    tlen = kaug.shape[0] // bsz
    return pl.pallas_call(
        functools.partial(_sel_prompt_kernel, tq=tq, tk=tk),
        grid=(bsz, nqb),
        in_specs=[pl.BlockSpec((1, r, L), lambda i, j: (i * nqb + j, 0, 0)),
                  pl.BlockSpec((1, tlen, r), lambda i, j: (i, 0, 0)),
                  pl.BlockSpec((tlen // tk, N_KV_HEADS * VT_ROWS, tk), lambda i, j: (i, 0, 0)),
                  pl.BlockSpec((1, 1, L), lambda i, j: (j, 0, 0))],
        out_specs=pl.BlockSpec((1, N_HEADS * HEAD_DIM, tq), lambda i, j: (i * nqb + j, 0, 0)),
        out_shape=jax.ShapeDtypeStruct((nq, N_HEADS * HEAD_DIM, tq), F32),
        scratch_shapes=[pltpu.VMEM((1, L), F32), pltpu.VMEM((N_KV_HEADS, VT_ROWS, L // N_KV_HEADS), F32),
                        pltpu.VMEM((tk, L), F32), pltpu.VMEM((tk, L), F32)],
        compiler_params=_cparams(("parallel", "parallel"), VMEM_LIMIT), name="sel_prompt",
    )(qaug, kaug.reshape(bsz, tlen, r), vt, tl)


def _sel_prep_kernel(k_ref, v_ref, oh_ref, kaug_ref, vt_ref):
    kaug_ref[:, 0:KV_WIDTH] = k_ref[...].astype(BF16)
    kaug_ref[:, KV_WIDTH:] = oh_ref[...]
    vt = v_ref[...].T.astype(BF16)
    ones = jnp.ones((VSUM_ROWS, vt.shape[1]), BF16)
    for h in range(N_KV_HEADS):
        vt_ref[0, h * VT_ROWS:h * VT_ROWS + HEAD_DIM, :] = vt[h * HEAD_DIM:(h + 1) * HEAD_DIM, :]
        vt_ref[0, h * VT_ROWS + HEAD_DIM:(h + 1) * VT_ROWS, :] = ones


def _sel_prep(k, v, oh, *, tk):
    n = k.shape[0]
    tiles_per_seq = oh.shape[0] // tk
    nsel = oh.shape[1]
    return pl.pallas_call(
        _sel_prep_kernel, grid=(n // tk,),
        in_specs=[pl.BlockSpec((tk, KV_WIDTH), lambda i: (i, 0)), pl.BlockSpec((tk, KV_WIDTH), lambda i: (i, 0)),
                  pl.BlockSpec((tk, nsel), lambda i: (i % tiles_per_seq, 0))],
        out_specs=(pl.BlockSpec((tk, KV_WIDTH + nsel), lambda i: (i, 0)),
                   pl.BlockSpec((1, N_KV_HEADS * VT_ROWS, tk), lambda i: (i, 0, 0))),
        out_shape=(jax.ShapeDtypeStruct((n, KV_WIDTH + nsel), BF16),
                   jax.ShapeDtypeStruct((n // tk, N_KV_HEADS * VT_ROWS, tk), BF16)),
        compiler_params=_cparams(("parallel",)), name="sel_prep",
    )(k, v, oh)


SEL_PAGES = 32
BLOCKS_PER_STEP = 64


def _sel_paged_kernel(pt_ref, q_ref, sbt_ref, sbt_tail_ref, ohc_ref, trow_ref, knew_ref, vnew_ref, *refs, past_len):
    del pt_ref
    g = SEL_PAGES
    kpages, vpages, o_ref, m_sc, l_sc, acc_sc, kcat_sc, vcat_sc = refs[:g], refs[g:2 * g], refs[2 * g], *refs[2 * g + 1:]
    js = pl.program_id(1)
    q = q_ref[0]

    @pl.when(js == 0)
    def _():
        m_sc[...] = jnp.full(m_sc.shape, NEG, F32)
        l_sc[...] = jnp.zeros(l_sc.shape, F32)
        acc_sc[...] = jnp.zeros(acc_sc.shape, F32)

    def update(s, vt):
        m_old = m_sc[...]
        m_new = jnp.maximum(m_old, jnp.max(s, axis=1, keepdims=True))
        alpha = jnp.exp2(m_old - m_new)
        p = jnp.exp2(s - m_new)
        l_sc[...] = alpha * l_sc[...] + jnp.sum(p, axis=1, keepdims=True)
        m_sc[...] = m_new
        acc_sc[...] = acc_sc[...] * alpha + _dot_nt(p.astype(BF16), vt)

    for i in range(g):
        kcat_sc[:, i * LANE:(i + 1) * LANE] = kpages[i][0, 0].astype(BF16)
        vcat_sc[:, i * LANE:(i + 1) * LANE] = vpages[i][0, 0].astype(BF16)
    ohc = ohc_ref[...]
    s = _dot(q, kcat_sc[...]) + _dot(sbt_ref[0, 0], ohc)
    update(s, vcat_sc[...])

    @pl.when(js == pl.num_programs(1) - 1)
    def _():
        sn = _dot(q, knew_ref[0].astype(BF16)) + _dot(sbt_tail_ref[0, 0], ohc[:, 0:LANE])
        kpos = past_len + lax.broadcasted_iota(I32, sn.shape, 1)
        sn = jnp.where(kpos <= trow_ref[...], sn, NEG)
        update(sn, vnew_ref[0].astype(BF16))
        o_ref[0] = acc_sc[...] * (1.0 / jnp.maximum(l_sc[...], TINY))


def _sel_paged(page_table, qstd, sbt, ohc, trow, knew_t, vnew_t, pool_k, pool_v, layer, *, past_len):
    b, rows, _ = qstd.shape
    g = SEL_PAGES
    page = pool_k.shape[3]
    n_steps = page_table.shape[1] // g
    pspec = lambda i: pl.BlockSpec((1, 1, KV_WIDTH, page), lambda bi, j, pt, i=i: (layer, pt[bi, j * g + i], 0, 0))
    per_b = lambda a: pl.BlockSpec((1,) + a.shape[1:], lambda bi, j, pt: (bi,) + (0,) * (a.ndim - 1))
    full = lambda a: pl.BlockSpec(a.shape, lambda bi, j, pt: (0,) * a.ndim)
    grid_spec = pltpu.PrefetchScalarGridSpec(
        num_scalar_prefetch=1, grid=(b, n_steps),
        in_specs=[per_b(qstd),
                  pl.BlockSpec((1, 1) + sbt.shape[2:], lambda bi, j, pt: (bi, j, 0, 0)),
                  pl.BlockSpec((1, 1) + sbt.shape[2:], lambda bi, j, pt: (bi, n_steps, 0, 0)),
                  full(ohc), full(trow), per_b(knew_t), per_b(vnew_t)]
                 + [pspec(i) for i in range(g)] + [pspec(i) for i in range(g)],
        out_specs=pl.BlockSpec((1, rows, KV_WIDTH), lambda bi, j, pt: (bi, 0, 0)),
        scratch_shapes=[pltpu.VMEM((rows, 1), F32), pltpu.VMEM((rows, 1), F32), pltpu.VMEM((rows, KV_WIDTH), F32),
                        pltpu.VMEM((KV_WIDTH, g * page), BF16), pltpu.VMEM((KV_WIDTH, g * page), BF16)])
    return pl.pallas_call(
        functools.partial(_sel_paged_kernel, past_len=past_len),
        grid_spec=grid_spec, out_shape=jax.ShapeDtypeStruct((b, rows, KV_WIDTH), F32),
        compiler_params=_cparams(("parallel", "arbitrary"), VMEM_LIMIT), name="sel_paged",
    )(page_table, qstd, sbt, sbt, ohc, trow, knew_t, vnew_t, *([pool_k] * g), *([pool_v] * g))


def _window_kernel(q_ref, k_ref, v_ref, t_ref, o_ref, *, tq, band, kpos_base):
    qi = pl.program_id(1)
    L = N_HEADS * tq
    hl = L // N_KV_HEADS
    q1 = q_ref[0]
    t = t_ref[0]
    start = pl.multiple_of(jnp.maximum(qi * tq + tq - band, 0), LANE)
    kb = k_ref[0, pl.ds(start, band), :].astype(BF16)
    s = _dot(kb, q1)
    r_hi = t - (kpos_base + start)
    row = lax.broadcasted_iota(I32, s.shape, 0)
    s = jnp.where((row <= r_hi) & (row > r_hi - WINDOW), s, NEG)
    m = jnp.max(s, axis=0, keepdims=True)
    pb = jnp.exp2(s - m).astype(BF16)
    keep = m > NEG / 2
    vt = v_ref[0, pl.ds(start, band), :].T.astype(BF16)
    ones = jnp.ones((VSUM_ROWS, band), BF16)
    for h in range(N_KV_HEADS):
        vth = jnp.concatenate([vt[HEAD_DIM * h:HEAD_DIM * (h + 1), :], ones], axis=0)
        o_h = _dot(vth, pb[:, h * hl:(h + 1) * hl])
        for g in range(GQA):
            hd = h * GQA + g
            gs = slice(g * tq, (g + 1) * tq)
            linv = jnp.where(keep[:, hd * tq:(hd + 1) * tq], 1.0 / jnp.maximum(o_h[HEAD_DIM:HEAD_DIM + 1, gs], TINY), 0.0)
            o_ref[0, HEAD_DIM * hd:HEAD_DIM * (hd + 1), :] = o_h[0:HEAD_DIM, gs] * linv


def _window(qblk, k, v, tl, *, tq, band, kpos_base):
    nq, _, L = qblk.shape
    b, tlen, _ = k.shape
    nqb = nq // b
    shared_t = tl.shape[0] == 1
    return pl.pallas_call(
        functools.partial(_window_kernel, tq=tq, band=band, kpos_base=kpos_base),
        grid=(b, nqb),
        in_specs=[pl.BlockSpec((1, KV_WIDTH, L), lambda i, j: (i * nqb + j, 0, 0)),
                  pl.BlockSpec((1, tlen, KV_WIDTH), lambda i, j: (i, 0, 0)),
                  pl.BlockSpec((1, tlen, KV_WIDTH), lambda i, j: (i, 0, 0)),
                  pl.BlockSpec((1, 1, L), (lambda i, j: (0, 0, 0)) if shared_t else (lambda i, j: (j, 0, 0)))],
        out_specs=pl.BlockSpec((1, N_HEADS * HEAD_DIM, tq), lambda i, j: (i * nqb + j, 0, 0)),
        out_shape=jax.ShapeDtypeStruct((nq, N_HEADS * HEAD_DIM, tq), F32),
        compiler_params=_cparams(("parallel", "parallel"), VMEM_LIMIT), name="window",
    )(qblk, k, v, tl)


def _combine_kernel(x_ref, a_ref, oc_ref, os_ref, ow_ref, gt_ref, e_ref, woa_ref, wob_ref, o_ref, *, tm):
    hi, lo = _split_bf16(gt_ref[...])
    ge = _dot(e_ref[...], hi) + _dot(e_ref[...], lo)
    bw = N_HEADS * HEAD_DIM
    parts = []
    for j in range(tm // LANE):
        ls = slice(j * LANE, (j + 1) * LANE)
        parts.append(ge[0:bw, ls] * oc_ref[j] + ge[bw:2 * bw, ls] * os_ref[j] + ge[2 * bw:3 * bw, ls] * ow_ref[j])
    mixt = parts[0] if len(parts) == 1 else jnp.concatenate(parts, axis=1)
    mix = mixt.T.astype(BF16)
    o_ref[...] = x_ref[...] + _dot(a_ref[...], woa_ref[...]) + _dot(mix, wob_ref[...])


def _combine(x, a, oc, os_, ow, gt, e, woa, wob, *, tm):
    n, dm = x.shape
    bw = N_HEADS * HEAD_DIM
    full = lambda arr: pl.BlockSpec(arr.shape, lambda i: (0,) * arr.ndim)
    ospec = pl.BlockSpec((tm // LANE, bw, LANE), lambda i: (i, 0, 0))
    return pl.pallas_call(
        functools.partial(_combine_kernel, tm=tm),
        grid=(n // tm,),
        in_specs=[pl.BlockSpec((tm, dm), lambda i: (i, 0)), pl.BlockSpec((tm, a.shape[1]), lambda i: (i, 0)),
                  ospec, ospec, ospec, pl.BlockSpec((32, tm), lambda i: (0, i)), full(e), full(woa), full(wob)],
        out_specs=pl.BlockSpec((tm, dm), lambda i: (i, 0)),
        out_shape=jax.ShapeDtypeStruct((n, dm), F32),
        compiler_params=_cparams(("parallel",), VMEM_LIMIT), name="combine",
    )(x, a, oc, os_, ow, gt, e, woa, wob)


def _ffn_dense_kernel(x_ref, g_ref, wg_ref, wu_ref, wd_ref, o_ref):
    xf = x_ref[...]
    h = _rms(xf, g_ref[...]).astype(BF16)
    gate = _dot(h, wg_ref[...])
    up = _dot(h, wu_ref[...])
    act = (gate * _sigmoid(gate) * up).astype(BF16)
    o_ref[...] = xf + _dot(act, wd_ref[...])


def _ffn_dense(x, g, wg, wu, wd, *, tm):
    n, dm = x.shape
    full = lambda arr: pl.BlockSpec(arr.shape, lambda i: (0,) * arr.ndim, pipeline_mode=pl.Buffered(1))
    return pl.pallas_call(
        _ffn_dense_kernel, grid=(n // tm,),
        in_specs=[pl.BlockSpec((tm, dm), lambda i: (i, 0)), full(g), full(wg), full(wu), full(wd)],
        out_specs=pl.BlockSpec((tm, dm), lambda i: (i, 0)),
        out_shape=jax.ShapeDtypeStruct((n, dm), F32),
        compiler_params=_cparams(("parallel",), VMEM_LIMIT), name="ffn_dense",
    )(x, g, wg, wu, wd)


def _router_kernel(x_ref, g_ref, wrh_ref, wrl_ref, tri_ref, h_ref, meta_ref, cnt_ref, *, tm, n_exp):
    h = _rms(x_ref[...], g_ref[...])
    h_ref[...] = h
    hi, lo = _split_bf16(h)
    logits = _dot(hi, wrh_ref[...]) + _dot(lo, wrh_ref[...]) + _dot(hi, wrl_ref[...])
    lane = lax.broadcasted_iota(I32, (tm, LANE), 1)
    lanef = lane.astype(F32)
    logits = jnp.where(lane < n_exp, logits, NEG)
    m1 = jnp.max(logits, axis=1, keepdims=True)
    i1 = jnp.min(jnp.where(logits == m1, lanef, 1e9), axis=1, keepdims=True)
    rest = jnp.where(lanef == i1, -3e38, logits)
    m2 = jnp.max(rest, axis=1, keepdims=True)
    i2 = jnp.min(jnp.where(rest == m2, lanef, 1e9), axis=1, keepdims=True)
    e2 = jnp.exp(m2 - m1)
    w1 = 1.0 / (1.0 + e2)
    w2 = e2 / (1.0 + e2)
    hit1 = lanef == i1
    hit2 = lanef == i2
    msel = jnp.where(hit1 | hit2, 1.0, 0.0)
    ranks = _dot(tri_ref[...], msel.astype(BF16))
    r1 = jnp.sum(jnp.where(hit1, ranks, 0.0), axis=1, keepdims=True)
    r2 = jnp.sum(jnp.where(hit2, ranks, 0.0), axis=1, keepdims=True)
    meta = jnp.where(lane == 0, i1, 0.0)
    for k, val in enumerate((i2, w1, w2, r1, r2)):
        meta = jnp.where(lane == k + 1, val, meta)
    meta_ref[...] = meta
    cnt_ref[0] = jnp.sum(msel, axis=0, keepdims=True)


def _router(x, g, wrh, wrl, tri, *, tm, n_exp):
    n, dm = x.shape
    full = lambda arr: pl.BlockSpec(arr.shape, lambda i: (0,) * arr.ndim)
    return pl.pallas_call(
        functools.partial(_router_kernel, tm=tm, n_exp=n_exp), grid=(n // tm,),
        in_specs=[pl.BlockSpec((tm, dm), lambda i: (i, 0)), full(g), full(wrh), full(wrl), full(tri)],
        out_specs=(pl.BlockSpec((tm, dm), lambda i: (i, 0)), pl.BlockSpec((tm, LANE), lambda i: (i, 0)),
                   pl.BlockSpec((1, 1, LANE), lambda i: (i, 0, 0))),
        out_shape=(jax.ShapeDtypeStruct((n, dm), F32), jax.ShapeDtypeStruct((n, LANE), F32),
                   jax.ShapeDtypeStruct((n // tm, 1, LANE), F32)),
        compiler_params=_cparams(("parallel",), VMEM_LIMIT), name="router",
    )(x, g, wrh, wrl, tri)


def _row_copy(src, src_row, dst, dst_row, sem):
    return pltpu.make_async_copy(src.at[pl.ds(src_row, 1)], dst.at[pl.ds(dst_row, 1)], sem)


def _dispatch_kernel(dest_ref, h_ref, xg_in_ref, xg_ref, sem, *, tm):
    del xg_in_ref

    def issue(r, carry):
        for k in range(TOP_K):
            _row_copy(h_ref, r, xg_ref, dest_ref[TOP_K * r + k], sem).start(priority=k % 2)
        return carry

    lax.fori_loop(0, tm, issue, 0)

    def drain(r, carry):
        for k in range(TOP_K):
            _row_copy(h_ref, 0, xg_ref, 0, sem).wait()
        return carry

    lax.fori_loop(0, tm, drain, 0)


def _dispatch(dest, h, xg0, *, tm):
    n, dm = h.shape
    return pl.pallas_call(
        functools.partial(_dispatch_kernel, tm=tm), grid=(n // tm,),
        in_specs=[pl.BlockSpec((TOP_K * tm,), lambda i: (i,), memory_space=pltpu.SMEM),
                  pl.BlockSpec((tm, dm), lambda i: (i, 0)),
                  pl.BlockSpec(memory_space=pl.ANY)],
        out_specs=pl.BlockSpec(memory_space=pl.ANY),
        out_shape=jax.ShapeDtypeStruct(xg0.shape, F32),
        scratch_shapes=[pltpu.SemaphoreType.DMA],
        input_output_aliases={2: 0},
        compiler_params=_cparams(("arbitrary",), VMEM_LIMIT), name="moe_dispatch",
    )(dest, h, xg0)


def _expert_ffn_kernel(te_ref, nused_ref, x_ref, wg_ref, wu_ref, wd_ref, o_ref):
    s = pl.program_id(0)

    @pl.when(s < nused_ref[0])
    def _():
        h = x_ref[...].astype(BF16)
        gate = _dot(h, wg_ref[0])
        up = _dot(h, wu_ref[0])
        act = (gate * _sigmoid(gate) * up).astype(BF16)
        o_ref[...] = _dot(act, wd_ref[0])

    @pl.when(s >= nused_ref[0])
    def _():
        o_ref[...] = jnp.zeros(o_ref.shape, F32)


def _expert_ffn(te, nused, xg, wg, wu, wd, *, ts):
    s_rows, dm = xg.shape
    dff = wg.shape[2]
    grid_spec = pltpu.PrefetchScalarGridSpec(
        num_scalar_prefetch=2, grid=(s_rows // ts,),
        in_specs=[pl.BlockSpec((ts, dm), lambda i, te, nu: (i, 0)),
                  pl.BlockSpec((1, dm, dff), lambda i, te, nu: (te[i], 0, 0), pipeline_mode=pl.Buffered(1)),
                  pl.BlockSpec((1, dm, dff), lambda i, te, nu: (te[i], 0, 0), pipeline_mode=pl.Buffered(1)),
                  pl.BlockSpec((1, dff, dm), lambda i, te, nu: (te[i], 0, 0), pipeline_mode=pl.Buffered(1))],
        out_specs=pl.BlockSpec((ts, dm), lambda i, te, nu: (i, 0)))
    return pl.pallas_call(
        _expert_ffn_kernel, grid_spec=grid_spec,
        out_shape=jax.ShapeDtypeStruct((s_rows, dm), F32),
        compiler_params=_cparams(("arbitrary",), VMEM_LIMIT), name="expert_ffn",
    )(te, nused, xg, wg, wu, wd)


def _moe_combine_kernel(dest_ref, x_ref, meta_ref, y_ref, *rest, tm):
    gf_ref = rest[0] if len(rest) == 4 else None
    o_ref, buf, sem = rest[-3:]

    def issue(r, carry):
        for k in range(TOP_K):
            _row_copy(y_ref, dest_ref[TOP_K * r + k], buf.at[k], r, sem).start(priority=k % 2)
        return carry

    lax.fori_loop(0, tm, issue, 0)

    def drain(r, carry):
        for k in range(TOP_K):
            _row_copy(y_ref, 0, buf.at[k], 0, sem).wait()
        return carry

    lax.fori_loop(0, tm, drain, 0)
    meta = meta_ref[...]
    out = x_ref[...] + meta[:, 2:3] * buf[0] + meta[:, 3:4] * buf[1]
    o_ref[...] = out if gf_ref is None else _rms(out, gf_ref[...])


def _moe_combine(dest, x, meta, y, gf, *, tm):
    n, dm = x.shape
    extra = [] if gf is None else [gf]
    return pl.pallas_call(
        functools.partial(_moe_combine_kernel, tm=tm), grid=(n // tm,),
        in_specs=[pl.BlockSpec((TOP_K * tm,), lambda i: (i,), memory_space=pltpu.SMEM),
                  pl.BlockSpec((tm, dm), lambda i: (i, 0)),
                  pl.BlockSpec((tm, LANE), lambda i: (i, 0)),
                  pl.BlockSpec(memory_space=pl.ANY)] + [pl.BlockSpec(a.shape, lambda i: (0, 0)) for a in extra],
        out_specs=pl.BlockSpec((tm, dm), lambda i: (i, 0)),
        out_shape=jax.ShapeDtypeStruct((n, dm), F32),
        scratch_shapes=[pltpu.VMEM((TOP_K, tm, dm), F32), pltpu.SemaphoreType.DMA],
        compiler_params=_cparams(("arbitrary",), VMEM_LIMIT), name="moe_combine",
    )(dest, x, meta, y, *extra)


def _moe(x, g, wr, wg, wu, wd, gf, *, tm, ts):
    n, dm = x.shape
    n_exp = wr.shape[1]
    wr_pad = jnp.zeros((dm, LANE), F32).at[:, :n_exp].set(wr)
    wrh, wrl = _split_bf16(wr_pad)
    tri = jnp.tril(jnp.ones((tm, tm), F32), -1).astype(BF16)
    h, meta, cnt = _router(x, g, wrh, wrl, tri, tm=tm, n_exp=n_exp)
    cnt = cnt[:, 0, :n_exp].astype(I32)
    tile_off = jnp.cumsum(cnt, axis=0) - cnt
    tot = jnp.sum(cnt, axis=0)
    padded = ((tot + ts - 1) // ts) * ts
    gend = jnp.cumsum(padded)
    gstart = gend - padded
    ei = meta[:, 0:2].astype(I32)
    rk = meta[:, 4:6].astype(I32)
    base = jnp.broadcast_to((gstart[None, :] + tile_off)[:, None, :], (n // tm, tm, n_exp)).reshape(n, 1, n_exp)
    pick = ei[:, :, None] == jnp.arange(n_exp, dtype=I32)[None, None, :]
    dest = (jnp.sum(jnp.where(pick, base, 0), axis=-1) + rk).reshape(-1)
    n_slot_tiles = -(-(TOP_K * n + n_exp * (ts - 1)) // ts)
    n_slots = n_slot_tiles * ts
    nused = (gend[-1] // ts).astype(I32).reshape(1)
    tile_start = jnp.minimum(jnp.arange(n_slot_tiles, dtype=I32), jnp.maximum(nused[0] - 1, 0)) * ts
    te = jnp.minimum(jnp.sum((gend[None, :] <= tile_start[:, None]).astype(I32), axis=1), n_exp - 1)
    xg = _dispatch(dest, h, jnp.zeros((n_slots, dm), F32), tm=tm)
    y = _expert_ffn(te, nused, xg, wg, wu, wd, ts=ts)
    return _moe_combine(dest, x, meta, y, gf, tm=tm)


def _norm_kernel(x_ref, g_ref, o_ref):
    o_ref[...] = _rms(x_ref[...], g_ref[...])


def _final_norm(x, g, *, tm):
    n, dm = x.shape
    return pl.pallas_call(
        _norm_kernel, grid=(n // tm,),
        in_specs=[pl.BlockSpec((tm, dm), lambda i: (i, 0)), pl.BlockSpec(g.shape, lambda i: (0, 0))],
        out_specs=pl.BlockSpec((tm, dm), lambda i: (i, 0)),
        out_shape=jax.ShapeDtypeStruct((n, dm), F32),
        compiler_params=_cparams(("parallel",)), name="final_norm",
    )(x, g)


def _rope_tables(pos):
    half = HEAD_DIM // 2
    inv = ROPE_THETA ** (-jnp.arange(half, dtype=F32) / half)
    ang = pos.astype(F32)[:, None] * inv[None, :]
    cos, sin = jnp.cos(ang), jnp.sin(ang)
    cos_std = jnp.tile(cos, (1, 4))
    sin_std = jnp.tile(jnp.concatenate([-sin, sin], axis=1), (1, 2))
    return cos_std, sin_std, cos.T, sin.T


def _cmp_to_sel_t(n_cmp_pad, n_sel_pad):
    i0 = np.arange(n_cmp_pad, dtype=np.int64)[None, :] * CMP_STRIDE
    j0 = np.arange(n_sel_pad, dtype=np.int64)[:, None] * SEL_LEN
    ov = np.clip(np.minimum(i0 + CMP_LEN, j0 + SEL_LEN) - np.maximum(i0, j0), 0, CMP_LEN)
    return jnp.asarray(ov.astype(np.float32) / CMP_LEN, dtype=BF16)


def _block_onehot(n_keys, n_sel_pad):
    blk = np.arange(n_keys)[:, None] // SEL_LEN
    return jnp.asarray((blk == np.arange(n_sel_pad)[None, :]).astype(np.float32), dtype=BF16)


def _lane_positions(pos_tiles, tq):
    return jnp.tile(pos_tiles.astype(I32), (1, N_HEADS))[:, None, :]


def _compress_weights(w1, w2, pe):
    nr = CMP_LEN // CMP_STRIDE
    w1r = w1.reshape(nr, CMP_STRIDE, HEAD_DIM, CMP_HIDDEN)
    eye = jnp.eye(N_KV_HEADS, dtype=F32)
    wbig = jnp.einsum("rsde,gh->sgdrhe", w1r, eye)
    wbig = wbig.reshape(CMP_STRIDE * KV_WIDTH, nr * N_KV_HEADS * CMP_HIDDEN).astype(BF16)
    w2big = jnp.einsum("ed,gh->gehd", w2, eye).reshape(N_KV_HEADS * CMP_HIDDEN, KV_WIDTH).astype(BF16)
    pe8 = jnp.tile(pe.reshape(1, CMP_LEN * HEAD_DIM), (8, 1)).astype(BF16)
    return wbig, pe8, w1.astype(BF16), w2big


def _layer_weights(l, g_mix, w_in, g_sgu, w_sgu, b_sgu, w_cmpk1, w_cmpk2, pe_cmpk, w_cmpv1, w_cmpv2, pe_cmpv, w_o, dec_t):
    aw = A_GROUPS * CHUNK
    bw = N_HEADS * HEAD_DIM
    w = w_in[l]
    wstd = jnp.concatenate([w[:, :2 * aw], w[:, 2 * aw + bw:2 * aw + bw + 6 * KV_WIDTH]], axis=1).astype(BF16)
    n_gate = 3 * N_HEADS
    wt = jnp.concatenate([w[:, 2 * aw:2 * aw + bw], w[:, -n_gate:], jnp.zeros((w.shape[0], 32 - n_gate), F32)], axis=1).T.astype(BF16)
    causal = jnp.tril(jnp.ones((CHUNK, CHUNK), bool))
    wsgu_p = jnp.where(causal[None], w_sgu[l], 0.0).astype(BF16)
    bsgu_p = jnp.broadcast_to(b_sgu[l][:, :, None], (A_GROUPS, CHUNK, CHUNK)).astype(F32)
    reps = CHUNK // dec_t
    wsmall = jnp.where(causal[None, :dec_t, :dec_t], w_sgu[l][:, :dec_t, :dec_t], 0.0)
    wsgu_s = jnp.einsum("ab,gts->gatbs", jnp.eye(reps, dtype=F32), wsmall).reshape(A_GROUPS, CHUNK, CHUNK).astype(BF16)
    bsgu_s = jnp.broadcast_to(jnp.tile(b_sgu[l][:, :dec_t], (1, reps))[:, :, None], (A_GROUPS, CHUNK, CHUNK)).astype(F32)
    ck_w = _compress_weights(w_cmpk1[l], w_cmpk2[l], pe_cmpk[l])
    cv_w = _compress_weights(w_cmpv1[l], w_cmpv2[l], pe_cmpv[l])
    cv_w = cv_w[:3] + (cv_w[3].T,)
    e = np.zeros((3 * bw, 32), np.float32)
    for br in range(3):
        for hd in range(N_HEADS):
            e[br * bw + hd * HEAD_DIM:br * bw + (hd + 1) * HEAD_DIM, br * N_HEADS + hd] = 1.0
    return dict(gmix=g_mix[l][None, :], wstd=wstd, wt=wt, gsgu=g_sgu[l][None, :],
                wsgu_p=wsgu_p, bsgu_p=bsgu_p, wsgu_s=wsgu_s, bsgu_s=bsgu_s, ck_w=ck_w, cv_w=cv_w,
                e=jnp.asarray(e, dtype=BF16), woa=w_o[l][:aw].astype(BF16), wob=w_o[l][aw:].astype(BF16))


def _channel_mixer(l, x, g_ffn, dense_w, w_router, moe_w, gf, *, tm_dense, tm_moe):
    g = g_ffn[l][None, :]
    i = l // 2
    if l % 2 == 0:
        y = _ffn_dense(x, g, *(w[i] for w in dense_w), tm=tm_dense)
        return y if gf is None else _final_norm(y, gf, tm=tm_dense)
    return _moe(x, g, w_router[i], *(w[i] for w in moe_w), gf, tm=tm_moe, ts=512)


def kernel(x_prompt, x_sample, cache_cmp_k, cache_cmp_v, cache_sel_k, cache_sel_v, cache_win_k, cache_win_v, page_table,
           g_mix, w_in, g_sgu, w_sgu, b_sgu, w_cmpk1, w_cmpk2, pe_cmpk, w_cmpv1, w_cmpv2, pe_cmpv, w_o, g_ffn,
           w_ff_gate, w_ff_up, w_ff_down, w_router, w_moe_gate, w_moe_up, w_moe_down, g_final):
    bsz, seq, dm = x_prompt.shape
    dec_b, dec_t, _ = x_sample.shape
    depth = g_mix.shape[0]
    page = cache_cmp_k.shape[2]
    past_len = page_table.shape[1] * page
    wbuf = cache_win_k.shape[2]
    n_p, n_s = bsz * seq, dec_b * dec_t
    assert n_s == CHUNK and seq % 512 == 0 and wbuf == WINDOW and past_len % 2048 == 0
    tq_p, tq_s = LANE, LANE // N_HEADS
    nq_b = seq // tq_p
    band = WINDOW + tq_p

    pos_p = jnp.tile(jnp.arange(seq, dtype=I32), bsz)
    pos_s = jnp.tile(past_len + jnp.arange(dec_t, dtype=I32), dec_b)
    rope_p = _rope_tables(pos_p)
    rope_s = _rope_tables(pos_s)
    tl_p = _lane_positions(jnp.arange(seq, dtype=I32).reshape(nq_b, tq_p), tq_p)
    tok_s = past_len + jnp.minimum(jnp.arange(tq_s, dtype=I32), dec_t - 1)
    tl_s = _lane_positions(tok_s[None, :], tq_s)
    nch_p = seq // CMP_STRIDE
    nsel_p = seq // SEL_LEN
    nch_s = past_len // CMP_STRIDE
    nsel_s = -(-(past_len // SEL_LEN + 1) // LANE) * LANE
    mt_p = _cmp_to_sel_t(nch_p, nsel_p)
    mt_s = _cmp_to_sel_t(nch_s, nsel_s)
    oh_p = _block_onehot(seq, nsel_p)
    n_sel_steps = past_len // (SEL_PAGES * page) + 1
    assert nsel_s >= n_sel_steps * BLOCKS_PER_STEP and SEL_PAGES * page == BLOCKS_PER_STEP * SEL_LEN
    ohc_s = _block_onehot(SEL_PAGES * page, LANE).T
    trow_s = jnp.broadcast_to(jnp.tile(tok_s, N_HEADS)[:, None], (N_HEADS * tq_s, LANE)).astype(I32)
    head_of_row = jnp.asarray((np.arange(N_HEADS)[None, :] // GQA == np.arange(N_KV_HEADS)[:, None]).astype(np.float32))

    pools_t = [jnp.transpose(c, (0, 1, 3, 4, 2)).reshape(depth, c.shape[1], KV_WIDTH, page)
               for c in (cache_cmp_k, cache_cmp_v, cache_sel_k, cache_sel_v)]
    flat = lambda a: a.reshape(a.shape[0], a.shape[1], KV_WIDTH)

    dense_w = tuple(w.astype(BF16) for w in (w_ff_gate, w_ff_up, w_ff_down))
    moe_w = tuple(w.astype(BF16) for w in (w_moe_gate, w_moe_up, w_moe_down))
    xp = x_prompt.reshape(n_p, dm)
    xs = x_sample.reshape(n_s, dm)
    outs = [[] for _ in range(13)]
    for l in range(depth):
        lw = _layer_weights(l, g_mix, w_in, g_sgu, w_sgu, b_sgu, w_cmpk1, w_cmpk2, pe_cmpk, w_cmpv1, w_cmpv2, pe_cmpv, w_o, dec_t)
        a, _, kc, vc, ks, vs, kw, vw, qblk, gt = _inproj(
            xp, lw["gmix"], lw["wstd"], lw["wt"], lw["gsgu"], lw["wsgu_p"], lw["bsgu_p"], *rope_p, tm=512, emit_qblk=True)
        ck = _compress(kc.reshape(bsz, nch_p, CMP_STRIDE * KV_WIDTH), *lw["ck_w"], transpose_out=False)
        cvt = _compress(vc.reshape(bsz, nch_p, CMP_STRIDE * KV_WIDTH), *lw["cv_w"], transpose_out=True)
        oc, qaug = _cmp_select(qblk, ck, cvt, mt_p, tl_p, tq=tq_p, nq_per_b=nq_b)
        kaug, vst = _sel_prep(ks, vs, oh_p, tk=SEL_TK)
        osel = _sel_prompt(qaug, kaug, vst, tl_p, tq=tq_p, tk=SEL_TK, bsz=bsz)
        ow = _window(qblk, kw.reshape(bsz, seq, KV_WIDTH), vw.reshape(bsz, seq, KV_WIDTH), tl_p, tq=tq_p, band=band, kpos_base=0)
        xp = _combine(xp, a, oc, osel, ow, gt, lw["e"], lw["woa"], lw["wob"], tm=512)
        kv4 = lambda t: t.reshape(bsz, seq, N_KV_HEADS, HEAD_DIM)
        for idx, t in enumerate((kc, vc, ks, vs)):
            outs[idx].append(kv4(t))
        nwin_p = min(WINDOW, seq)
        outs[4].append(kv4(kw)[:, seq - nwin_p:])
        outs[5].append(kv4(vw)[:, seq - nwin_p:])

        a, v_s, kc, vc, ks, vs, kw, vw, qt, gt = _inproj(
            xs, lw["gmix"], lw["wstd"], lw["wt"], lw["gsgu"], lw["wsgu_s"], lw["bsgu_s"], *rope_s, tm=CHUNK, emit_qblk=False)
        q4 = qt.reshape(N_HEADS, HEAD_DIM, dec_b, dec_t)
        q4 = jnp.pad(q4, ((0, 0), (0, 0), (0, 0), (0, tq_s - dec_t)))
        qb = jnp.einsum("hdbt,gh->bgdht", q4.astype(F32), head_of_row).astype(BF16)
        qblk_s = qb.reshape(dec_b, KV_WIDTH, N_HEADS * tq_s)
        ck = _compress_paged(page_table, pools_t[0], l, *lw["ck_w"], transpose_out=False)
        cvt = _compress_paged(page_table, pools_t[1], l, *lw["cv_w"], transpose_out=True)
        oc, _, sbt = _cmp_select(qblk_s, ck, cvt, mt_s, tl_s, tq=tq_s, nq_per_b=1, n_bias_steps=n_sel_steps)
        new_t = lambda t: jnp.pad(t.reshape(dec_b, dec_t, KV_WIDTH).transpose(0, 2, 1), ((0, 0), (0, 0), (0, page - dec_t)))
        osel_rows = _sel_paged(page_table, jnp.swapaxes(qblk_s, 1, 2), sbt, ohc_s, trow_s, new_t(ks), new_t(vs),
                               pools_t[2], pools_t[3], l, past_len=past_len)
        osel = jnp.einsum("bhtgd,gh->bhdt", osel_rows.reshape(dec_b, N_HEADS, tq_s, N_KV_HEADS, HEAD_DIM),
                          head_of_row).reshape(dec_b, N_HEADS * HEAD_DIM, tq_s)
        kw_all = jnp.concatenate([flat(cache_win_k[l]), kw.reshape(dec_b, dec_t, KV_WIDTH)], axis=1)
        vw_all = jnp.concatenate([flat(cache_win_v[l]), vw.reshape(dec_b, dec_t, KV_WIDTH)], axis=1)
        wpad = ((0, 0), (0, band - wbuf - dec_t), (0, 0))
        ow = _window(qblk_s, jnp.pad(kw_all, wpad), jnp.pad(vw_all, wpad), tl_s, tq=tq_s, band=band, kpos_base=past_len - wbuf)
        untile = lambda o: o[:, :, :dec_t].transpose(1, 0, 2).reshape(1, N_HEADS * HEAD_DIM, n_s)
        xs = _combine(xs, a, untile(oc), untile(osel), untile(ow), gt, lw["e"], lw["woa"], lw["wob"], tm=CHUNK)
        kv4s = lambda t: t.reshape(dec_b, dec_t, N_KV_HEADS, HEAD_DIM)
        for idx, t in enumerate((kc, vc, ks, vs)):
            outs[6 + idx].append(kv4s(t))
        nwin_s = min(WINDOW, wbuf + dec_t)
        outs[10].append(kw_all[:, wbuf + dec_t - nwin_s:].reshape(dec_b, nwin_s, N_KV_HEADS, HEAD_DIM))
        outs[11].append(vw_all[:, wbuf + dec_t - nwin_s:].reshape(dec_b, nwin_s, N_KV_HEADS, HEAD_DIM))
        outs[12].append(v_s.reshape(dec_b, dec_t, A_GROUPS * CHUNK))

        gf = g_final[None, :] if l == depth - 1 else None
        xp = _channel_mixer(l, xp, g_ffn, dense_w, w_router, moe_w, gf, tm_dense=512, tm_moe=512)
        xs = _channel_mixer(l, xs, g_ffn, dense_w, w_router, moe_w, gf, tm_dense=CHUNK, tm_moe=CHUNK)

    y_prompt = xp.reshape(bsz, seq, dm)
    y_sample = xs.reshape(dec_b, dec_t, dm)
    return (y_prompt, y_sample) + tuple(jnp.stack(o, axis=0) for o in outs)
```
